```python
import jax, jax.numpy as jnp
from jax import lax
import numpy as np


D_MODEL = 2048
BATCH = 4
SEQ = 4096
DEPTH = 4

BRANCH_WIDTH = D_MODEL // 2
N_BRANCH = 3
HEAD_A = 64
N_HEADS_A = BRANCH_WIDTH // HEAD_A
LORA_DECAY = 64
LORA_AAA = 64
LORA_GATE = 160
DECAY_SCALE = 0.606531
GN_EPS = 64e-5
HEAD_FB = 128
HEAD_IB = 128
N_HEADS_B = BRANCH_WIDTH // HEAD_IB
CHUNK = 64
F_TINY = 1e-30
QK_NOPE = 128
QK_ROPE = 64
V_HEAD = 128
N_HEADS_C = BRANCH_WIDTH // V_HEAD
Q_LORA = 3 * D_MODEL // 8
KV_LORA = D_MODEL // 4
Q_BLOCK = 128
ROPE_THETA = 10000.0
D_FF = 4 * D_MODEL
PLE_DIM = 256
NORM_EPS = 1e-6

RWKV_W = 3 * BRANCH_WIDTH + 2 * LORA_DECAY + 2 * LORA_AAA + LORA_GATE
HGRN_W = 5 * BRANCH_WIDTH
MLA_W = Q_LORA + KV_LORA + QK_ROPE
GATE_W = N_BRANCH * D_MODEL
IN_W = RWKV_W + HGRN_W + MLA_W + GATE_W

kernel_name = 'hybrid_rwkv7_hgrn2_mla_encoder'


def _split(t, sizes):
    out, start = [], 0
    for s in sizes:
        out.append(t[..., start:start + s])
        start += s
    return out


def rms_norm(t, g):
    tf = t.astype(jnp.float32)
    tf = tf * lax.rsqrt(jnp.mean(tf * tf, axis=-1, keepdims=True) + NORM_EPS)
    return (tf * g.astype(jnp.float32)).astype(t.dtype)


def centred_shift(u, mu):
    prev = jnp.pad(u, ((0, 0), (1, 0), (0, 0)))[:, :-1]
    nxt = jnp.pad(u, ((0, 0), (0, 1), (0, 0)))[:, 1:]
    return u + mu[0] * (prev - u) + mu[1] * (nxt - u)


def rwkv7_scan(r, w, k, v, kk, a, reverse):
    Bn, S, H, N = r.shape

    def step(state, inp):
        r_t, w_t, k_t, v_t, kk_t, a_t = inp
        sa = jnp.einsum('bhvk,bhk->bhv', state, -kk_t)
        state = (state * w_t[:, :, None, :]
                 + sa[..., None] * (kk_t * a_t)[:, :, None, :]
                 + v_t[..., None] * k_t[:, :, None, :])
        return state, jnp.einsum('bhvk,bhk->bhv', state, r_t)

    xs = tuple(jnp.moveaxis(t, 1, 0) for t in (r, w, k, v, kk, a))
    init = jnp.zeros((Bn, H, N, N), jnp.float32)
    _, ys = lax.scan(step, init, xs, reverse=reverse)
    return jnp.moveaxis(ys, 0, 1)


def rwkv7_mixer(u, mu, w0, w2, a0, a2, g2, k_k, k_a, r_k, gn_w, gn_b):
    dt = u.dtype
    Bn, S, _ = u.shape
    f32 = jnp.float32
    u = centred_shift(u.astype(f32), mu.astype(f32))
    r, k, v, wd, ad, gd = _split(u, [BRANCH_WIDTH] * 3 + [2 * LORA_DECAY, 2 * LORA_AAA, LORA_GATE])
    wd = wd.reshape(Bn, S, 2, LORA_DECAY)
    ad = ad.reshape(Bn, S, 2, LORA_AAA)
    w_raw = w0 + jnp.einsum('bsnl,nlc->bsnc', jnp.tanh(wd), w2)
    decay = jnp.exp(-DECAY_SCALE * jax.nn.sigmoid(w_raw))
    a = jax.nn.sigmoid(a0 + jnp.einsum('bsnl,nlc->bsnc', ad, a2))
    g = jax.nn.sigmoid(gd) @ g2
    heads = lambda t: t.reshape(Bn, S, N_HEADS_A, HEAD_A)
    kk = heads(k * k_k)
    kk = kk / jnp.maximum(jnp.sqrt(jnp.sum(kk * kk, axis=-1, keepdims=True)), 1e-12)
    kd = k[:, :, None, :] * (1.0 + (a - 1.0) * k_a)
    rh, vh = heads(r), heads(v)
    y = (rwkv7_scan(rh, heads(decay[:, :, 0]), heads(kd[:, :, 0]), vh, kk, heads(a[:, :, 0]), False)
         + rwkv7_scan(rh, heads(decay[:, :, 1]), heads(kd[:, :, 1]), vh, kk, heads(a[:, :, 1]), True))
    mean = jnp.mean(y, axis=-1, keepdims=True)
    var = jnp.mean(jnp.square(y - mean), axis=-1, keepdims=True)
    y = ((y - mean) * lax.rsqrt(var + GN_EPS)).reshape(Bn, S, BRANCH_WIDTH) * gn_w + gn_b
    bonus = jnp.sum(rh * heads(kd[:, :, 0] + kd[:, :, 1]) * r_k, axis=-1, keepdims=True) * vh
    y = y + bonus.reshape(Bn, S, BRANCH_WIDTH)
    return (y * g).astype(dt)


def gla_chunk_scan(q, k, v, logf):
    Bn, H, S, DK = q.shape
    DV = v.shape[-1]
    nc = S // CHUNK
    chunks = lambda t: jnp.moveaxis(t.reshape(Bn, H, nc, CHUNK, t.shape[-1]), 2, 0)
    lower = jnp.tril(jnp.ones((CHUNK, CHUNK), dtype=bool))[:, :, None]

    def step(state, inp):
        qc, kc, vc, gc = inp
        b = jnp.cumsum(gc, axis=2)
        b_last = b[:, :, -1:, :]
        inter = jnp.einsum('bhtk,bhkv->bhtv', qc * jnp.exp(b), state)
        diff = b[:, :, :, None, :] - b[:, :, None, :, :]
        rel = jnp.where(lower, jnp.exp(jnp.where(lower, diff, 0.0)), 0.0)
        scores = jnp.einsum('bhtk,bhsk,bhtsk->bhts', qc, kc, rel)
        intra = jnp.einsum('bhts,bhsv->bhtv', scores, vc)
        state = (state * jnp.exp(b_last)[:, :, 0, :, None]
                 + jnp.einsum('bhsk,bhsv->bhkv', kc * jnp.exp(b_last - b), vc))
        return state, inter + intra

    init = jnp.zeros((Bn, H, DK, DV), jnp.float32)
    _, out = lax.scan(step, init, (chunks(q), chunks(k), chunks(v), chunks(logf)))
    return jnp.moveaxis(out, 0, 2).reshape(Bn, H, S, DV)


def hgrn2_mixer(u, lb, norm_g):
    dt = u.dtype
    Bn, S, _ = u.shape
    q, z_f, z_b, i, g = jnp.split(u.astype(jnp.float32), 5, axis=-1)
    q = jax.nn.silu(q)
    lb = lb.astype(jnp.float32)

    def gate(zz):
        f = lb + (1.0 - lb) * jax.nn.sigmoid(zz)
        return (1.0 - lb) * jax.nn.sigmoid(-zz), jnp.log(jnp.maximum(f, F_TINY))

    k_f, logf_f = gate(z_f)
    k_b, logf_b = gate(z_b)
    heads = lambda t: t.reshape(Bn, S, N_HEADS_B, -1).transpose(0, 2, 1, 3)
    flip = lambda t: jnp.flip(t, axis=2)
    qh, ih = heads(q), heads(i)
    o = (gla_chunk_scan(qh, heads(k_f), ih, heads(logf_f))
         + flip(gla_chunk_scan(flip(qh), flip(heads(k_b)), flip(ih), flip(heads(logf_b)))))
    o = o.transpose(0, 2, 1, 3)
    o = rms_norm(o, norm_g) * jax.nn.silu(g.reshape(Bn, S, N_HEADS_B, HEAD_IB))
    return o.reshape(Bn, S, BRANCH_WIDTH).astype(dt)


def apply_rope(t, cos, sin):
    t1, t2 = jnp.split(t.astype(jnp.float32), 2, axis=-1)
    return jnp.concatenate([t1 * cos - t2 * sin, t2 * cos + t1 * sin], axis=-1).astype(t.dtype)


def mla_mixer(u, positions, q_norm_g, kv_norm_g, w_uq, w_ukv):
    Bn, S, _ = u.shape
    H = N_HEADS_C
    cq, ckv, k_rope = _split(u, [Q_LORA, KV_LORA, QK_ROPE])
    q = (rms_norm(cq, q_norm_g) @ w_uq).reshape(Bn, S, H, QK_NOPE + QK_ROPE)
    kv = (rms_norm(ckv, kv_norm_g) @ w_ukv).reshape(Bn, S, H, QK_NOPE + V_HEAD)
    q_nope, q_rope = q[..., :QK_NOPE], q[..., QK_NOPE:]
    k_nope, v = kv[..., :QK_NOPE], kv[..., QK_NOPE:]
    inv_freq = 1.0 / (ROPE_THETA ** (jnp.arange(0, QK_ROPE, 2, dtype=jnp.float32) / QK_ROPE))
    ang = positions.astype(jnp.float32)[..., None] * inv_freq
    cos, sin = jnp.cos(ang), jnp.sin(ang)
    q_rope = apply_rope(q_rope, cos[:, :, None], sin[:, :, None])
    k_rope = apply_rope(k_rope, cos, sin)
    scale = (QK_NOPE + QK_ROPE) ** -0.5
    nb = S // Q_BLOCK
    blocks = lambda t: jnp.moveaxis(t.reshape(Bn, nb, Q_BLOCK, H, t.shape[-1]), 1, 0)

    def attend(qb):
        qn, qr = qb
        s = (jnp.einsum('bqhd,bkhd->bhqk', qn, k_nope)
             + jnp.einsum('bqhd,bkd->bhqk', qr, k_rope))
        w = jax.nn.softmax(s.astype(jnp.float32) * scale, axis=-1).astype(v.dtype)
        return jnp.einsum('bhqk,bkhd->bqhd', w, v)

    o = lax.map(attend, (blocks(q_nope), blocks(q_rope)))
    return jnp.moveaxis(o, 0, 1).reshape(Bn, S, H * V_HEAD)


def setup_inputs(seed: int = 0) -> dict:
    key = jax.random.key(seed)
    ks = jax.random.split(key, 32)
    L = DEPTH
    nrm = lambda k, shape, s: jax.random.normal(k, shape, jnp.float32) * s
    gain = lambda k, shape: 1.0 + 0.02 * jax.random.normal(k, shape, jnp.float32)
    return {
        'x': nrm(ks[0], (BATCH, SEQ, D_MODEL), 1.0),
        'p': nrm(ks[1], (DEPTH, BATCH, SEQ, PLE_DIM), 1.0),
        'positions': (jnp.arange(SEQ, dtype=jnp.int32)[None, :]
                      + jax.random.randint(ks[2], (BATCH, 1), 0, SEQ, dtype=jnp.int32)),
        'ln1_g': gain(ks[3], (L, D_MODEL)),
        'w_in': nrm(ks[4], (L, D_MODEL, IN_W), D_MODEL ** -0.5),
        'rwkv_mu': jax.random.uniform(ks[5], (L, 2, RWKV_W), jnp.float32, 0.0, 0.5),
        'rwkv_w0': nrm(ks[6], (L, 2, BRANCH_WIDTH), 1.0),
        'rwkv_w2': nrm(ks[7], (L, 2, LORA_DECAY, BRANCH_WIDTH), 0.5 * LORA_DECAY ** -0.5),
        'rwkv_a0': nrm(ks[8], (L, 2, BRANCH_WIDTH), 0.5),
        'rwkv_a2': nrm(ks[9], (L, 2, LORA_AAA, BRANCH_WIDTH), 0.5 * LORA_AAA ** -0.5),
        'rwkv_g2': nrm(ks[10], (L, LORA_GATE, BRANCH_WIDTH), LORA_GATE ** -0.5),
        'rwkv_kk': 1.0 + nrm(ks[11], (L, BRANCH_WIDTH), 0.1),
        'rwkv_ka': 1.0 + nrm(ks[12], (L, BRANCH_WIDTH), 0.1),
        'rwkv_rk': nrm(ks[13], (L, N_HEADS_A, HEAD_A), 0.1),
        'rwkv_gn_w': gain(ks[14], (L, BRANCH_WIDTH)),
        'rwkv_gn_b': nrm(ks[15], (L, BRANCH_WIDTH), 0.02),
        'hgrn_lb': nrm(ks[16], (L, BRANCH_WIDTH), 0.1),
        'hgrn_norm_g': gain(ks[17], (L, HEAD_IB)),
        'mla_q_norm_g': gain(ks[18], (L, Q_LORA)),
        'mla_kv_norm_g': gain(ks[19], (L, KV_LORA)),
        'mla_w_uq': nrm(ks[20], (L, Q_LORA, N_HEADS_C * (QK_NOPE + QK_ROPE)), Q_LORA ** -0.5),
        'mla_w_ukv': nrm(ks[21], (L, KV_LORA, N_HEADS_C * (QK_NOPE + V_HEAD)), KV_LORA ** -0.5),
        'w_branch': nrm(ks[22], (L, N_BRANCH, BRANCH_WIDTH, D_MODEL), BRANCH_WIDTH ** -0.5),
        'w_o': nrm(ks[23], (L, D_MODEL, D_MODEL), D_MODEL ** -0.5),
        'ln2_g': gain(ks[24], (L, D_MODEL)),
        'w_mlp1': nrm(ks[25], (L, D_MODEL, D_FF), D_MODEL ** -0.5),
        'w_mlp2': nrm(ks[26], (L, D_FF, D_MODEL), D_FF ** -0.5),
        'w_pe': nrm(ks[27], (L, PLE_DIM, D_MODEL), PLE_DIM ** -0.5),
        'w_pg': nrm(ks[28], (L, D_MODEL, D_MODEL), D_MODEL ** -0.5),
        'final_g': gain(ks[29], (D_MODEL,)),
    }


def reference(x, p, positions, ln1_g, w_in, rwkv_mu, rwkv_w0, rwkv_w2, rwkv_a0, rwkv_a2,
              rwkv_g2, rwkv_kk, rwkv_ka, rwkv_rk, rwkv_gn_w, rwkv_gn_b, hgrn_lb, hgrn_norm_g,
              mla_q_norm_g, mla_kv_norm_g, mla_w_uq, mla_w_ukv, w_branch, w_o, ln2_g,
              w_mlp1, w_mlp2, w_pe, w_pg, final_g):
    Bn, S, D = x.shape
    lb_w = jax.nn.softmax(hgrn_lb.astype(jnp.float32), axis=0)
    lower_bounds = jnp.cumsum(lb_w, axis=0) - lb_w[0]
    h = x
    for l in range(DEPTH):
        hn = rms_norm(h, ln1_g[l])
        z = hn @ w_in[l]
        u_a, u_b, u_c, u_g = _split(z, [RWKV_W, HGRN_W, MLA_W, GATE_W])
        y_a = rwkv7_mixer(u_a, rwkv_mu[l], rwkv_w0[l], rwkv_w2[l], rwkv_a0[l], rwkv_a2[l],
                          rwkv_g2[l], rwkv_kk[l], rwkv_ka[l], rwkv_rk[l], rwkv_gn_w[l], rwkv_gn_b[l])
        y_b = hgrn2_mixer(u_b, lower_bounds[l], hgrn_norm_g[l])
        y_c = mla_mixer(u_c, positions, mla_q_norm_g[l], mla_kv_norm_g[l], mla_w_uq[l], mla_w_ukv[l])
        gates = jax.nn.sigmoid(u_g).reshape(Bn, S, N_BRANCH, D)
        mixed = (gates[:, :, 0] * (y_a @ w_branch[l, 0])
                 + gates[:, :, 1] * (y_b @ w_branch[l, 1])
                 + gates[:, :, 2] * (y_c @ w_branch[l, 2]))
        h = h + mixed @ w_o[l]
        hn = rms_norm(h, ln2_g[l])
        h = h + jnp.square(jax.nn.relu(hn @ w_mlp1[l])) @ w_mlp2[l]
        h = h + jax.nn.sigmoid(h @ w_pg[l]) * (p[l] @ w_pe[l])
    return rms_norm(h, final_g)
```

```python
import functools

import numpy as np
import jax
import jax.numpy as jnp
from jax import lax
from jax.experimental import pallas as pl
from jax.experimental.pallas import tpu as pltpu

F32 = jnp.float32
BF16 = jnp.bfloat16

LANE = 128
VMEM_LIMIT = 48 * 2**20

HEAD_A = 64
LORA_DECAY = 64
LORA_AAA = 64
LORA_GATE = 160
DECAY_SCALE = 0.606531
GN_EPS = 64e-5
HEAD_B = 128
F_TINY = 1e-30
QK_NOPE = 128
QK_ROPE = 64
V_HEAD = 128
ROPE_THETA = 10000.0
NORM_EPS = 1e-6
CHUNK = 64


def _cparams(*sem):
    return pltpu.CompilerParams(dimension_semantics=sem, vmem_limit_bytes=VMEM_LIMIT)


def _sigmoid(x):
    return 1.0 / (1.0 + jnp.exp(-x))


def _dot(a, b):
    return jnp.dot(a, b, preferred_element_type=F32)


def _dot_nt(a, b):
    return lax.dot_general(a, b, (((1,), (1,)), ((), ())), preferred_element_type=F32)


def _dot_tn(a, b):
    return lax.dot_general(a, b, (((0,), (0,)), ((), ())), preferred_element_type=F32)


def _split3(x):
    x1 = x.astype(BF16)
    r1 = x - x1.astype(F32)
    x2 = r1.astype(BF16)
    x3 = (r1 - x2.astype(F32)).astype(BF16)
    return x1, x2, x3


def _rmsnorm_kernel(x_ref, g_ref, o_ref):
    x = x_ref[...]
    ms = jnp.mean(x * x, axis=-1, keepdims=True)
    o_ref[...] = (x * lax.rsqrt(ms + NORM_EPS) * g_ref[...]).astype(o_ref.dtype)


def rmsnorm(x, g, out_dtype, tm=512):
    T, D = x.shape
    return pl.pallas_call(
        _rmsnorm_kernel,
        out_shape=jax.ShapeDtypeStruct((T, D), out_dtype),
        grid=(T // tm,),
        in_specs=[pl.BlockSpec((tm, D), lambda i: (i, 0)),
                  pl.BlockSpec((1, D), lambda i: (0, 0))],
        out_specs=pl.BlockSpec((tm, D), lambda i: (i, 0)),
        compiler_params=_cparams("parallel"),
        name="rmsnorm",
    )(x, g.reshape(1, D))


def _mm_kernel(a_ref, w_ref, *rest, nk, epilogue, n_extra):
    extras = rest[:n_extra]
    o_ref = rest[n_extra]

    def finish(acc):
        o_ref[...] = epilogue(acc, *[e[...] for e in extras]).astype(o_ref.dtype)

    if nk == 1:
        finish(_dot(a_ref[...], w_ref[...]))
    else:
        acc_ref = rest[n_extra + 1]
        k = pl.program_id(2)

        @pl.when(k == 0)
        def _():
            acc_ref[...] = jnp.zeros_like(acc_ref)

        acc_ref[...] += _dot(a_ref[...], w_ref[...])

        @pl.when(k == nk - 1)
        def _():
            finish(acc_ref[...])


def matmul(a, w, *, out_dtype, tm, tn, tk=None, epilogue=None, extras=(), alias_extra=None,
           name="matmul"):
    M, K = a.shape
    N = w.shape[1]
    tk = K if tk is None else tk
    nk = K // tk
    epilogue = epilogue or (lambda acc: acc)
    kern = functools.partial(_mm_kernel, nk=nk, epilogue=epilogue, n_extra=len(extras))
    in_specs = [pl.BlockSpec((tm, tk), lambda i, j, k: (i, k)),
                pl.BlockSpec((tk, tn), lambda i, j, k: (k, j))]
    in_specs += [pl.BlockSpec((tm, tn), lambda i, j, k: (i, j)) for _ in extras]
    aliases = {} if alias_extra is None else {2 + alias_extra: 0}
    return pl.pallas_call(
        kern,
        out_shape=jax.ShapeDtypeStruct((M, N), out_dtype),
        grid=(M // tm, N // tn, nk),
        in_specs=in_specs,
        out_specs=pl.BlockSpec((tm, tn), lambda i, j, k: (i, j)),
        scratch_shapes=[pltpu.VMEM((tm, tn), F32)] if nk > 1 else [],
        input_output_aliases=aliases,
        compiler_params=_cparams("parallel", "parallel", "arbitrary"),
        name=name,
    )(a, w, *extras)


def _branch_kernel(ya_ref, yb_ref, yc_ref, p_ref, ga_ref, gb_ref, gc_ref, o_ref):
    acc = _sigmoid(ga_ref[...]) * _dot(ya_ref[...], p_ref[0])
    acc += _sigmoid(gb_ref[...]) * _dot(yb_ref[...], p_ref[1])
    acc += _sigmoid(gc_ref[...]) * _dot(yc_ref[...], p_ref[2])
    o_ref[...] = acc.astype(o_ref.dtype)


def branch_mix(ya, yb, yc, p, zg, *, tm=1024, tn=512):
    T, W = ya.shape
    D = p.shape[2]
    nj = D // tn
    y_spec = pl.BlockSpec((tm, W), lambda i, j: (i, 0))
    g_specs = [pl.BlockSpec((tm, tn), functools.partial(lambda i, j, n: (i, n * nj + j), n=n))
               for n in range(3)]
    return pl.pallas_call(
        _branch_kernel,
        out_shape=jax.ShapeDtypeStruct((T, D), BF16),
        grid=(T // tm, nj),
        in_specs=[y_spec, y_spec, y_spec,
                  pl.BlockSpec((3, W, tn), lambda i, j: (0, 0, j))] + g_specs,
        out_specs=pl.BlockSpec((tm, tn), lambda i, j: (i, j)),
        compiler_params=_cparams("parallel", "parallel"),
        name="branch_mix",
    )(ya, yb, yc, p, zg, zg, zg)


def _ple_kernel(h_ref, wpg_ref, p_ref, wpe_ref, hres_ref, o_ref):
    gate = _sigmoid(_dot(h_ref[...].astype(BF16), wpg_ref[...]))
    emb = _dot(p_ref[...].astype(BF16), wpe_ref[...])
    o_ref[...] = hres_ref[...] + gate * emb


def ple_update(h, wpg, p, wpe, *, tm=512, tn=512):
    T, D = h.shape
    E = p.shape[1]
    return pl.pallas_call(
        _ple_kernel,
        out_shape=jax.ShapeDtypeStruct((T, D), F32),
        grid=(T // tm, D // tn),
        in_specs=[pl.BlockSpec((tm, D), lambda i, j: (i, 0)),
                  pl.BlockSpec((D, tn), lambda i, j: (0, j)),
                  pl.BlockSpec((tm, E), lambda i, j: (i, 0)),
                  pl.BlockSpec((E, tn), lambda i, j: (0, j)),
                  pl.BlockSpec((tm, tn), lambda i, j: (i, j))],
        out_specs=pl.BlockSpec((tm, tn), lambda i, j: (i, j)),
        compiler_params=_cparams("parallel", "parallel"),
        name="ple_update",
    )(h, wpg, p, wpe, h)


def _mla_q_kernel(cq_ref, g_ref, wa_ref, wb_ref, c_ref, s_ref, o_ref, xn_ref):
    @pl.when(pl.program_id(1) == 0)
    def _():
        x = cq_ref[...]
        ms = jnp.mean(x * x, axis=-1, keepdims=True)
        xn_ref[...] = (x * lax.rsqrt(ms + NORM_EPS) * g_ref[...]).astype(BF16)

    xn = xn_ref[...]
    o_ref[...] = (_dot(xn, wa_ref[...]) * c_ref[...] + _dot(xn, wb_ref[...]) * s_ref[...]).astype(BF16)


def mla_q_proj(cq, g, wa, wb, ctab, stab, n_heads, *, tm=512):
    T, R = cq.shape
    HW = wa.shape[1] // n_heads
    return pl.pallas_call(
        _mla_q_kernel,
        out_shape=jax.ShapeDtypeStruct((T, n_heads * HW), BF16),
        grid=(T // tm, n_heads),
        in_specs=[pl.BlockSpec((tm, R), lambda i, h: (i, 0)),
                  pl.BlockSpec((1, R), lambda i, h: (0, 0)),
                  pl.BlockSpec((R, HW), lambda i, h: (0, h)),
                  pl.BlockSpec((R, HW), lambda i, h: (0, h)),
                  pl.BlockSpec((tm, HW), lambda i, h: (i, 0)),
                  pl.BlockSpec((tm, HW), lambda i, h: (i, 0))],
        out_specs=pl.BlockSpec((tm, HW), lambda i, h: (i, h)),
        scratch_shapes=[pltpu.VMEM((tm, R), BF16)],
        compiler_params=_cparams("parallel", "arbitrary"),
        name="mla_q_proj",
    )(cq, g.reshape(1, R), wa, wb, ctab, stab)


def _mla_kv_kernel(z_ref, g_ref, w_ref, c_ref, s_ref, k_ref, v_ref, xn_ref, *, kv_lora):
    @pl.when(pl.program_id(1) == 0)
    def _():
        x = z_ref[:, :kv_lora]
        ms = jnp.mean(x * x, axis=-1, keepdims=True)
        xn_ref[...] = (x * lax.rsqrt(ms + NORM_EPS) * g_ref[...]).astype(BF16)

    kv = _dot(xn_ref[...], w_ref[...])
    kr = z_ref[:, kv_lora:kv_lora + LANE] * c_ref[...] + z_ref[:, kv_lora + LANE:] * s_ref[...]
    k_ref[:, :QK_NOPE] = kv[:, :QK_NOPE].astype(BF16)
    k_ref[:, QK_NOPE:] = kr.astype(BF16)
    v_ref[...] = kv[:, QK_NOPE:].astype(BF16)


def mla_kv_proj(zkv, g, w, ctab, stab, n_heads, *, tm=512):
    T, ZW = zkv.shape
    R = ZW - 2 * LANE
    return pl.pallas_call(
        functools.partial(_mla_kv_kernel, kv_lora=R),
        out_shape=(jax.ShapeDtypeStruct((T, n_heads * 2 * LANE), BF16),
                   jax.ShapeDtypeStruct((T, n_heads * V_HEAD), BF16)),
        grid=(T // tm, n_heads),
        in_specs=[pl.BlockSpec((tm, ZW), lambda i, h: (i, 0)),
                  pl.BlockSpec((1, R), lambda i, h: (0, 0)),
                  pl.BlockSpec((R, QK_NOPE + V_HEAD), lambda i, h: (0, h)),
                  pl.BlockSpec((tm, LANE), lambda i, h: (i, 0)),
                  pl.BlockSpec((tm, LANE), lambda i, h: (i, 0))],
        out_specs=(pl.BlockSpec((tm, 2 * LANE), lambda i, h: (i, h)),
                   pl.BlockSpec((tm, V_HEAD), lambda i, h: (i, h))),
        scratch_shapes=[pltpu.VMEM((tm, R), BF16)],
        compiler_params=_cparams("parallel", "arbitrary"),
        name="mla_kv_proj",
    )(zkv, g.reshape(1, R), w, ctab, stab)


def _attn_kernel(q_ref, k_ref, v_ref, o_ref):
    s = _dot_nt(q_ref[...], k_ref[...])
    m = jnp.max(s, axis=-1, keepdims=True)
    p = jnp.exp(s - m)
    l = jnp.sum(p, axis=-1, keepdims=True)
    o = _dot(p.astype(BF16), v_ref[...])
    o_ref[...] = (o / l).astype(o_ref.dtype)


def mla_attention(q, k, v, batch, n_heads, *, tq=256):
    T = q.shape[0]
    S = T // batch
    nq = S // tq
    QW = q.shape[1] // n_heads
    return pl.pallas_call(
        _attn_kernel,
        out_shape=jax.ShapeDtypeStruct((T, n_heads * V_HEAD), BF16),
        grid=(batch, n_heads, nq),
        in_specs=[pl.BlockSpec((tq, QW), lambda b, h, i: (b * nq + i, h)),
                  pl.BlockSpec((S, QW), lambda b, h, i: (b, h)),
                  pl.BlockSpec((S, V_HEAD), lambda b, h, i: (b, h))],
        out_specs=pl.BlockSpec((tq, V_HEAD), lambda b, h, i: (b * nq + i, h)),
        compiler_params=_cparams("parallel", "parallel", "arbitrary"),
        name="mla_attention",
    )(q, k, v)


def _rope_tables(positions):
    inv_freq = 1.0 / (ROPE_THETA ** (jnp.arange(0, QK_ROPE, 2, dtype=F32) / QK_ROPE))
    ang = positions.astype(F32).reshape(-1, 1) * inv_freq
    cos, sin = jnp.cos(ang), jnp.sin(ang)
    T = ang.shape[0]
    scale = (QK_NOPE + QK_ROPE) ** -0.5
    z64 = jnp.zeros((T, LANE - QK_ROPE), F32)
    ck = jnp.concatenate([cos, cos, z64], axis=1)
    sk = jnp.concatenate([-sin, sin, z64], axis=1)
    cq = jnp.concatenate([jnp.ones((T, QK_NOPE), F32), ck], axis=1) * scale
    sq = jnp.concatenate([jnp.zeros((T, QK_NOPE), F32), sk], axis=1) * scale
    return cq, sq, ck, sk


def _mla_q_weights(w_uq, n_heads):
    R = w_uq.shape[0]
    w = w_uq.reshape(R, n_heads, QK_NOPE + QK_ROPE)
    half = QK_ROPE // 2
    zpad = jnp.zeros((R, n_heads, LANE - QK_ROPE), w.dtype)
    wa = jnp.concatenate([w, zpad], axis=2)
    wb = jnp.concatenate([jnp.zeros((R, n_heads, QK_NOPE), w.dtype),
                          w[:, :, QK_NOPE + half:], w[:, :, QK_NOPE:QK_NOPE + half], zpad], axis=2)
    return (wa.reshape(R, n_heads * 2 * LANE).astype(BF16),
            wb.reshape(R, n_heads * 2 * LANE).astype(BF16))


N_LEVELS = 6


def _gla_constants():
    C = CHUNK
    t = np.arange(C)[:, None]
    u = np.arange(C)[None, :]
    dd = np.zeros((N_LEVELS + 3, C, C), np.float32)
    mk = np.zeros((N_LEVELS + 1, C, C), np.float32)
    for l in range(N_LEVELS):
        m = C >> (l + 1)
        mid = (t // (2 * m)) * (2 * m) + m - 1
        second = (t % (2 * m)) >= m
        dd[l] = np.where(second, (u > mid) & (u <= t), (u > t) & (u <= mid))
        mk[l] = (t // (2 * m) == u // (2 * m)) & second & ((u % (2 * m)) < m)
    dd[N_LEVELS] = u <= t
    dd[N_LEVELS + 1] = u > t
    dd[N_LEVELS + 2] = 1.0
    mk[N_LEVELS] = t == u
    dd_b = dd[:, ::-1, ::-1]
    mk_b = mk[:, ::-1, ::-1]
    to2d = lambda a: np.ascontiguousarray(a).reshape(-1, C)
    return to2d(dd), to2d(dd_b), to2d(mk), to2d(mk_b)


def _gla_chunk(q, k, v, g, st, dd, mk):
    C = CHUNK
    g1, g2, g3 = _split3(g)
    e = _dot(dd, g1) + _dot(dd, g2) + _dot(dd, g3)
    p = jnp.exp(jnp.minimum(e, 0.0))
    kb = k.astype(BF16)
    scores = jnp.where(mk[N_LEVELS * C:(N_LEVELS + 1) * C] > 0, _dot_nt(q.astype(BF16), kb), 0.0)
    for l in range(N_LEVELS):
        pl_ = p[l * C:(l + 1) * C]
        a = _dot_nt((q * pl_).astype(BF16), (k * pl_).astype(BF16))
        scores += jnp.where(mk[l * C:(l + 1) * C] > 0, a, 0.0)
    p_in = p[N_LEVELS * C:(N_LEVELS + 1) * C]
    p_out = p[(N_LEVELS + 1) * C:(N_LEVELS + 2) * C]
    p_tot = p[(N_LEVELS + 2) * C:(N_LEVELS + 2) * C + 1]
    vb = v.astype(BF16)
    o = _dot_nt((q * p_in).astype(BF16), st.astype(BF16)) + _dot(scores.astype(BF16), vb)
    st_new = st * p_tot + _dot_tn(vb, (k * p_out).astype(BF16))
    return o, st_new


def _hgrn_kernel(q_ref, zf_ref, zb_ref, i_ref, g_ref, lb_ref, ng_ref, ddf_ref, ddb_ref, mkf_ref, mkb_ref,
                 o_ref, of_ref, ob_ref, *, n_chunks):
    C = CHUNK
    lb = lb_ref[...]
    one_m_lb = 1.0 - lb
    ddf, ddb, mkf, mkb = ddf_ref[...], ddb_ref[...], mkf_ref[...], mkb_ref[...]

    def gate(zz):
        f = lb + one_m_lb * _sigmoid(zz)
        return one_m_lb * _sigmoid(-zz), jnp.log(jnp.maximum(f, F_TINY))

    def load(ref, c):
        return ref[pl.ds(pl.multiple_of(c * C, C), C), :]

    def body(c, carry):
        st_f, st_b = carry
        cb = n_chunks - 1 - c
        qf = load(q_ref, c)
        qf = qf * _sigmoid(qf)
        kf, gf = gate(load(zf_ref, c))
        o, st_f = _gla_chunk(qf, kf, load(i_ref, c), gf, st_f, ddf, mkf)
        of_ref[pl.ds(pl.multiple_of(c * C, C), C), :] = o
        qb = load(q_ref, cb)
        qb = qb * _sigmoid(qb)
        kb, gb = gate(load(zb_ref, cb))
        o, st_b = _gla_chunk(qb, kb, load(i_ref, cb), gb, st_b, ddb, mkb)
        ob_ref[pl.ds(pl.multiple_of(cb * C, C), C), :] = o
        return st_f, st_b

    dv, dk = i_ref.shape[1], q_ref.shape[1]
    z = jnp.zeros((dv, dk), F32)
    lax.fori_loop(0, n_chunks, body, (z, z))

    o = of_ref[...] + ob_ref[...]
    ms = jnp.mean(o * o, axis=-1, keepdims=True)
    gg = g_ref[...]
    o_ref[...] = (o * lax.rsqrt(ms + NORM_EPS) * ng_ref[...] * (gg * _sigmoid(gg))).astype(o_ref.dtype)


def hgrn2_mixer(z, lb, norm_g, batch, *, col0=0):
    T = z.shape[0]
    W = lb.shape[-1]
    S = T // batch
    H = W // HEAD_B
    c0 = col0 // HEAD_B
    ddf, ddb, mkf, mkb = _gla_constants()
    part = lambda n: pl.BlockSpec((S, HEAD_B), functools.partial(lambda b, h, n: (b, c0 + n * H + h), n=n))
    const = lambda a: pl.BlockSpec(a.shape, lambda b, h: (0, 0))
    return pl.pallas_call(
        functools.partial(_hgrn_kernel, n_chunks=S // CHUNK),
        out_shape=jax.ShapeDtypeStruct((T, W), BF16),
        grid=(batch, H),
        in_specs=[part(0), part(1), part(2), part(3), part(4),
                  pl.BlockSpec((1, HEAD_B), lambda b, h: (0, h)),
                  pl.BlockSpec((1, HEAD_B), lambda b, h: (0, 0)),
                  const(ddf), const(ddb), const(mkf), const(mkb)],
        out_specs=pl.BlockSpec((S, HEAD_B), lambda b, h: (b, h)),
        scratch_shapes=[pltpu.VMEM((S, HEAD_B), F32), pltpu.VMEM((S, HEAD_B), F32)],
        compiler_params=_cparams("parallel", "parallel"),
        name="hgrn2_mixer",
    )(z, z, z, z, z, lb.reshape(1, W), norm_g.reshape(1, HEAD_B),
      jnp.asarray(ddf, BF16), jnp.asarray(ddb, BF16), jnp.asarray(mkf), jnp.asarray(mkb))


def _head_sum(x, bd):
    x1, x2, x3 = _split3(x)
    return _dot(x1, bd) + _dot(x2, bd) + _dot(x3, bd)


def _rwkv_prep_kernel(x_ref, xp_ref, xn_ref, mu_ref, w0_ref, w2_ref, a0_ref, a2_ref, g2_ref, kk_ref,
                      ka_ref, rk_ref, bd_ref,
                      r_ref, v_ref, kap_ref, kdf_ref, kdb_ref, alf_ref, alb_ref, lwf_ref, lwb_ref,
                      g_ref, bon_ref, *, tm, seq, width):
    W = width
    i = pl.program_id(0)
    x = x_ref[...]
    at_start = (i * tm) % seq == 0
    at_end = ((i + 1) * tm) % seq == 0
    prev_row = jnp.where(at_start, 0.0, xp_ref[7:8, :])
    next_row = jnp.where(at_end, 0.0, xn_ref[0:1, :])
    row = lax.broadcasted_iota(jnp.int32, (tm, 1), 0)
    x_prev = jnp.where(row == 0, prev_row, pltpu.roll(x, 1, 0))
    x_next = jnp.where(row == tm - 1, next_row, pltpu.roll(x, tm - 1, 0))
    u = x + mu_ref[0:1, :] * (x_prev - x) + mu_ref[1:2, :] * (x_next - x)

    r, k, v = u[:, :W], u[:, W:2 * W], u[:, 2 * W:3 * W]
    o = 3 * W
    wd = jnp.tanh(u[:, o:o + 2 * LORA_DECAY]).astype(BF16)
    o += 2 * LORA_DECAY
    ad = u[:, o:o + 2 * LORA_AAA].astype(BF16)
    o += 2 * LORA_AAA
    gd = _sigmoid(u[:, o:]).astype(BF16)

    bd = bd_ref[...]
    kkr = k * kk_ref[...]
    sq = kkr * kkr
    rks = []
    a_dir = []
    for n in range(2):
        w_raw = w0_ref[n:n + 1, :] + _dot(wd[:, n * LORA_DECAY:(n + 1) * LORA_DECAY], w2_ref[n])
        lw = -DECAY_SCALE * _sigmoid(w_raw)
        a = _sigmoid(a0_ref[n:n + 1, :] + _dot(ad[:, n * LORA_AAA:(n + 1) * LORA_AAA], a2_ref[n]))
        a_dir.append(a)
        (lwf_ref, lwb_ref)[n][...] = lw
    kd = [k * (1.0 + (a - 1.0) * ka_ref[...]) for a in a_dir]
    kdf_ref[...] = kd[0]
    kdb_ref[...] = kd[1]
    rkk = r * (kd[0] + kd[1]) * rk_ref[...]
    for c in range(W // LANE):
        sl = slice(c * LANE, (c + 1) * LANE)
        nrm = jnp.maximum(jnp.sqrt(_head_sum(sq[:, sl], bd)), 1e-12)
        kap = kkr[:, sl] / nrm
        kap_ref[:, sl] = kap
        alf_ref[:, sl] = kap * a_dir[0][:, sl]
        alb_ref[:, sl] = kap * a_dir[1][:, sl]
        bon_ref[:, sl] = _head_sum(rkk[:, sl], bd) * v[:, sl]
    r_ref[...] = r
    v_ref[...] = v
    g_ref[...] = _dot(gd, g2_ref[...])


def rwkv_prep(z, mu, w0, w2, a0, a2, g2, k_k, k_a, r_k, batch, *, tm=256):
    T, ZW = z.shape
    W = w0.shape[-1]
    S = T // batch
    nb8 = tm // 8
    used = 3 * W + 2 * LORA_DECAY + 2 * LORA_AAA + LORA_GATE
    mu_p = jnp.pad(mu, ((0, 0), (0, ZW - used)))
    g2_p = jnp.pad(g2, ((0, ZW - used), (0, 0))).astype(BF16)
    bd = np.kron(np.eye(LANE // HEAD_A, dtype=np.float32), np.ones((HEAD_A, HEAD_A), np.float32))
    full = lambda a: pl.BlockSpec(a.shape, lambda i: (0,) * a.ndim)
    vec = lambda a: a.reshape(1, W)
    args = [z, z, z, mu_p, w0, w2.astype(BF16), a0, a2.astype(BF16), g2_p, vec(k_k), vec(k_a),
            vec(r_k), jnp.asarray(bd, BF16)]
    in_specs = [pl.BlockSpec((tm, ZW), lambda i: (i, 0)),
                pl.BlockSpec((8, ZW), lambda i: (jnp.maximum(i * nb8 - 1, 0), 0)),
                pl.BlockSpec((8, ZW), lambda i: (jnp.minimum((i + 1) * nb8, T // 8 - 1), 0))]
    in_specs += [full(a) for a in args[3:]]
    out = jax.ShapeDtypeStruct((T, W), F32)
    return pl.pallas_call(
        functools.partial(_rwkv_prep_kernel, tm=tm, seq=S, width=W),
        out_shape=(out,) * 11,
        grid=(T // tm,),
        in_specs=in_specs,
        out_specs=(pl.BlockSpec((tm, W), lambda i: (i, 0)),) * 11,
        compiler_params=_cparams("parallel"),
        name="rwkv_prep",
    )(*args)


def _rwkv_constants():
    C = CHUNK
    t = np.arange(C)[:, None]
    u = np.arange(C)[None, :]
    dd = np.stack([u <= t, u < t, u > t, np.ones((C, C), bool)]).astype(np.float32)
    dd_b = dd[:, ::-1, ::-1]
    to2d = lambda a: np.ascontiguousarray(a).reshape(-1, C)
    return to2d(dd), to2d(dd_b)


def _dot3(a, b):
    a1 = a.astype(BF16)
    a2 = (a - a1.astype(F32)).astype(BF16)
    b1 = b.astype(BF16)
    b2 = (b - b1.astype(F32)).astype(BF16)
    return _dot(a1, b1) + _dot(a1, b2) + _dot(a2, b1)


def _unit_tri_inverse(m, eye, diag_blocks):
    md = jnp.where(diag_blocks, m, 0.0)
    mo = m - md
    x = eye - md
    p = _dot3(md, md)
    x = x + _dot3(x, p)
    p = _dot3(p, p)
    x = x + _dot3(x, p)
    p = _dot3(p, p)
    td = x + _dot3(x, p)
    g = _dot3(td, mo)
    y = eye - g
    y = y + _dot3(y, _dot3(g, g))
    return _dot3(y, td)


def _rwkv_chunk(r, v, kap, kd, al, lw, st, dd, causal_incl, causal_strict, eye, diag_blocks, head_masks):
    C = CHUNK
    l1, l2, l3 = _split3(lw)
    e = _dot(dd, l1) + _dot(dd, l2) + _dot(dd, l3)
    b_in, b_ex, b_out, b_tot = e[:C], e[C:2 * C], e[2 * C:3 * C], e[3 * C:3 * C + 1]
    p_neg = jnp.exp(-b_in)
    kap_h = kap * jnp.exp(b_ex)
    r_h = r * jnp.exp(b_in)
    rhs_cat = jnp.concatenate([(al * p_neg).astype(BF16), (kd * p_neg).astype(BF16)], axis=0)
    p_out = jnp.exp(b_out)
    stb = st.astype(BF16)
    vb = v.astype(BF16)
    kap_st = _dot_nt(kap_h.astype(BF16), stb)
    y = _dot_nt(r_h.astype(BF16), stb)
    u = jnp.zeros_like(v)
    for hm in head_masks:
        lhs = jnp.concatenate([jnp.where(hm, kap_h, 0.0).astype(BF16),
                               jnp.where(hm, r_h, 0.0).astype(BF16)], axis=0)
        blk = _dot_nt(lhs, rhs_cat)
        m = jnp.where(causal_strict, blk[:C, :C], 0.0)
        n = jnp.where(causal_strict, blk[:C, C:], 0.0)
        ra = jnp.where(causal_incl, blk[C:, :C], 0.0)
        rk = jnp.where(causal_incl, blk[C:, C:], 0.0)
        tinv = _unit_tri_inverse(m, eye, diag_blocks)
        v_h = jnp.where(hm, v, 0.0).astype(BF16)
        rhs = jnp.where(hm, kap_st, 0.0) + _dot(n.astype(BF16), v_h)
        u_h = _dot3(tinv, rhs)
        y += _dot(rk.astype(BF16), v_h) - _dot(ra.astype(BF16), u_h.astype(BF16))
        u += u_h
    st_new = (st * jnp.exp(b_tot)
              + _dot_tn(vb, (kd * p_out).astype(BF16))
              - _dot_tn(u.astype(BF16), (al * p_out).astype(BF16)))
    return y, st_new


def _rwkv_scan_kernel(rf_ref, vf_ref, kapf_ref, kdf_ref, alf_ref, lwf_ref,
                      rb_ref, vb_ref, kapb_ref, kdb_ref, alb_ref, lwb_ref, ddf_ref, ddb_ref,
                      yf_ref, yb_ref, stf_ref, stb_ref, *, n_chunks):
    C = CHUNK

    @pl.when(pl.program_id(2) == 0)
    def _():
        stf_ref[...] = jnp.zeros_like(stf_ref)
        stb_ref[...] = jnp.zeros_like(stb_ref)

    ti = lax.broadcasted_iota(jnp.int32, (C, C), 0)
    si = lax.broadcasted_iota(jnp.int32, (C, C), 1)
    eye = (ti == si).astype(F32)
    diag_blocks = (ti // 16) == (si // 16)
    lane = lax.broadcasted_iota(jnp.int32, (1, LANE), 1)
    head_masks = [lane < HEAD_A, lane >= HEAD_A]
    vi = lax.broadcasted_iota(jnp.int32, (LANE, LANE), 0)
    ki = lax.broadcasted_iota(jnp.int32, (LANE, LANE), 1)
    same_head = (vi // HEAD_A) == (ki // HEAD_A)
    ddf, ddb = ddf_ref[...], ddb_ref[...]

    def load(ref, c):
        return ref[pl.ds(pl.multiple_of(c * C, C), C), :]

    def body(c, carry):
        st_f, st_b = carry
        y, st_f = _rwkv_chunk(load(rf_ref, c), load(vf_ref, c), load(kapf_ref, c), load(kdf_ref, c),
                              load(alf_ref, c), load(lwf_ref, c), st_f, ddf,
                              si <= ti, si < ti, eye, diag_blocks, head_masks)
        st_f = jnp.where(same_head, st_f, 0.0)
        yf_ref[pl.ds(pl.multiple_of(c * C, C), C), :] = y
        cb = n_chunks - 1 - c
        y, st_b = _rwkv_chunk(load(rb_ref, cb), load(vb_ref, cb), load(kapb_ref, cb), load(kdb_ref, cb),
                              load(alb_ref, cb), load(lwb_ref, cb), st_b, ddb,
                              si >= ti, si > ti, eye, diag_blocks, head_masks)
        st_b = jnp.where(same_head, st_b, 0.0)
        yb_ref[pl.ds(pl.multiple_of(cb * C, C), C), :] = y
        return st_f, st_b

    st_f, st_b = lax.fori_loop(0, n_chunks, body, (stf_ref[...], stb_ref[...]))
    stf_ref[...] = st_f
    stb_ref[...] = st_b


def rwkv_scan(r, v, kap, kd_f, kd_b, al_f, al_b, lw_f, lw_b, batch, *, ts=512):
    T, W = r.shape
    S = T // batch
    ts = min(ts, S)
    ns = S // ts
    ddf, ddb = _rwkv_constants()
    fwd = pl.BlockSpec((ts, LANE), lambda b, h, s: (b * ns + s, h))
    bwd = pl.BlockSpec((ts, LANE), lambda b, h, s: (b * ns + ns - 1 - s, h))
    const = lambda a: pl.BlockSpec(a.shape, lambda b, h, s: (0, 0))
    out = jax.ShapeDtypeStruct((T, W), F32)
    return pl.pallas_call(
        functools.partial(_rwkv_scan_kernel, n_chunks=ts // CHUNK),
        out_shape=(out, out),
        grid=(batch, W // LANE, ns),
        in_specs=[fwd] * 6 + [bwd] * 6 + [const(ddf), const(ddb)],
        out_specs=(fwd, bwd),
        scratch_shapes=[pltpu.VMEM((LANE, LANE), F32), pltpu.VMEM((LANE, LANE), F32)],
        compiler_params=_cparams("parallel", "parallel", "arbitrary"),
        name="rwkv_scan",
    )(r, v, kap, kd_f, al_f, lw_f, r, v, kap, kd_b, al_b, lw_b,
      jnp.asarray(ddf, BF16), jnp.asarray(ddb, BF16))


def _rwkv_out_kernel(yf_ref, yb_ref, bon_ref, g_ref, gw_ref, gb_ref, bd_ref, o_ref, *, width):
    bd = bd_ref[...]
    inv_n = 1.0 / HEAD_A
    for c in range(width // LANE):
        sl = slice(c * LANE, (c + 1) * LANE)
        y = yf_ref[:, sl] + yb_ref[:, sl]
        mean = _head_sum(y, bd) * inv_n
        d = y - mean
        var = _head_sum(d * d, bd) * inv_n
        yn = d * lax.rsqrt(var + GN_EPS) * gw_ref[:, sl] + gb_ref[:, sl] + bon_ref[:, sl]
        o_ref[:, sl] = (yn * g_ref[:, sl]).astype(o_ref.dtype)


def rwkv_out(y_f, y_b, bonus, g, gn_w, gn_b, *, tm=512):
    T, W = y_f.shape
    bd = np.kron(np.eye(LANE // HEAD_A, dtype=np.float32), np.ones((HEAD_A, HEAD_A), np.float32))
    row = pl.BlockSpec((tm, W), lambda i: (i, 0))
    vec = pl.BlockSpec((1, W), lambda i: (0, 0))
    return pl.pallas_call(
        functools.partial(_rwkv_out_kernel, width=W),
        out_shape=jax.ShapeDtypeStruct((T, W), BF16),
        grid=(T // tm,),
        in_specs=[row, row, row, row, vec, vec, pl.BlockSpec((LANE, LANE), lambda i: (0, 0))],
        out_specs=row,
        compiler_params=_cparams("parallel"),
        name="rwkv_out",
    )(y_f, y_b, bonus, g, gn_w.reshape(1, W), gn_b.reshape(1, W), jnp.asarray(bd, BF16))


def rwkv7_mixer(z, mu, w0, w2, a0, a2, g2, k_k, k_a, r_k, gn_w, gn_b, batch):
    r, v, kap, kd_f, kd_b, al_f, al_b, lw_f, lw_b, g, bonus = rwkv_prep(
        z, mu, w0, w2, a0, a2, g2, k_k, k_a, r_k.reshape(-1), batch)
    y_f, y_b = rwkv_scan(r, v, kap, kd_f, kd_b, al_f, al_b, lw_f, lw_b, batch)
    return rwkv_out(y_f, y_b, bonus, g, gn_w, gn_b)


def _relu2(acc):
    r = jnp.maximum(acc, 0.0)
    return r * r


def _add(acc, res):
    return acc + res


def _pad_cols(w, mult):
    return jnp.pad(w, ((0, 0), (0, (-w.shape[1]) % mult)))


def kernel(x, p, positions, ln1_g, w_in, rwkv_mu, rwkv_w0, rwkv_w2, rwkv_a0, rwkv_a2, rwkv_g2, rwkv_kk, rwkv_ka, rwkv_rk, rwkv_gn_w, rwkv_gn_b, hgrn_lb, hgrn_norm_g, mla_q_norm_g, mla_kv_norm_g, mla_w_uq, mla_w_ukv, w_branch, w_o, ln2_g, w_mlp1, w_mlp2, w_pe, w_pg, final_g):
    Bn, S, D = x.shape
    L = w_in.shape[0]
    T = Bn * S
    W = rwkv_w0.shape[-1]
    q_lora, kv_lora = mla_q_norm_g.shape[-1], mla_kv_norm_g.shape[-1]
    n_heads_c = mla_w_ukv.shape[-1] // (QK_NOPE + V_HEAD)
    rwkv_w = 3 * W + 2 * LORA_DECAY + 2 * LORA_AAA + LORA_GATE
    hgrn_w = 5 * W
    o_hgrn = rwkv_w
    o_cq = o_hgrn + hgrn_w
    o_ckv = o_cq + q_lora
    o_kr = o_ckv + kv_lora
    o_gate = o_kr + QK_ROPE
    half = QK_ROPE // 2

    lb_w = jax.nn.softmax(hgrn_lb.astype(F32), axis=0)
    lower_bounds = jnp.cumsum(lb_w, axis=0) - lb_w[0]
    cq_tab, sq_tab, ck_tab, sk_tab = _rope_tables(positions)

    h = x.reshape(T, D)
    for l in range(L):
        wl = w_in[l]
        w_rwkv = _pad_cols(wl[:, :rwkv_w], 512).astype(BF16)
        w_hgrn = wl[:, o_hgrn:o_cq].astype(BF16)
        w_cq = wl[:, o_cq:o_ckv].astype(BF16)
        w_kr = wl[:, o_kr:o_gate]
        w_kv = jnp.concatenate(
            [wl[:, o_ckv:o_kr], _pad_cols(w_kr, LANE),
             _pad_cols(jnp.concatenate([w_kr[:, half:], w_kr[:, :half]], axis=1), LANE)], axis=1).astype(BF16)
        w_gate = wl[:, o_gate:].astype(BF16)

        hn = rmsnorm(h, ln1_g[l], BF16)
        z_rwkv = matmul(hn, w_rwkv, out_dtype=F32, tm=1024, tn=512, name="in_rwkv")
        z_hgrn = matmul(hn, w_hgrn, out_dtype=F32, tm=1024, tn=512, name="in_hgrn")
        z_cq = matmul(hn, w_cq, out_dtype=F32, tm=1024, tn=w_cq.shape[1], name="in_cq")
        z_kv = matmul(hn, w_kv, out_dtype=F32, tm=1024, tn=w_kv.shape[1], name="in_kv")
        z_gate = matmul(hn, w_gate, out_dtype=F32, tm=1024, tn=512, name="in_gate")

        y_a = rwkv7_mixer(z_rwkv, rwkv_mu[l], rwkv_w0[l], rwkv_w2[l], rwkv_a0[l], rwkv_a2[l], rwkv_g2[l],
                          rwkv_kk[l], rwkv_ka[l], rwkv_rk[l], rwkv_gn_w[l], rwkv_gn_b[l], Bn)
        y_b = hgrn2_mixer(z_hgrn, lower_bounds[l], hgrn_norm_g[l], Bn)
        wqa, wqb = _mla_q_weights(mla_w_uq[l], n_heads_c)
        q = mla_q_proj(z_cq, mla_q_norm_g[l], wqa, wqb, cq_tab, sq_tab, n_heads_c)
        k, v = mla_kv_proj(z_kv, mla_kv_norm_g[l], mla_w_ukv[l].astype(BF16), ck_tab, sk_tab, n_heads_c)
        y_c = mla_attention(q, k, v, Bn, n_heads_c)

        mixed = branch_mix(y_a, y_b, y_c, w_branch[l].astype(BF16), z_gate)
        h = matmul(mixed, w_o[l].astype(BF16), out_dtype=F32, tm=1024, tn=1024, epilogue=_add,
                   extras=(h,), alias_extra=0, name="w_o")
        hn = rmsnorm(h, ln2_g[l], BF16)
        hid = matmul(hn, w_mlp1[l].astype(BF16), out_dtype=BF16, tm=1024, tn=1024, epilogue=_relu2,
                     name="mlp1")
        h = matmul(hid, w_mlp2[l].astype(BF16), out_dtype=F32, tm=1024, tn=1024, tk=2048, epilogue=_add,
                   extras=(h,), alias_extra=0, name="mlp2")
        h = ple_update(h, w_pg[l].astype(BF16), p[l].reshape(T, -1), w_pe[l].astype(BF16))
    return rmsnorm(h, final_g, F32).reshape(Bn, S, D)
```

```python
import functools

import numpy as np
import jax
import jax.numpy as jnp
from jax import lax
from jax.experimental import pallas as pl
from jax.experimental.pallas import tpu as pltpu

F32 = jnp.float32
BF16 = jnp.bfloat16

LANE = 128
VMEM_LIMIT = 48 * 2**20

HEAD_A = 64
LORA_DECAY = 64
LORA_AAA = 64
LORA_GATE = 160
DECAY_SCALE = 0.606531
GN_EPS = 64e-5
HEAD_B = 128
F_TINY = 1e-30
QK_NOPE = 128
QK_ROPE = 64
V_HEAD = 128
ROPE_THETA = 10000.0
NORM_EPS = 1e-6
CHUNK = 64


def _cparams(*sem):
    return pltpu.CompilerParams(dimension_semantics=sem, vmem_limit_bytes=VMEM_LIMIT)


def _sigmoid(x):
    return 1.0 / (1.0 + jnp.exp(-x))


def _dot(a, b):
    return jnp.dot(a, b, preferred_element_type=F32)


def _dot_nt(a, b):
    return lax.dot_general(a, b, (((1,), (1,)), ((), ())), preferred_element_type=F32)


def _dot_tn(a, b):
    return lax.dot_general(a, b, (((0,), (0,)), ((), ())), preferred_element_type=F32)


def _split3(x):
    x1 = x.astype(BF16)
    r1 = x - x1.astype(F32)
    x2 = r1.astype(BF16)
    x3 = (r1 - x2.astype(F32)).astype(BF16)
    return x1, x2, x3


def _rmsnorm_kernel(x_ref, g_ref, o_ref):
    x = x_ref[...]
    ms = jnp.mean(x * x, axis=-1, keepdims=True)
    o_ref[...] = (x * lax.rsqrt(ms + NORM_EPS) * g_ref[...]).astype(o_ref.dtype)


def rmsnorm(x, g, out_dtype, tm=512):
    T, D = x.shape
    return pl.pallas_call(
        _rmsnorm_kernel,
        out_shape=jax.ShapeDtypeStruct((T, D), out_dtype),
        grid=(T // tm,),
        in_specs=[pl.BlockSpec((tm, D), lambda i: (i, 0)),
                  pl.BlockSpec((1, D), lambda i: (0, 0))],
        out_specs=pl.BlockSpec((tm, D), lambda i: (i, 0)),
        compiler_params=_cparams("parallel"),
        name="rmsnorm",
    )(x, g.reshape(1, D))


def _mm_kernel(a_ref, w_ref, *rest, nk, epilogue, n_extra):
    extras = rest[:n_extra]
    o_ref = rest[n_extra]

    def finish(acc):
        o_ref[...] = epilogue(acc, *[e[...] for e in extras]).astype(o_ref.dtype)

    if nk == 1:
        finish(_dot(a_ref[...], w_ref[...]))
    else:
        acc_ref = rest[n_extra + 1]
        k = pl.program_id(2)

        @pl.when(k == 0)
        def _():
            acc_ref[...] = jnp.zeros_like(acc_ref)

        acc_ref[...] += _dot(a_ref[...], w_ref[...])

        @pl.when(k == nk - 1)
        def _():
            finish(acc_ref[...])


def matmul(a, w, *, out_dtype, tm, tn, tk=None, epilogue=None, extras=(), alias_extra=None,
           name="matmul"):
    M, K = a.shape
    N = w.shape[1]
    tk = K if tk is None else tk
    nk = K // tk
    epilogue = epilogue or (lambda acc: acc)
    kern = functools.partial(_mm_kernel, nk=nk, epilogue=epilogue, n_extra=len(extras))
    in_specs = [pl.BlockSpec((tm, tk), lambda i, j, k: (i, k)),
                pl.BlockSpec((tk, tn), lambda i, j, k: (k, j))]
    in_specs += [pl.BlockSpec((tm, tn), lambda i, j, k: (i, j)) for _ in extras]
    aliases = {} if alias_extra is None else {2 + alias_extra: 0}
    return pl.pallas_call(
        kern,
        out_shape=jax.ShapeDtypeStruct((M, N), out_dtype),
        grid=(M // tm, N // tn, nk),
        in_specs=in_specs,
        out_specs=pl.BlockSpec((tm, tn), lambda i, j, k: (i, j)),
        scratch_shapes=[pltpu.VMEM((tm, tn), F32)] if nk > 1 else [],
        input_output_aliases=aliases,
        compiler_params=_cparams("parallel", "parallel", "arbitrary"),
        name=name,
    )(a, w, *extras)


def _branch_kernel(ya_ref, yb_ref, yc_ref, p_ref, ga_ref, gb_ref, gc_ref, o_ref):
    acc = _sigmoid(ga_ref[...]) * _dot(ya_ref[...], p_ref[0])
    acc += _sigmoid(gb_ref[...]) * _dot(yb_ref[...], p_ref[1])
    acc += _sigmoid(gc_ref[...]) * _dot(yc_ref[...], p_ref[2])
    o_ref[...] = acc.astype(o_ref.dtype)


def branch_mix(ya, yb, yc, p, zg, *, tm=1024, tn=512):
    T, W = ya.shape
    D = p.shape[2]
    nj = D // tn
    y_spec = pl.BlockSpec((tm, W), lambda i, j: (i, 0))
    g_specs = [pl.BlockSpec((tm, tn), functools.partial(lambda i, j, n: (i, n * nj + j), n=n))
               for n in range(3)]
    return pl.pallas_call(
        _branch_kernel,
        out_shape=jax.ShapeDtypeStruct((T, D), BF16),
        grid=(T // tm, nj),
        in_specs=[y_spec, y_spec, y_spec,
                  pl.BlockSpec((3, W, tn), lambda i, j: (0, 0, j))] + g_specs,
        out_specs=pl.BlockSpec((tm, tn), lambda i, j: (i, j)),
        compiler_params=_cparams("parallel", "parallel"),
        name="branch_mix",
    )(ya, yb, yc, p, zg, zg, zg)


def _ple_kernel(h_ref, wpg_ref, p_ref, wpe_ref, hres_ref, o_ref):
    gate = _sigmoid(_dot(h_ref[...].astype(BF16), wpg_ref[...]))
    emb = _dot(p_ref[...].astype(BF16), wpe_ref[...])
    o_ref[...] = hres_ref[...] + gate * emb


def ple_update(h, wpg, p, wpe, *, tm=512, tn=512):
    T, D = h.shape
    E = p.shape[1]
    return pl.pallas_call(
        _ple_kernel,
        out_shape=jax.ShapeDtypeStruct((T, D), F32),
        grid=(T // tm, D // tn),
        in_specs=[pl.BlockSpec((tm, D), lambda i, j: (i, 0)),
                  pl.BlockSpec((D, tn), lambda i, j: (0, j)),
                  pl.BlockSpec((tm, E), lambda i, j: (i, 0)),
                  pl.BlockSpec((E, tn), lambda i, j: (0, j)),
                  pl.BlockSpec((tm, tn), lambda i, j: (i, j))],
        out_specs=pl.BlockSpec((tm, tn), lambda i, j: (i, j)),
        compiler_params=_cparams("parallel", "parallel"),
        name="ple_update",
    )(h, wpg, p, wpe, h)


def _mla_q_kernel(cq_ref, g_ref, wa_ref, wb_ref, c_ref, s_ref, o_ref, xn_ref):
    @pl.when(pl.program_id(1) == 0)
    def _():
        x = cq_ref[...]
        ms = jnp.mean(x * x, axis=-1, keepdims=True)
        xn_ref[...] = (x * lax.rsqrt(ms + NORM_EPS) * g_ref[...]).astype(BF16)

    xn = xn_ref[...]
    o_ref[...] = (_dot(xn, wa_ref[...]) * c_ref[...] + _dot(xn, wb_ref[...]) * s_ref[...]).astype(BF16)


def mla_q_proj(cq, g, wa, wb, ctab, stab, n_heads, *, tm=512):
    T, R = cq.shape
    HW = wa.shape[1] // n_heads
    return pl.pallas_call(
        _mla_q_kernel,
        out_shape=jax.ShapeDtypeStruct((T, n_heads * HW), BF16),
        grid=(T // tm, n_heads),
        in_specs=[pl.BlockSpec((tm, R), lambda i, h: (i, 0)),
                  pl.BlockSpec((1, R), lambda i, h: (0, 0)),
                  pl.BlockSpec((R, HW), lambda i, h: (0, h)),
                  pl.BlockSpec((R, HW), lambda i, h: (0, h)),
                  pl.BlockSpec((tm, HW), lambda i, h: (i, 0)),
                  pl.BlockSpec((tm, HW), lambda i, h: (i, 0))],
        out_specs=pl.BlockSpec((tm, HW), lambda i, h: (i, h)),
        scratch_shapes=[pltpu.VMEM((tm, R), BF16)],
        compiler_params=_cparams("parallel", "arbitrary"),
        name="mla_q_proj",
    )(cq, g.reshape(1, R), wa, wb, ctab, stab)


def _mla_kv_kernel(z_ref, g_ref, w_ref, c_ref, s_ref, k_ref, v_ref, xn_ref, *, kv_lora):
    @pl.when(pl.program_id(1) == 0)
    def _():
        x = z_ref[:, :kv_lora]
        ms = jnp.mean(x * x, axis=-1, keepdims=True)
        xn_ref[...] = (x * lax.rsqrt(ms + NORM_EPS) * g_ref[...]).astype(BF16)

    kv = _dot(xn_ref[...], w_ref[...])
    kr = z_ref[:, kv_lora:kv_lora + LANE] * c_ref[...] + z_ref[:, kv_lora + LANE:] * s_ref[...]
    k_ref[:, :QK_NOPE] = kv[:, :QK_NOPE].astype(BF16)
    k_ref[:, QK_NOPE:] = kr.astype(BF16)
    v_ref[...] = kv[:, QK_NOPE:].astype(BF16)


def mla_kv_proj(zkv, g, w, ctab, stab, n_heads, *, tm=512):
    T, ZW = zkv.shape
    R = ZW - 2 * LANE
    return pl.pallas_call(
        functools.partial(_mla_kv_kernel, kv_lora=R),
        out_shape=(jax.ShapeDtypeStruct((T, n_heads * 2 * LANE), BF16),
                   jax.ShapeDtypeStruct((T, n_heads * V_HEAD), BF16)),
        grid=(T // tm, n_heads),
        in_specs=[pl.BlockSpec((tm, ZW), lambda i, h: (i, 0)),
                  pl.BlockSpec((1, R), lambda i, h: (0, 0)),
                  pl.BlockSpec((R, QK_NOPE + V_HEAD), lambda i, h: (0, h)),
                  pl.BlockSpec((tm, LANE), lambda i, h: (i, 0)),
                  pl.BlockSpec((tm, LANE), lambda i, h: (i, 0))],
        out_specs=(pl.BlockSpec((tm, 2 * LANE), lambda i, h: (i, h)),
                   pl.BlockSpec((tm, V_HEAD), lambda i, h: (i, h))),
        scratch_shapes=[pltpu.VMEM((tm, R), BF16)],
        compiler_params=_cparams("parallel", "arbitrary"),
        name="mla_kv_proj",
    )(zkv, g.reshape(1, R), w, ctab, stab)


def _attn_kernel(q_ref, k_ref, v_ref, o_ref):
    s = _dot_nt(q_ref[...], k_ref[...])
    m = jnp.max(s, axis=-1, keepdims=True)
    p = jnp.exp(s - m)
    l = jnp.sum(p, axis=-1, keepdims=True)
    o = _dot(p.astype(BF16), v_ref[...])
    o_ref[...] = (o / l).astype(o_ref.dtype)


def mla_attention(q, k, v, batch, n_heads, *, tq=256):
    T = q.shape[0]
    S = T // batch
    nq = S // tq
    QW = q.shape[1] // n_heads
    return pl.pallas_call(
        _attn_kernel,
        out_shape=jax.ShapeDtypeStruct((T, n_heads * V_HEAD), BF16),
        grid=(batch, n_heads, nq),
        in_specs=[pl.BlockSpec((tq, QW), lambda b, h, i: (b * nq + i, h)),
                  pl.BlockSpec((S, QW), lambda b, h, i: (b, h)),
                  pl.BlockSpec((S, V_HEAD), lambda b, h, i: (b, h))],
        out_specs=pl.BlockSpec((tq, V_HEAD), lambda b, h, i: (b * nq + i, h)),
        compiler_params=_cparams("parallel", "parallel", "arbitrary"),
        name="mla_attention",
    )(q, k, v)


def _rope_tables(positions):
    inv_freq = 1.0 / (ROPE_THETA ** (jnp.arange(0, QK_ROPE, 2, dtype=F32) / QK_ROPE))
    ang = positions.astype(F32).reshape(-1, 1) * inv_freq
    cos, sin = jnp.cos(ang), jnp.sin(ang)
    T = ang.shape[0]
    scale = (QK_NOPE + QK_ROPE) ** -0.5
    z64 = jnp.zeros((T, LANE - QK_ROPE), F32)
    ck = jnp.concatenate([cos, cos, z64], axis=1)
    sk = jnp.concatenate([-sin, sin, z64], axis=1)
    cq = jnp.concatenate([jnp.ones((T, QK_NOPE), F32), ck], axis=1) * scale
    sq = jnp.concatenate([jnp.zeros((T, QK_NOPE), F32), sk], axis=1) * scale
    return cq, sq, ck, sk


def _mla_q_weights(w_uq, n_heads):
    R = w_uq.shape[0]
    w = w_uq.reshape(R, n_heads, QK_NOPE + QK_ROPE)
    half = QK_ROPE // 2
    zpad = jnp.zeros((R, n_heads, LANE - QK_ROPE), w.dtype)
    wa = jnp.concatenate([w, zpad], axis=2)
    wb = jnp.concatenate([jnp.zeros((R, n_heads, QK_NOPE), w.dtype),
                          w[:, :, QK_NOPE + half:], w[:, :, QK_NOPE:QK_NOPE + half], zpad], axis=2)
    return (wa.reshape(R, n_heads * 2 * LANE).astype(BF16),
            wb.reshape(R, n_heads * 2 * LANE).astype(BF16))


N_LEVELS = 6


def _gla_constants():
    C = CHUNK
    t = np.arange(C)[:, None]
    u = np.arange(C)[None, :]
    dd = np.zeros((N_LEVELS + 3, C, C), np.float32)
    mk = np.zeros((N_LEVELS + 1, C, C), np.float32)
    for l in range(N_LEVELS):
        m = C >> (l + 1)
        mid = (t // (2 * m)) * (2 * m) + m - 1
        second = (t % (2 * m)) >= m
        dd[l] = np.where(second, (u > mid) & (u <= t), (u > t) & (u <= mid))
        mk[l] = (t // (2 * m) == u // (2 * m)) & second & ((u % (2 * m)) < m)
    dd[N_LEVELS] = u <= t
    dd[N_LEVELS + 1] = u > t
    dd[N_LEVELS + 2] = 1.0
    mk[N_LEVELS] = t == u
    dd_b = dd[:, ::-1, ::-1]
    mk_b = mk[:, ::-1, ::-1]
    to2d = lambda a: np.ascontiguousarray(a).reshape(-1, C)
    return to2d(dd), to2d(dd_b), to2d(mk), to2d(mk_b)


def _gla_chunk(q, k, v, g, st, dd, mk):
    C = CHUNK
    g1, g2, g3 = _split3(g)
    e = _dot(dd, g1) + _dot(dd, g2) + _dot(dd, g3)
    p = jnp.exp(jnp.minimum(e, 0.0))
    kb = k.astype(BF16)
    scores = jnp.where(mk[N_LEVELS * C:(N_LEVELS + 1) * C] > 0, _dot_nt(q.astype(BF16), kb), 0.0)
    for l in range(N_LEVELS):
        pl_ = p[l * C:(l + 1) * C]
        a = _dot_nt((q * pl_).astype(BF16), (k * pl_).astype(BF16))
        scores += jnp.where(mk[l * C:(l + 1) * C] > 0, a, 0.0)
    p_in = p[N_LEVELS * C:(N_LEVELS + 1) * C]
    p_out = p[(N_LEVELS + 1) * C:(N_LEVELS + 2) * C]
    p_tot = p[(N_LEVELS + 2) * C:(N_LEVELS + 2) * C + 1]
    vb = v.astype(BF16)
    o = _dot_nt((q * p_in).astype(BF16), st.astype(BF16)) + _dot(scores.astype(BF16), vb)
    st_new = st * p_tot + _dot_tn(vb, (k * p_out).astype(BF16))
    return o, st_new


def _hgrn_kernel(q_ref, zf_ref, zb_ref, i_ref, g_ref, lb_ref, ng_ref, ddf_ref, ddb_ref, mkf_ref, mkb_ref,
                 o_ref, of_ref, ob_ref, *, n_chunks):
    C = CHUNK
    lb = lb_ref[...]
    one_m_lb = 1.0 - lb
    ddf, ddb, mkf, mkb = ddf_ref[...], ddb_ref[...], mkf_ref[...], mkb_ref[...]

    def gate(zz):
        f = lb + one_m_lb * _sigmoid(zz)
        return one_m_lb * _sigmoid(-zz), jnp.log(jnp.maximum(f, F_TINY))

    def load(ref, c):
        return ref[pl.ds(pl.multiple_of(c * C, C), C), :]

    def body(c, carry):
        st_f, st_b = carry
        cb = n_chunks - 1 - c
        qf = load(q_ref, c)
        qf = qf * _sigmoid(qf)
        kf, gf = gate(load(zf_ref, c))
        o, st_f = _gla_chunk(qf, kf, load(i_ref, c), gf, st_f, ddf, mkf)
        of_ref[pl.ds(pl.multiple_of(c * C, C), C), :] = o
        qb = load(q_ref, cb)
        qb = qb * _sigmoid(qb)
        kb, gb = gate(load(zb_ref, cb))
        o, st_b = _gla_chunk(qb, kb, load(i_ref, cb), gb, st_b, ddb, mkb)
        ob_ref[pl.ds(pl.multiple_of(cb * C, C), C), :] = o
        return st_f, st_b

    dv, dk = i_ref.shape[1], q_ref.shape[1]
    z = jnp.zeros((dv, dk), F32)
    lax.fori_loop(0, n_chunks, body, (z, z), unroll=2)

    o = of_ref[...] + ob_ref[...]
    ms = jnp.mean(o * o, axis=-1, keepdims=True)
    gg = g_ref[...]
    o_ref[...] = (o * lax.rsqrt(ms + NORM_EPS) * ng_ref[...] * (gg * _sigmoid(gg))).astype(o_ref.dtype)


def hgrn2_mixer(z, lb, norm_g, batch, *, col0=0):
    T = z.shape[0]
    W = lb.shape[-1]
    S = T // batch
    H = W // HEAD_B
    c0 = col0 // HEAD_B
    ddf, ddb, mkf, mkb = _gla_constants()
    part = lambda n: pl.BlockSpec((S, HEAD_B), functools.partial(lambda b, h, n: (b, c0 + n * H + h), n=n))
    const = lambda a: pl.BlockSpec(a.shape, lambda b, h: (0, 0))
    return pl.pallas_call(
        functools.partial(_hgrn_kernel, n_chunks=S // CHUNK),
        out_shape=jax.ShapeDtypeStruct((T, W), BF16),
        grid=(batch, H),
        in_specs=[part(0), part(1), part(2), part(3), part(4),
                  pl.BlockSpec((1, HEAD_B), lambda b, h: (0, h)),
                  pl.BlockSpec((1, HEAD_B), lambda b, h: (0, 0)),
                  const(ddf), const(ddb), const(mkf), const(mkb)],
        out_specs=pl.BlockSpec((S, HEAD_B), lambda b, h: (b, h)),
        scratch_shapes=[pltpu.VMEM((S, HEAD_B), F32), pltpu.VMEM((S, HEAD_B), F32)],
        compiler_params=_cparams("parallel", "parallel"),
        name="hgrn2_mixer",
    )(z, z, z, z, z, lb.reshape(1, W), norm_g.reshape(1, HEAD_B),
      jnp.asarray(ddf, BF16), jnp.asarray(ddb, BF16), jnp.asarray(mkf), jnp.asarray(mkb))


def _head_sum(x, bd):
    x1, x2, x3 = _split3(x)
    return _dot(x1, bd) + _dot(x2, bd) + _dot(x3, bd)


def _rwkv_prep_kernel(x_ref, xp_ref, xn_ref, mu_ref, w0_ref, w2_ref, a0_ref, a2_ref, g2_ref, kk_ref,
                      ka_ref, rk_ref, bd_ref,
                      r_ref, v_ref, kap_ref, kdf_ref, kdb_ref, alf_ref, alb_ref, lwf_ref, lwb_ref,
                      g_ref, bon_ref, *, tm, seq, width):
    W = width
    i = pl.program_id(0)
    x = x_ref[...]
    at_start = (i * tm) % seq == 0
    at_end = ((i + 1) * tm) % seq == 0
    prev_row = jnp.where(at_start, 0.0, xp_ref[7:8, :])
    next_row = jnp.where(at_end, 0.0, xn_ref[0:1, :])
    row = lax.broadcasted_iota(jnp.int32, (tm, 1), 0)
    x_prev = jnp.where(row == 0, prev_row, pltpu.roll(x, 1, 0))
    x_next = jnp.where(row == tm - 1, next_row, pltpu.roll(x, tm - 1, 0))
    u = x + mu_ref[0:1, :] * (x_prev - x) + mu_ref[1:2, :] * (x_next - x)

    r, k, v = u[:, :W], u[:, W:2 * W], u[:, 2 * W:3 * W]
    o = 3 * W
    wd = jnp.tanh(u[:, o:o + 2 * LORA_DECAY]).astype(BF16)
    o += 2 * LORA_DECAY
    ad = u[:, o:o + 2 * LORA_AAA].astype(BF16)
    o += 2 * LORA_AAA
    gd = _sigmoid(u[:, o:]).astype(BF16)

    bd = bd_ref[...]
    kkr = k * kk_ref[...]
    sq = kkr * kkr
    rks = []
    a_dir = []
    for n in range(2):
        w_raw = w0_ref[n:n + 1, :] + _dot(wd[:, n * LORA_DECAY:(n + 1) * LORA_DECAY], w2_ref[n])
        lw = -DECAY_SCALE * _sigmoid(w_raw)
        a = _sigmoid(a0_ref[n:n + 1, :] + _dot(ad[:, n * LORA_AAA:(n + 1) * LORA_AAA], a2_ref[n]))
        a_dir.append(a)
        (lwf_ref, lwb_ref)[n][...] = lw
    kd = [k * (1.0 + (a - 1.0) * ka_ref[...]) for a in a_dir]
    kdf_ref[...] = kd[0]
    kdb_ref[...] = kd[1]
    rkk = r * (kd[0] + kd[1]) * rk_ref[...]
    for c in range(W // LANE):
        sl = slice(c * LANE, (c + 1) * LANE)
        nrm = jnp.maximum(jnp.sqrt(_head_sum(sq[:, sl], bd)), 1e-12)
        kap = kkr[:, sl] / nrm
        kap_ref[:, sl] = kap
        alf_ref[:, sl] = kap * a_dir[0][:, sl]
        alb_ref[:, sl] = kap * a_dir[1][:, sl]
        bon_ref[:, sl] = _head_sum(rkk[:, sl], bd) * v[:, sl]
    r_ref[...] = r
    v_ref[...] = v
    g_ref[...] = _dot(gd, g2_ref[...])


def rwkv_prep(z, mu, w0, w2, a0, a2, g2, k_k, k_a, r_k, batch, *, tm=256):
    T, ZW = z.shape
    W = w0.shape[-1]
    S = T // batch
    nb8 = tm // 8
    used = 3 * W + 2 * LORA_DECAY + 2 * LORA_AAA + LORA_GATE
    mu_p = jnp.pad(mu, ((0, 0), (0, ZW - used)))
    g2_p = jnp.pad(g2, ((0, ZW - used), (0, 0))).astype(BF16)
    bd = np.kron(np.eye(LANE // HEAD_A, dtype=np.float32), np.ones((HEAD_A, HEAD_A), np.float32))
    full = lambda a: pl.BlockSpec(a.shape, lambda i: (0,) * a.ndim)
    vec = lambda a: a.reshape(1, W)
    args = [z, z, z, mu_p, w0, w2.astype(BF16), a0, a2.astype(BF16), g2_p, vec(k_k), vec(k_a),
            vec(r_k), jnp.asarray(bd, BF16)]
    in_specs = [pl.BlockSpec((tm, ZW), lambda i: (i, 0)),
                pl.BlockSpec((8, ZW), lambda i: (jnp.maximum(i * nb8 - 1, 0), 0)),
                pl.BlockSpec((8, ZW), lambda i: (jnp.minimum((i + 1) * nb8, T // 8 - 1), 0))]
    in_specs += [full(a) for a in args[3:]]
    out = jax.ShapeDtypeStruct((T, W), F32)
    return pl.pallas_call(
        functools.partial(_rwkv_prep_kernel, tm=tm, seq=S, width=W),
        out_shape=(out,) * 11,
        grid=(T // tm,),
        in_specs=in_specs,
        out_specs=(pl.BlockSpec((tm, W), lambda i: (i, 0)),) * 11,
        compiler_params=_cparams("parallel"),
        name="rwkv_prep",
    )(*args)


RWKV_GROUP = 256


def _rwkv_constants():
    G, C = RWKV_GROUP, CHUNK
    t = np.arange(G)[:, None]
    u = np.arange(G)[None, :]
    same = (t // C) == (u // C)
    dd = np.stack([same & (u <= t), same]).astype(np.float32)
    dd_b = dd[:, ::-1, ::-1]
    to2d = lambda a: np.ascontiguousarray(a).reshape(-1, G)
    fold = np.zeros((2, G, LANE), np.float32)
    for hd in range(2):
        fold[hd, np.arange(G), hd * C + np.arange(G) % C] = 1.0
    return to2d(dd), to2d(dd_b), fold[0], fold[1]


def _bdot(a, b):
    return _dot(a.astype(BF16), b.astype(BF16))


def _unit_tri_inverse(m, eye, diag_blocks):
    md = jnp.where(diag_blocks, m, 0.0)
    mo = m - md
    x = eye - md
    p = _bdot(md, md)
    x = x + _bdot(x, p)
    p = _bdot(p, p)
    x = x + _bdot(x, p)
    p = _bdot(p, p)
    td = x + _bdot(x, p)
    g = _bdot(td, mo)
    y = eye - g
    y = y + _bdot(y, _bdot(g, g))
    return _bdot(y, td)


def _rwkv_group(r, v, kap, kd, al, lw, dd, causal_incl, causal_strict, eye, diag_blocks, head_masks, folds):
    G = r.shape[0]
    l1, l2, l3 = _split3(lw)
    e = _dot(dd, l1) + _dot(dd, l2) + _dot(dd, l3)
    b_in, b_tot = e[:G], e[G:]
    b_ex, b_out = b_in - lw, b_tot - b_in
    p_neg = jnp.exp(-b_in)
    kap_h = kap * jnp.exp(b_ex)
    r_h = r * jnp.exp(b_in)
    rhs_cat = jnp.concatenate([(al * p_neg).astype(BF16), (kd * p_neg).astype(BF16)], axis=0)
    p_out = jnp.exp(b_out)
    kw = jnp.zeros_like(r)
    uv = jnp.zeros_like(r)
    y0 = jnp.zeros_like(r)
    ra_cat = jnp.zeros_like(r)
    for hm, fold in zip(head_masks, folds):
        kap_hb = jnp.where(hm, kap_h, 0.0).astype(BF16)
        lhs = jnp.concatenate([kap_hb, jnp.where(hm, r_h, 0.0).astype(BF16)], axis=0)
        blk = _dot_nt(lhs, rhs_cat)
        m = jnp.where(causal_strict, blk[:G, :G], 0.0)
        n = jnp.where(causal_strict, blk[:G, G:], 0.0)
        ra = jnp.where(causal_incl, blk[G:, :G], 0.0)
        rk = jnp.where(causal_incl, blk[G:, G:], 0.0)
        tinv = _unit_tri_inverse(m, eye, diag_blocks).astype(BF16)
        v_h = jnp.where(hm, v, 0.0).astype(BF16)
        kw += _dot(tinv, kap_hb)
        uv += _dot(tinv, _bdot(n, v_h).astype(BF16))
        y0 += _bdot(rk, v_h)
        ra_cat += _bdot(ra, fold)
    return kw, r_h, ra_cat, kd * p_out, al * p_out, uv, y0, jnp.exp(b_tot)


N_LOCAL = 8


def _rwkv_local_kernel(r_ref, v_ref, kap_ref, kdf_ref, kdb_ref, alf_ref, alb_ref, lwf_ref, lwb_ref,
                       ddf_ref, ddb_ref, f0_ref, f1_ref, *out_refs):
    G, C = RWKV_GROUP, CHUNK
    ti = lax.broadcasted_iota(jnp.int32, (G, G), 0)
    si = lax.broadcasted_iota(jnp.int32, (G, G), 1)
    same_chunk = (ti // C) == (si // C)
    eye = (ti == si).astype(F32)
    diag_blocks = (ti // 16) == (si // 16)
    lane = lax.broadcasted_iota(jnp.int32, (1, LANE), 1)
    head_masks = [lane < HEAD_A, lane >= HEAD_A]
    folds = [f0_ref[...], f1_ref[...]]
    r, v, kap = r_ref[...], v_ref[...], kap_ref[...]
    for d, (kd_ref, al_ref, lw_ref, dd_ref) in enumerate(((kdf_ref, alf_ref, lwf_ref, ddf_ref),
                                                          (kdb_ref, alb_ref, lwb_ref, ddb_ref))):
        incl = same_chunk & ((si <= ti) if d == 0 else (si >= ti))
        strict = same_chunk & ((si < ti) if d == 0 else (si > ti))
        res = _rwkv_group(r, v, kap, kd_ref[...], al_ref[...], lw_ref[...], dd_ref[...],
                          incl, strict, eye, diag_blocks, head_masks, folds)
        outs = out_refs[d * N_LOCAL:(d + 1) * N_LOCAL]
        for o_ref, val in zip(outs[:-1], res[:-1]):
            o_ref[...] = val.astype(o_ref.dtype)
        p_tot = res[-1]
        outs[-1][...] = jnp.concatenate([p_tot[c * C:c * C + 8] for c in range(G // C)], axis=0)


def rwkv_local(r, v, kap, kd_f, kd_b, al_f, al_b, lw_f, lw_b):
    T, W = r.shape
    G = RWKV_GROUP
    ddf, ddb, f0, f1 = _rwkv_constants()
    blk = pl.BlockSpec((G, LANE), lambda i, h: (i, h))
    pblk = pl.BlockSpec((G // 8, LANE), lambda i, h: (i, h))
    const = lambda a: pl.BlockSpec(a.shape, lambda i, h: (0, 0))
    b16 = jax.ShapeDtypeStruct((T, W), BF16)
    f32 = jax.ShapeDtypeStruct((T, W), F32)
    per_dir = (b16, b16, b16, b16, b16, f32, f32, jax.ShapeDtypeStruct((T // 8, W), F32))
    outs = pl.pallas_call(
        _rwkv_local_kernel,
        out_shape=per_dir * 2,
        grid=(T // G, W // LANE),
        in_specs=[blk] * 9 + [const(ddf), const(ddb), const(f0), const(f1)],
        out_specs=((blk,) * 7 + (pblk,)) * 2,
        compiler_params=_cparams("parallel", "parallel"),
        name="rwkv_local",
    )(r, v, kap, kd_f, kd_b, al_f, al_b, lw_f, lw_b,
      jnp.asarray(ddf, BF16), jnp.asarray(ddb, BF16), jnp.asarray(f0, BF16), jnp.asarray(f1, BF16))
    return outs[:N_LOCAL], outs[N_LOCAL:]


def _rwkv_scan_chunk(kw, rh, ra_cat, kout, aout, uv, y0, v, p_tot, st, head_masks, same_head):
    C = CHUNK
    m1 = _dot_nt(jnp.concatenate([kw, rh], axis=0), st.astype(BF16))
    u = m1[:C] + uv
    u_cat = jnp.concatenate([jnp.where(hm, u, 0.0).astype(BF16) for hm in head_masks], axis=0)
    y = m1[C:] + y0 - _dot(ra_cat, u_cat)
    lhs = jnp.concatenate([v.astype(BF16), (-u).astype(BF16)], axis=0)
    rhs = jnp.concatenate([kout, aout], axis=0)
    st_new = jnp.where(same_head, st * p_tot + _dot_tn(lhs, rhs), 0.0)
    return y, st_new


def _rwkv_scan_kernel(*refs, n_chunks):
    C = CHUNK
    f_refs, vf_ref = refs[:N_LOCAL], refs[N_LOCAL]
    b_refs, vb_ref = refs[N_LOCAL + 1:2 * N_LOCAL + 1], refs[2 * N_LOCAL + 1]
    yf_ref, yb_ref, stf_ref, stb_ref = refs[2 * N_LOCAL + 2:]

    @pl.when(pl.program_id(2) == 0)
    def _():
        stf_ref[...] = jnp.zeros_like(stf_ref)
        stb_ref[...] = jnp.zeros_like(stb_ref)

    lane = lax.broadcasted_iota(jnp.int32, (1, LANE), 1)
    head_masks = [lane < HEAD_A, lane >= HEAD_A]
    vi = lax.broadcasted_iota(jnp.int32, (LANE, LANE), 0)
    ki = lax.broadcasted_iota(jnp.int32, (LANE, LANE), 1)
    same_head = (vi // HEAD_A) == (ki // HEAD_A)

    def rows(ref, c, n):
        return ref[pl.ds(pl.multiple_of(c * n, n), n), :]

    def step(d_refs, v_ref, y_ref, c, st):
        args = [rows(ref, c, C) for ref in d_refs[:-1]] + [rows(v_ref, c, C), rows(d_refs[-1], c, 8)[0:1]]
        y, st = _rwkv_scan_chunk(*args, st, head_masks, same_head)
        y_ref[pl.ds(pl.multiple_of(c * C, C), C), :] = y
        return st

    def body(c, carry):
        st_f, st_b = carry
        return (step(f_refs, vf_ref, yf_ref, c, st_f),
                step(b_refs, vb_ref, yb_ref, n_chunks - 1 - c, st_b))

    st_f, st_b = lax.fori_loop(0, n_chunks, body, (stf_ref[...], stb_ref[...]), unroll=2)
    stf_ref[...] = st_f
    stb_ref[...] = st_b


def rwkv_scan(loc_f, loc_b, v, batch, *, ts=512):
    T, W = v.shape
    S = T // batch
    ts = min(ts, S)
    ns = S // ts
    fwd = lambda rows: pl.BlockSpec((rows, LANE), lambda b, h, s: (b * ns + s, h))
    bwd = lambda rows: pl.BlockSpec((rows, LANE), lambda b, h, s: (b * ns + ns - 1 - s, h))
    specs = lambda mk: [mk(ts)] * (N_LOCAL - 1) + [mk(ts // 8), mk(ts)]
    out = jax.ShapeDtypeStruct((T, W), F32)
    return pl.pallas_call(
        functools.partial(_rwkv_scan_kernel, n_chunks=ts // CHUNK),
        out_shape=(out, out),
        grid=(batch, W // LANE, ns),
        in_specs=specs(fwd) + specs(bwd),
        out_specs=(fwd(ts), bwd(ts)),
        scratch_shapes=[pltpu.VMEM((LANE, LANE), F32), pltpu.VMEM((LANE, LANE), F32)],
        compiler_params=_cparams("parallel", "parallel", "arbitrary"),
        name="rwkv_scan",
    )(*loc_f, v, *loc_b, v)


def _rwkv_out_kernel(yf_ref, yb_ref, bon_ref, g_ref, gw_ref, gb_ref, bd_ref, o_ref, *, width):
    bd = bd_ref[...]
    inv_n = 1.0 / HEAD_A
    for c in range(width // LANE):
        sl = slice(c * LANE, (c + 1) * LANE)
        y = yf_ref[:, sl] + yb_ref[:, sl]
        mean = _head_sum(y, bd) * inv_n
        d = y - mean
        var = _head_sum(d * d, bd) * inv_n
        yn = d * lax.rsqrt(var + GN_EPS) * gw_ref[:, sl] + gb_ref[:, sl] + bon_ref[:, sl]
        o_ref[:, sl] = (yn * g_ref[:, sl]).astype(o_ref.dtype)


def rwkv_out(y_f, y_b, bonus, g, gn_w, gn_b, *, tm=512):
    T, W = y_f.shape
    bd = np.kron(np.eye(LANE // HEAD_A, dtype=np.float32), np.ones((HEAD_A, HEAD_A), np.float32))
    row = pl.BlockSpec((tm, W), lambda i: (i, 0))
    vec = pl.BlockSpec((1, W), lambda i: (0, 0))
    return pl.pallas_call(
        functools.partial(_rwkv_out_kernel, width=W),
        out_shape=jax.ShapeDtypeStruct((T, W), BF16),
        grid=(T // tm,),
        in_specs=[row, row, row, row, vec, vec, pl.BlockSpec((LANE, LANE), lambda i: (0, 0))],
        out_specs=row,
        compiler_params=_cparams("parallel"),
        name="rwkv_out",
    )(y_f, y_b, bonus, g, gn_w.reshape(1, W), gn_b.reshape(1, W), jnp.asarray(bd, BF16))


def rwkv7_mixer(z, mu, w0, w2, a0, a2, g2, k_k, k_a, r_k, gn_w, gn_b, batch):
    r, v, kap, kd_f, kd_b, al_f, al_b, lw_f, lw_b, g, bonus = rwkv_prep(
        z, mu, w0, w2, a0, a2, g2, k_k, k_a, r_k.reshape(-1), batch)
    loc_f, loc_b = rwkv_local(r, v, kap, kd_f, kd_b, al_f, al_b, lw_f, lw_b)
    y_f, y_b = rwkv_scan(loc_f, loc_b, v, batch)
    return rwkv_out(y_f, y_b, bonus, g, gn_w, gn_b)


def _relu2(acc):
    r = jnp.maximum(acc, 0.0)
    return r * r


def _add(acc, res):
    return acc + res


def _pad_cols(w, mult):
    return jnp.pad(w, ((0, 0), (0, (-w.shape[1]) % mult)))


def kernel(x, p, positions, ln1_g, w_in, rwkv_mu, rwkv_w0, rwkv_w2, rwkv_a0, rwkv_a2, rwkv_g2, rwkv_kk, rwkv_ka, rwkv_rk, rwkv_gn_w, rwkv_gn_b, hgrn_lb, hgrn_norm_g, mla_q_norm_g, mla_kv_norm_g, mla_w_uq, mla_w_ukv, w_branch, w_o, ln2_g, w_mlp1, w_mlp2, w_pe, w_pg, final_g):
    Bn, S, D = x.shape
    L = w_in.shape[0]
    T = Bn * S
    W = rwkv_w0.shape[-1]
    q_lora, kv_lora = mla_q_norm_g.shape[-1], mla_kv_norm_g.shape[-1]
    n_heads_c = mla_w_ukv.shape[-1] // (QK_NOPE + V_HEAD)
    rwkv_w = 3 * W + 2 * LORA_DECAY + 2 * LORA_AAA + LORA_GATE
    hgrn_w = 5 * W
    o_hgrn = rwkv_w
    o_cq = o_hgrn + hgrn_w
    o_ckv = o_cq + q_lora
    o_kr = o_ckv + kv_lora
    o_gate = o_kr + QK_ROPE
    half = QK_ROPE // 2

    lb_w = jax.nn.softmax(hgrn_lb.astype(F32), axis=0)
    lower_bounds = jnp.cumsum(lb_w, axis=0) - lb_w[0]
    cq_tab, sq_tab, ck_tab, sk_tab = _rope_tables(positions)

    h = x.reshape(T, D)
    for l in range(L):
        wl = w_in[l]
        w_rwkv = _pad_cols(wl[:, :rwkv_w], 512).astype(BF16)
        w_hgrn = wl[:, o_hgrn:o_cq].astype(BF16)
        w_cq = wl[:, o_cq:o_ckv].astype(BF16)
        w_kr = wl[:, o_kr:o_gate]
        w_kv = jnp.concatenate(
            [wl[:, o_ckv:o_kr], _pad_cols(w_kr, LANE),
             _pad_cols(jnp.concatenate([w_kr[:, half:], w_kr[:, :half]], axis=1), LANE)], axis=1).astype(BF16)
        w_gate = wl[:, o_gate:].astype(BF16)

        hn = rmsnorm(h, ln1_g[l], BF16)
        z_rwkv = matmul(hn, w_rwkv, out_dtype=F32, tm=1024, tn=512, name="in_rwkv")
        z_hgrn = matmul(hn, w_hgrn, out_dtype=F32, tm=1024, tn=512, name="in_hgrn")
        z_cq = matmul(hn, w_cq, out_dtype=F32, tm=1024, tn=w_cq.shape[1], name="in_cq")
        z_kv = matmul(hn, w_kv, out_dtype=F32, tm=1024, tn=w_kv.shape[1], name="in_kv")
        z_gate = matmul(hn, w_gate, out_dtype=F32, tm=1024, tn=512, name="in_gate")

        y_a = rwkv7_mixer(z_rwkv, rwkv_mu[l], rwkv_w0[l], rwkv_w2[l], rwkv_a0[l], rwkv_a2[l], rwkv_g2[l],
                          rwkv_kk[l], rwkv_ka[l], rwkv_rk[l], rwkv_gn_w[l], rwkv_gn_b[l], Bn)
        y_b = hgrn2_mixer(z_hgrn, lower_bounds[l], hgrn_norm_g[l], Bn)
        wqa, wqb = _mla_q_weights(mla_w_uq[l], n_heads_c)
        q = mla_q_proj(z_cq, mla_q_norm_g[l], wqa, wqb, cq_tab, sq_tab, n_heads_c)
        k, v = mla_kv_proj(z_kv, mla_kv_norm_g[l], mla_w_ukv[l].astype(BF16), ck_tab, sk_tab, n_heads_c)
        y_c = mla_attention(q, k, v, Bn, n_heads_c)

        mixed = branch_mix(y_a, y_b, y_c, w_branch[l].astype(BF16), z_gate)
        h = matmul(mixed, w_o[l].astype(BF16), out_dtype=F32, tm=1024, tn=1024, epilogue=_add,
                   extras=(h,), alias_extra=0, name="w_o")
        hn = rmsnorm(h, ln2_g[l], BF16)
        hid = matmul(hn, w_mlp1[l].astype(BF16), out_dtype=BF16, tm=1024, tn=1024, epilogue=_relu2,
                     name="mlp1")
        h = matmul(hid, w_mlp2[l].astype(BF16), out_dtype=F32, tm=1024, tn=1024, tk=2048, epilogue=_add,
                   extras=(h,), alias_extra=0, name="mlp2")
        h = ple_update(h, w_pg[l].astype(BF16), p[l].reshape(T, -1), w_pe[l].astype(BF16))
    return rmsnorm(h, final_g, F32).reshape(Bn, S, D)
```

```python
import functools

import numpy as np
import jax
import jax.numpy as jnp
from jax import lax
from jax.experimental import pallas as pl
from jax.experimental.pallas import tpu as pltpu

F32 = jnp.float32
BF16 = jnp.bfloat16

LANE = 128
VMEM_LIMIT = 48 * 2**20

HEAD_A = 64
LORA_DECAY = 64
LORA_AAA = 64
LORA_GATE = 160
DECAY_SCALE = 0.606531
GN_EPS = 64e-5
HEAD_B = 128
F_TINY = 1e-30
QK_NOPE = 128
QK_ROPE = 64
V_HEAD = 128
ROPE_THETA = 10000.0
NORM_EPS = 1e-6
CHUNK = 64
LOG2_E = 1.4426950408889634


def _cparams(*sem, flags=None):
    return pltpu.CompilerParams(dimension_semantics=sem, vmem_limit_bytes=VMEM_LIMIT, flags=flags)


def _sigmoid(x):
    return 1.0 / (1.0 + jnp.exp(-x))


def _dot(a, b):
    return jnp.dot(a, b, preferred_element_type=F32)


def _dot_nt(a, b):
    return lax.dot_general(a, b, (((1,), (1,)), ((), ())), preferred_element_type=F32)


def _dot_tn(a, b):
    return lax.dot_general(a, b, (((0,), (0,)), ((), ())), preferred_element_type=F32)


def _split3(x):
    x1 = x.astype(BF16)
    r1 = x - x1.astype(F32)
    x2 = r1.astype(BF16)
    x3 = (r1 - x2.astype(F32)).astype(BF16)
    return x1, x2, x3


def _rmsnorm_kernel(x_ref, g_ref, o_ref):
    x = x_ref[...]
    ms = jnp.mean(x * x, axis=-1, keepdims=True)
    o_ref[...] = (x * lax.rsqrt(ms + NORM_EPS) * g_ref[...]).astype(o_ref.dtype)


def rmsnorm(x, g, out_dtype, tm=512):
    T, D = x.shape
    return pl.pallas_call(
        _rmsnorm_kernel,
        out_shape=jax.ShapeDtypeStruct((T, D), out_dtype),
        grid=(T // tm,),
        in_specs=[pl.BlockSpec((tm, D), lambda i: (i, 0)),
                  pl.BlockSpec((1, D), lambda i: (0, 0))],
        out_specs=pl.BlockSpec((tm, D), lambda i: (i, 0)),
        compiler_params=_cparams("parallel"),
        name="rmsnorm",
    )(x, g.reshape(1, D))


def _mm_kernel(a_ref, w_ref, *rest, nk, epilogue, n_extra):
    extras = rest[:n_extra]
    o_ref = rest[n_extra]

    def finish(acc):
        o_ref[...] = epilogue(acc, *[e[...] for e in extras]).astype(o_ref.dtype)

    if nk == 1:
        finish(_dot(a_ref[...], w_ref[...]))
    else:
        acc_ref = rest[n_extra + 1]
        k = pl.program_id(2)

        @pl.when(k == 0)
        def _():
            acc_ref[...] = jnp.zeros_like(acc_ref)

        acc_ref[...] += _dot(a_ref[...], w_ref[...])

        @pl.when(k == nk - 1)
        def _():
            finish(acc_ref[...])


def matmul(a, w, *, out_dtype, tm, tn, tk=None, epilogue=None, extras=(), alias_extra=None,
           name="matmul"):
    M, K = a.shape
    N = w.shape[1]
    tk = K if tk is None else tk
    nk = K // tk
    epilogue = epilogue or (lambda acc: acc)
    kern = functools.partial(_mm_kernel, nk=nk, epilogue=epilogue, n_extra=len(extras))
    in_specs = [pl.BlockSpec((tm, tk), lambda i, j, k: (i, k)),
                pl.BlockSpec((tk, tn), lambda i, j, k: (k, j))]
    in_specs += [pl.BlockSpec((tm, tn), lambda i, j, k: (i, j)) for _ in extras]
    aliases = {} if alias_extra is None else {2 + alias_extra: 0}
    return pl.pallas_call(
        kern,
        out_shape=jax.ShapeDtypeStruct((M, N), out_dtype),
        grid=(M // tm, N // tn, nk),
        in_specs=in_specs,
        out_specs=pl.BlockSpec((tm, tn), lambda i, j, k: (i, j)),
        scratch_shapes=[pltpu.VMEM((tm, tn), F32)] if nk > 1 else [],
        input_output_aliases=aliases,
        compiler_params=_cparams("parallel", "parallel", "arbitrary"),
        name=name,
    )(a, w, *extras)


def _branch_kernel(ya_ref, yb_ref, yc_ref, p_ref, ga_ref, gb_ref, gc_ref, o_ref):
    acc = _sigmoid(ga_ref[...]) * _dot(ya_ref[...], p_ref[0])
    acc += _sigmoid(gb_ref[...]) * _dot(yb_ref[...], p_ref[1])
    acc += _sigmoid(gc_ref[...]) * _dot(yc_ref[...], p_ref[2])
    o_ref[...] = acc.astype(o_ref.dtype)


def branch_mix(ya, yb, yc, p, zg, *, tm=1024, tn=512):
    T, W = ya.shape
    D = p.shape[2]
    nj = D // tn
    y_spec = pl.BlockSpec((tm, W), lambda i, j: (i, 0))
    g_specs = [pl.BlockSpec((tm, tn), functools.partial(lambda i, j, n: (i, n * nj + j), n=n))
               for n in range(3)]
    return pl.pallas_call(
        _branch_kernel,
        out_shape=jax.ShapeDtypeStruct((T, D), BF16),
        grid=(T // tm, nj),
        in_specs=[y_spec, y_spec, y_spec,
                  pl.BlockSpec((3, W, tn), lambda i, j: (0, 0, j))] + g_specs,
        out_specs=pl.BlockSpec((tm, tn), lambda i, j: (i, j)),
        compiler_params=_cparams("parallel", "parallel"),
        name="branch_mix",
    )(ya, yb, yc, p, zg, zg, zg)


def _ple_kernel(h_ref, wpg_ref, p_ref, wpe_ref, hres_ref, o_ref):
    gate = _sigmoid(_dot(h_ref[...].astype(BF16), wpg_ref[...]))
    emb = _dot(p_ref[...].astype(BF16), wpe_ref[...])
    o_ref[...] = hres_ref[...] + gate * emb


def ple_update(h, wpg, p, wpe, *, tm=512, tn=512):
    T, D = h.shape
    E = p.shape[1]
    return pl.pallas_call(
        _ple_kernel,
        out_shape=jax.ShapeDtypeStruct((T, D), F32),
        grid=(T // tm, D // tn),
        in_specs=[pl.BlockSpec((tm, D), lambda i, j: (i, 0)),
                  pl.BlockSpec((D, tn), lambda i, j: (0, j)),
                  pl.BlockSpec((tm, E), lambda i, j: (i, 0)),
                  pl.BlockSpec((E, tn), lambda i, j: (0, j)),
                  pl.BlockSpec((tm, tn), lambda i, j: (i, j))],
        out_specs=pl.BlockSpec((tm, tn), lambda i, j: (i, j)),
        compiler_params=_cparams("parallel", "parallel"),
        name="ple_update",
    )(h, wpg, p, wpe, h)


def _mla_q_kernel(cq_ref, g_ref, wa_ref, wb_ref, c_ref, s_ref, o_ref, xn_ref):
    @pl.when(pl.program_id(1) == 0)
    def _():
        x = cq_ref[...]
        ms = jnp.mean(x * x, axis=-1, keepdims=True)
        xn_ref[...] = (x * lax.rsqrt(ms + NORM_EPS) * g_ref[...]).astype(BF16)

    xn = xn_ref[...]
    o_ref[...] = (_dot(xn, wa_ref[...]) * c_ref[...] + _dot(xn, wb_ref[...]) * s_ref[...]).astype(BF16)


def mla_q_proj(cq, g, wa, wb, ctab, stab, n_heads, *, tm=512):
    T, R = cq.shape
    HW = wa.shape[1] // n_heads
    return pl.pallas_call(
        _mla_q_kernel,
        out_shape=jax.ShapeDtypeStruct((T, n_heads * HW), BF16),
        grid=(T // tm, n_heads),
        in_specs=[pl.BlockSpec((tm, R), lambda i, h: (i, 0)),
                  pl.BlockSpec((1, R), lambda i, h: (0, 0)),
                  pl.BlockSpec((R, HW), lambda i, h: (0, h)),
                  pl.BlockSpec((R, HW), lambda i, h: (0, h)),
                  pl.BlockSpec((tm, HW), lambda i, h: (i, 0)),
                  pl.BlockSpec((tm, HW), lambda i, h: (i, 0))],
        out_specs=pl.BlockSpec((tm, HW), lambda i, h: (i, h)),
        scratch_shapes=[pltpu.VMEM((tm, R), BF16)],
        compiler_params=_cparams("parallel", "arbitrary"),
        name="mla_q_proj",
    )(cq, g.reshape(1, R), wa, wb, ctab, stab)


def _mla_kv_kernel(z_ref, g_ref, w_ref, c_ref, s_ref, k_ref, v_ref, xn_ref, *, kv_lora):
    @pl.when(pl.program_id(1) == 0)
    def _():
        x = z_ref[:, :kv_lora]
        ms = jnp.mean(x * x, axis=-1, keepdims=True)
        xn_ref[...] = (x * lax.rsqrt(ms + NORM_EPS) * g_ref[...]).astype(BF16)

    kv = _dot(xn_ref[...], w_ref[...])
    kr = z_ref[:, kv_lora:kv_lora + LANE] * c_ref[...] + z_ref[:, kv_lora + LANE:] * s_ref[...]
    k_ref[:, :QK_NOPE] = kv[:, :QK_NOPE].astype(BF16)
    k_ref[:, QK_NOPE:] = kr.astype(BF16)
    v_ref[...] = kv[:, QK_NOPE:].astype(BF16)


def mla_kv_proj(zkv, g, w, ctab, stab, n_heads, *, tm=512):
    T, ZW = zkv.shape
    R = ZW - 2 * LANE
    return pl.pallas_call(
        functools.partial(_mla_kv_kernel, kv_lora=R),
        out_shape=(jax.ShapeDtypeStruct((T, n_heads * 2 * LANE), BF16),
                   jax.ShapeDtypeStruct((T, n_heads * V_HEAD), BF16)),
        grid=(T // tm, n_heads),
        in_specs=[pl.BlockSpec((tm, ZW), lambda i, h: (i, 0)),
                  pl.BlockSpec((1, R), lambda i, h: (0, 0)),
                  pl.BlockSpec((R, QK_NOPE + V_HEAD), lambda i, h: (0, h)),
                  pl.BlockSpec((tm, LANE), lambda i, h: (i, 0)),
                  pl.BlockSpec((tm, LANE), lambda i, h: (i, 0))],
        out_specs=(pl.BlockSpec((tm, 2 * LANE), lambda i, h: (i, h)),
                   pl.BlockSpec((tm, V_HEAD), lambda i, h: (i, h))),
        scratch_shapes=[pltpu.VMEM((tm, R), BF16)],
        compiler_params=_cparams("parallel", "arbitrary"),
        name="mla_kv_proj",
    )(zkv, g.reshape(1, R), w, ctab, stab)


def _attn_kernel(q_ref, k_ref, v_ref, o_ref):
    s = _dot_nt(q_ref[...], k_ref[...])
    m = jnp.max(s, axis=-1, keepdims=True)
    p = jnp.exp(s - m)
    l = jnp.sum(p, axis=-1, keepdims=True)
    o = _dot(p.astype(BF16), v_ref[...])
    o_ref[...] = (o / l).astype(o_ref.dtype)


def mla_attention(q, k, v, batch, n_heads, *, tq=256):
    T = q.shape[0]
    S = T // batch
    nq = S // tq
    QW = q.shape[1] // n_heads
    return pl.pallas_call(
        _attn_kernel,
        out_shape=jax.ShapeDtypeStruct((T, n_heads * V_HEAD), BF16),
        grid=(batch, n_heads, nq),
        in_specs=[pl.BlockSpec((tq, QW), lambda b, h, i: (b * nq + i, h)),
                  pl.BlockSpec((S, QW), lambda b, h, i: (b, h)),
                  pl.BlockSpec((S, V_HEAD), lambda b, h, i: (b, h))],
        out_specs=pl.BlockSpec((tq, V_HEAD), lambda b, h, i: (b * nq + i, h)),
        compiler_params=_cparams("parallel", "parallel", "arbitrary"),
        name="mla_attention",
    )(q, k, v)


def _rope_tables(positions):
    inv_freq = 1.0 / (ROPE_THETA ** (jnp.arange(0, QK_ROPE, 2, dtype=F32) / QK_ROPE))
    ang = positions.astype(F32).reshape(-1, 1) * inv_freq
    cos, sin = jnp.cos(ang), jnp.sin(ang)
    T = ang.shape[0]
    scale = (QK_NOPE + QK_ROPE) ** -0.5
    z64 = jnp.zeros((T, LANE - QK_ROPE), F32)
    ck = jnp.concatenate([cos, cos, z64], axis=1)
    sk = jnp.concatenate([-sin, sin, z64], axis=1)
    cq = jnp.concatenate([jnp.ones((T, QK_NOPE), F32), ck], axis=1) * scale
    sq = jnp.concatenate([jnp.zeros((T, QK_NOPE), F32), sk], axis=1) * scale
    return cq, sq, ck, sk


def _mla_q_weights(w_uq, n_heads):
    R = w_uq.shape[0]
    w = w_uq.reshape(R, n_heads, QK_NOPE + QK_ROPE)
    half = QK_ROPE // 2
    zpad = jnp.zeros((R, n_heads, LANE - QK_ROPE), w.dtype)
    wa = jnp.concatenate([w, zpad], axis=2)
    wb = jnp.concatenate([jnp.zeros((R, n_heads, QK_NOPE), w.dtype),
                          w[:, :, QK_NOPE + half:], w[:, :, QK_NOPE:QK_NOPE + half], zpad], axis=2)
    return (wa.reshape(R, n_heads * 2 * LANE).astype(BF16),
            wb.reshape(R, n_heads * 2 * LANE).astype(BF16))


N_LEVELS = 6
HGRN_BATCH = 4


def _gla_constants():
    C = CHUNK
    t = np.arange(C)[:, None]
    u = np.arange(C)[None, :]
    mk = np.zeros((N_LEVELS + 1, C, C), np.float32)
    for l in range(N_LEVELS):
        m = C >> (l + 1)
        mk[l] = (t // (2 * m) == u // (2 * m)) & ((t % (2 * m)) >= m) & ((u % (2 * m)) < m)
    mk[N_LEVELS] = t == u
    tri = (u <= t).astype(np.float32)
    to2d = lambda a: np.ascontiguousarray(a).reshape(-1, C)
    return tri, np.ascontiguousarray(tri[::-1, ::-1]), to2d(mk), to2d(mk[:, ::-1, ::-1])


def _level_exponent(b, g, m, rev, row):
    C = CHUNK
    if m >= 4:
        r = m if rev else m - 1
        b_mid = jnp.concatenate([jnp.broadcast_to(b[s + r:s + r + 1, :], (2 * m, b.shape[1]))
                                 for s in range(0, C, 2 * m)], axis=0)
        return -jnp.abs(b - b_mid)
    g_next = pltpu.roll(g, C - 1, 0)
    g_prev = pltpu.roll(g, 1, 0)
    if m == 2:
        p4 = row % 4
        if rev:
            return jnp.where(p4 == 3, g_prev, jnp.where(p4 == 2, 0.0, jnp.where(p4 == 1, g, g + g_next)))
        return jnp.where(p4 == 0, g_next, jnp.where(p4 == 1, 0.0, jnp.where(p4 == 2, g, g + g_prev)))
    return jnp.where((row % 2 == 1) != rev, g, 0.0)


def _gla_local(chunks):
    C = CHUNK
    row = lax.broadcasted_iota(jnp.int32, (C, 1), 0)
    gs = [c[3] * LOG2_E for c in chunks]
    parts = [_split3(g) for g in gs]
    bs = [_dot(c[4], p[0]) + _dot(c[4], p[1]) + _dot(c[4], p[2]) for c, p in zip(chunks, parts)]
    scores = [c[5][N_LEVELS * C:(N_LEVELS + 1) * C] * _dot_nt(c[0].astype(BF16), c[1].astype(BF16))
              for c in chunks]
    for l in range(N_LEVELS):
        pls = [jnp.exp2(_level_exponent(b, g, C >> (l + 1), c[6], row)) for c, b, g in zip(chunks, bs, gs)]
        prods = [_dot_nt((c[0] * p).astype(BF16), (c[1] * p).astype(BF16)) for c, p in zip(chunks, pls)]
        scores = [s + c[5][l * C:(l + 1) * C] * a for s, c, a in zip(scores, chunks, prods)]
    out = []
    for c, b, s in zip(chunks, bs, scores):
        q, k, v, rev = c[0], c[1], c[2], c[6]
        b_tot = b[0:1] if rev else b[C - 1:C]
        vb = v.astype(BF16)
        out.append((_dot(s.astype(BF16), vb), (q * jnp.exp2(b)).astype(BF16),
                    _dot_tn(vb, (k * jnp.exp2(b_tot - b)).astype(BF16)), jnp.exp2(b_tot)))
    return out


def _hgrn_kernel(q_ref, zf_ref, zb_ref, i_ref, g_ref, lb_ref, ng_ref, trif_ref, trib_ref, mkf_ref, mkb_ref,
                 o_ref, of_ref, ob_ref, *, n_chunks):
    C = CHUNK
    lb = lb_ref[...]
    one_m_lb = 1.0 - lb
    trif, trib, mkf, mkb = trif_ref[...], trib_ref[...], mkf_ref[...], mkb_ref[...]

    def gate(zz):
        f = lb + one_m_lb * _sigmoid(zz)
        return one_m_lb * _sigmoid(-zz), jnp.log(jnp.maximum(f, F_TINY))

    def load(ref, c):
        return ref[pl.ds(pl.multiple_of(c * C, C), C), :]

    def chunk_inputs(c, z_ref, tri, mk, rev):
        q = load(q_ref, c)
        k, g = gate(load(z_ref, c))
        return (q * _sigmoid(q), k, load(i_ref, c), g, tri, mk, rev)

    def body(it, carry):
        states = list(carry)
        ids = [[it * HGRN_BATCH + j for j in range(HGRN_BATCH)]]
        ids.append([n_chunks - 1 - c for c in ids[0]])
        loc = _gla_local([chunk_inputs(c, zf_ref, trif, mkf, False) for c in ids[0]]
                         + [chunk_inputs(c, zb_ref, trib, mkb, True) for c in ids[1]])
        for d, out_ref in enumerate((of_ref, ob_ref)):
            for j, c in enumerate(ids[d]):
                o_intra, q_dec, st_inc, p_tot = loc[d * HGRN_BATCH + j]
                out_ref[pl.ds(pl.multiple_of(c * C, C), C), :] = o_intra + _dot_nt(q_dec, states[d].astype(BF16))
                states[d] = states[d] * p_tot + st_inc
        return tuple(states)

    dv, dk = i_ref.shape[1], q_ref.shape[1]
    z = jnp.zeros((dv, dk), F32)
    lax.fori_loop(0, n_chunks // HGRN_BATCH, body, (z, z))

    o = of_ref[...] + ob_ref[...]
    ms = jnp.mean(o * o, axis=-1, keepdims=True)
    gg = g_ref[...]
    o_ref[...] = (o * lax.rsqrt(ms + NORM_EPS) * ng_ref[...] * (gg * _sigmoid(gg))).astype(o_ref.dtype)


def hgrn2_mixer(z, lb, norm_g, batch, *, col0=0):
    T = z.shape[0]
    W = lb.shape[-1]
    S = T // batch
    H = W // HEAD_B
    c0 = col0 // HEAD_B
    trif, trib, mkf, mkb = _gla_constants()
    part = lambda n: pl.BlockSpec((S, HEAD_B), functools.partial(lambda b, h, n: (b, c0 + n * H + h), n=n))
    const = lambda a: pl.BlockSpec(a.shape, lambda b, h: (0, 0))
    return pl.pallas_call(
        functools.partial(_hgrn_kernel, n_chunks=S // CHUNK),
        out_shape=jax.ShapeDtypeStruct((T, W), BF16),
        grid=(batch, H),
        in_specs=[part(0), part(1), part(2), part(3), part(4),
                  pl.BlockSpec((1, HEAD_B), lambda b, h: (0, h)),
                  pl.BlockSpec((1, HEAD_B), lambda b, h: (0, 0)),
                  const(trif), const(trib), const(mkf), const(mkb)],
        out_specs=pl.BlockSpec((S, HEAD_B), lambda b, h: (b, h)),
        scratch_shapes=[pltpu.VMEM((S, HEAD_B), F32), pltpu.VMEM((S, HEAD_B), F32)],
        compiler_params=_cparams("parallel", "parallel"),
        name="hgrn2_mixer",
    )(z, z, z, z, z, lb.reshape(1, W), norm_g.reshape(1, HEAD_B),
      jnp.asarray(trif, BF16), jnp.asarray(trib, BF16), jnp.asarray(mkf), jnp.asarray(mkb))


def _head_sum(x, bd):
    x1, x2, x3 = _split3(x)
    return _dot(x1, bd) + _dot(x2, bd) + _dot(x3, bd)


def _rwkv_prep_kernel(x_ref, xp_ref, xn_ref, mu_ref, w0_ref, w2_ref, a0_ref, a2_ref, g2_ref, kk_ref,
                      ka_ref, rk_ref, bd_ref,
                      r_ref, v_ref, kap_ref, kdf_ref, kdb_ref, alf_ref, alb_ref, lwf_ref, lwb_ref,
                      g_ref, bon_ref, *, tm, seq, width):
    W = width
    i = pl.program_id(0)
    x = x_ref[...]
    at_start = (i * tm) % seq == 0
    at_end = ((i + 1) * tm) % seq == 0
    prev_row = jnp.where(at_start, 0.0, xp_ref[7:8, :])
    next_row = jnp.where(at_end, 0.0, xn_ref[0:1, :])
    row = lax.broadcasted_iota(jnp.int32, (tm, 1), 0)
    x_prev = jnp.where(row == 0, prev_row, pltpu.roll(x, 1, 0))
    x_next = jnp.where(row == tm - 1, next_row, pltpu.roll(x, tm - 1, 0))
    u = x + mu_ref[0:1, :] * (x_prev - x) + mu_ref[1:2, :] * (x_next - x)

    r, k, v = u[:, :W], u[:, W:2 * W], u[:, 2 * W:3 * W]
    o = 3 * W
    wd = jnp.tanh(u[:, o:o + 2 * LORA_DECAY]).astype(BF16)
    o += 2 * LORA_DECAY
    ad = u[:, o:o + 2 * LORA_AAA].astype(BF16)
    o += 2 * LORA_AAA
    gd = _sigmoid(u[:, o:]).astype(BF16)

    bd = bd_ref[...]
    kkr = k * kk_ref[...]
    sq = kkr * kkr
    rks = []
    a_dir = []
    for n in range(2):
        w_raw = w0_ref[n:n + 1, :] + _dot(wd[:, n * LORA_DECAY:(n + 1) * LORA_DECAY], w2_ref[n])
        lw = -DECAY_SCALE * _sigmoid(w_raw)
        a = _sigmoid(a0_ref[n:n + 1, :] + _dot(ad[:, n * LORA_AAA:(n + 1) * LORA_AAA], a2_ref[n]))
        a_dir.append(a)
        (lwf_ref, lwb_ref)[n][...] = lw
    kd = [k * (1.0 + (a - 1.0) * ka_ref[...]) for a in a_dir]
    kdf_ref[...] = kd[0]
    kdb_ref[...] = kd[1]
    rkk = r * (kd[0] + kd[1]) * rk_ref[...]
    for c in range(W // LANE):
        sl = slice(c * LANE, (c + 1) * LANE)
        nrm = jnp.maximum(jnp.sqrt(_head_sum(sq[:, sl], bd)), 1e-12)
        kap = kkr[:, sl] / nrm
        kap_ref[:, sl] = kap
        alf_ref[:, sl] = kap * a_dir[0][:, sl]
        alb_ref[:, sl] = kap * a_dir[1][:, sl]
        bon_ref[:, sl] = _head_sum(rkk[:, sl], bd) * v[:, sl]
    r_ref[...] = r
    v_ref[...] = v
    g_ref[...] = _dot(gd, g2_ref[...])


def rwkv_prep(z, mu, w0, w2, a0, a2, g2, k_k, k_a, r_k, batch, *, tm=256):
    T, ZW = z.shape
    W = w0.shape[-1]
    S = T // batch
    nb8 = tm // 8
    used = 3 * W + 2 * LORA_DECAY + 2 * LORA_AAA + LORA_GATE
    mu_p = jnp.pad(mu, ((0, 0), (0, ZW - used)))
    g2_p = jnp.pad(g2, ((0, ZW - used), (0, 0))).astype(BF16)
    bd = np.kron(np.eye(LANE // HEAD_A, dtype=np.float32), np.ones((HEAD_A, HEAD_A), np.float32))
    full = lambda a: pl.BlockSpec(a.shape, lambda i: (0,) * a.ndim)
    vec = lambda a: a.reshape(1, W)
    args = [z, z, z, mu_p, w0, w2.astype(BF16), a0, a2.astype(BF16), g2_p, vec(k_k), vec(k_a),
            vec(r_k), jnp.asarray(bd, BF16)]
    in_specs = [pl.BlockSpec((tm, ZW), lambda i: (i, 0)),
                pl.BlockSpec((8, ZW), lambda i: (jnp.maximum(i * nb8 - 1, 0), 0)),
                pl.BlockSpec((8, ZW), lambda i: (jnp.minimum((i + 1) * nb8, T // 8 - 1), 0))]
    in_specs += [full(a) for a in args[3:]]
    out = jax.ShapeDtypeStruct((T, W), F32)
    return pl.pallas_call(
        functools.partial(_rwkv_prep_kernel, tm=tm, seq=S, width=W),
        out_shape=(out,) * 11,
        grid=(T // tm,),
        in_specs=in_specs,
        out_specs=(pl.BlockSpec((tm, W), lambda i: (i, 0)),) * 11,
        compiler_params=_cparams("parallel"),
        name="rwkv_prep",
    )(*args)


RWKV_GROUP = 256


def _rwkv_constants():
    G, C = RWKV_GROUP, CHUNK
    t = np.arange(G)[:, None]
    u = np.arange(G)[None, :]
    dd = (((t // C) == (u // C)) & (u <= t)).astype(np.float32)
    fold = np.zeros((2, G, LANE), np.float32)
    for hd in range(2):
        fold[hd, np.arange(G), hd * C + np.arange(G) % C] = 1.0
    return dd, np.ascontiguousarray(dd[::-1, ::-1]), fold[0], fold[1]


def _bdot(a, b):
    return _dot(a.astype(BF16), b.astype(BF16))


def _unit_tri_inverses(ms, eye, diag_blocks):
    each = lambda f, *ls: [f(*a) for a in zip(*ls)]
    mds = [jnp.where(diag_blocks, m, 0.0) for m in ms]
    mos = each(lambda m, md: m - md, ms, mds)
    xs = [eye - md for md in mds]
    ps = each(_bdot, mds, mds)
    for _ in range(2):
        xs = each(lambda x, p: x + _bdot(x, p), xs, ps)
        ps = each(_bdot, ps, ps)
    tds = each(lambda x, p: x + _bdot(x, p), xs, ps)
    gs = each(_bdot, tds, mos)
    g2s = each(_bdot, gs, gs)
    ys = each(lambda g, g2: (eye - g) + _bdot(eye - g, g2), gs, g2s)
    return each(_bdot, ys, tds)


def _rwkv_groups(r, v, kap, dirs, eye, diag_blocks, head_masks, folds):
    G = r.shape[0]
    pre = []
    for kd, al, lw, dd, rev, causal_incl, causal_strict in dirs:
        l1, l2, l3 = _split3(lw)
        b_in = _dot(dd, l1) + _dot(dd, l2) + _dot(dd, l3)
        r_tot = 0 if rev else CHUNK - 1
        b_tot = jnp.concatenate([jnp.broadcast_to(b_in[s + r_tot:s + r_tot + 1], (CHUNK, b_in.shape[1]))
                                 for s in range(0, G, CHUNK)], axis=0)
        b_ex, b_out = b_in - lw, b_tot - b_in
        p_neg = jnp.exp(-b_in)
        p_out = jnp.exp(b_out)
        pre.append(dict(kap_h=kap * jnp.exp(b_ex), r_h=r * jnp.exp(b_in), al_n=(al * p_neg).astype(BF16),
                        kd_n=(kd * p_neg).astype(BF16), kout=kd * p_out, aout=al * p_out, p_tot=jnp.exp(b_tot),
                        incl=causal_incl, strict=causal_strict))
    chains = [(p, hm, fold) for p in pre for hm, fold in zip(head_masks, folds)]
    kap_hb = [jnp.where(hm, p["kap_h"], 0.0).astype(BF16) for p, hm, _ in chains]
    r_hb = [jnp.where(hm, p["r_h"], 0.0).astype(BF16) for p, hm, _ in chains]
    v_h = [jnp.where(hm, v, 0.0).astype(BF16) for _, hm, _ in chains]
    ms = [jnp.where(p["strict"], _dot_nt(k_, p["al_n"]), 0.0) for (p, _, _), k_ in zip(chains, kap_hb)]
    ns = [jnp.where(p["strict"], _dot_nt(k_, p["kd_n"]), 0.0) for (p, _, _), k_ in zip(chains, kap_hb)]
    ras = [jnp.where(p["incl"], _dot_nt(r_, p["al_n"]), 0.0) for (p, _, _), r_ in zip(chains, r_hb)]
    rks = [jnp.where(p["incl"], _dot_nt(r_, p["kd_n"]), 0.0) for (p, _, _), r_ in zip(chains, r_hb)]
    nvs = [_bdot(n, vh).astype(BF16) for n, vh in zip(ns, v_h)]
    y0s = [_bdot(rk, vh) for rk, vh in zip(rks, v_h)]
    racs = [_bdot(ra, fold) for ra, (_, _, fold) in zip(ras, chains)]
    tinvs = [t.astype(BF16) for t in _unit_tri_inverses(ms, eye, diag_blocks)]
    kws = [_dot(t, k_) for t, k_ in zip(tinvs, kap_hb)]
    uvs = [_dot(t, nv) for t, nv in zip(tinvs, nvs)]
    out = []
    for d, p in enumerate(pre):
        a, b = 2 * d, 2 * d + 1
        out.append((kws[a] + kws[b], p["r_h"], racs[a] + racs[b], p["kout"], p["aout"], uvs[a] + uvs[b],
                    y0s[a] + y0s[b], p["p_tot"]))
    return out


N_LOCAL = 8


def _rwkv_local_kernel(r_ref, v_ref, kap_ref, kdf_ref, kdb_ref, alf_ref, alb_ref, lwf_ref, lwb_ref,
                       ddf_ref, ddb_ref, f0_ref, f1_ref, *out_refs):
    G, C = RWKV_GROUP, CHUNK
    ti = lax.broadcasted_iota(jnp.int32, (G, G), 0)
    si = lax.broadcasted_iota(jnp.int32, (G, G), 1)
    same_chunk = (ti // C) == (si // C)
    eye = (ti == si).astype(F32)
    diag_blocks = (ti // 16) == (si // 16)
    lane = lax.broadcasted_iota(jnp.int32, (1, LANE), 1)
    head_masks = [lane < HEAD_A, lane >= HEAD_A]
    folds = [f0_ref[...], f1_ref[...]]
    r, v, kap = r_ref[...], v_ref[...], kap_ref[...]
    dirs = []
    for d, (kd_ref, al_ref, lw_ref, dd_ref) in enumerate(((kdf_ref, alf_ref, lwf_ref, ddf_ref),
                                                          (kdb_ref, alb_ref, lwb_ref, ddb_ref))):
        incl = same_chunk & ((si <= ti) if d == 0 else (si >= ti))
        strict = same_chunk & ((si < ti) if d == 0 else (si > ti))
        dirs.append((kd_ref[...], al_ref[...], lw_ref[...], dd_ref[...], d == 1, incl, strict))
    for d, res in enumerate(_rwkv_groups(r, v, kap, dirs, eye, diag_blocks, head_masks, folds)):
        outs = out_refs[d * N_LOCAL:(d + 1) * N_LOCAL]
        for o_ref, val in zip(outs[:-1], res[:-1]):
            o_ref[...] = val.astype(o_ref.dtype)
        p_tot = res[-1]
        outs[-1][...] = jnp.concatenate([p_tot[c * C:c * C + 8] for c in range(G // C)], axis=0)


def rwkv_local(r, v, kap, kd_f, kd_b, al_f, al_b, lw_f, lw_b):
    T, W = r.shape
    G = RWKV_GROUP
    ddf, ddb, f0, f1 = _rwkv_constants()
    blk = pl.BlockSpec((G, LANE), lambda i, h: (i, h))
    pblk = pl.BlockSpec((G // 8, LANE), lambda i, h: (i, h))
    const = lambda a: pl.BlockSpec(a.shape, lambda i, h: (0, 0))
    b16 = jax.ShapeDtypeStruct((T, W), BF16)
    f32 = jax.ShapeDtypeStruct((T, W), F32)
    per_dir = (b16, b16, b16, b16, b16, f32, f32, jax.ShapeDtypeStruct((T // 8, W), F32))
    outs = pl.pallas_call(
        _rwkv_local_kernel,
        out_shape=per_dir * 2,
        grid=(T // G, W // LANE),
        in_specs=[blk] * 9 + [const(ddf), const(ddb), const(f0), const(f1)],
        out_specs=((blk,) * 7 + (pblk,)) * 2,
        compiler_params=_cparams("parallel", "parallel"),
        name="rwkv_local",
    )(r, v, kap, kd_f, kd_b, al_f, al_b, lw_f, lw_b,
      jnp.asarray(ddf, BF16), jnp.asarray(ddb, BF16), jnp.asarray(f0, BF16), jnp.asarray(f1, BF16))
    return outs[:N_LOCAL], outs[N_LOCAL:]


SCAN_PAIRS = 4


def _rwkv_scan_chunks(chains, head_masks, same_head):
    C = CHUNK
    m1s = [_dot_nt(jnp.concatenate([c[0], c[1]], axis=0), c[9].astype(BF16)) for c in chains]
    us = [m1[:C] + c[5] for m1, c in zip(m1s, chains)]
    incs = [_dot_tn(jnp.concatenate([c[7].astype(BF16), (-u).astype(BF16)], axis=0),
                    jnp.concatenate([c[3], c[4]], axis=0)) for u, c in zip(us, chains)]
    sts = [jnp.where(same_head, c[9] * c[8] + inc, 0.0) for c, inc in zip(chains, incs)]
    u_cats = [jnp.concatenate([jnp.where(hm, u, 0.0).astype(BF16) for hm in head_masks], axis=0) for u in us]
    ys = [m1[C:] + c[6] - _dot(c[2], u_cat) for m1, c, u_cat in zip(m1s, chains, u_cats)]
    return list(zip(ys, sts))


def _rwkv_scan_kernel(*refs, n_chunks, n_pairs):
    C = CHUNK
    f_refs, vf_ref = refs[:N_LOCAL], refs[N_LOCAL]
    b_refs, vb_ref = refs[N_LOCAL + 1:2 * N_LOCAL + 1], refs[2 * N_LOCAL + 1]
    yf_ref, yb_ref, stf_ref, stb_ref = refs[2 * N_LOCAL + 2:]

    @pl.when(pl.program_id(2) == 0)
    def _():
        stf_ref[...] = jnp.zeros_like(stf_ref)
        stb_ref[...] = jnp.zeros_like(stb_ref)

    lane = lax.broadcasted_iota(jnp.int32, (1, LANE), 1)
    head_masks = [lane < HEAD_A, lane >= HEAD_A]
    vi = lax.broadcasted_iota(jnp.int32, (LANE, LANE), 0)
    ki = lax.broadcasted_iota(jnp.int32, (LANE, LANE), 1)
    same_head = (vi // HEAD_A) == (ki // HEAD_A)

    def rows(ref, c, n):
        return ref[pl.ds(pl.multiple_of(c * n, n), n), :]

    def body(it, carry):
        states = [list(carry[0]), list(carry[1])]
        chains = []
        for d, (d_refs, v_ref) in enumerate(((f_refs, vf_ref), (b_refs, vb_ref))):
            c = it if d == 0 else n_chunks - 1 - it
            vals = [rows(ref, c, C) for ref in d_refs[:-1]] + [rows(v_ref, c, C), rows(d_refs[-1], c, 8)[0:1]]
            for p in range(n_pairs):
                chains.append(tuple(a[:, p * LANE:(p + 1) * LANE] for a in vals) + (states[d][p],))
        res = _rwkv_scan_chunks(chains, head_masks, same_head)
        for d, y_ref in enumerate((yf_ref, yb_ref)):
            c = it if d == 0 else n_chunks - 1 - it
            y_ref[pl.ds(pl.multiple_of(c * C, C), C), :] = jnp.concatenate(
                [res[d * n_pairs + p][0] for p in range(n_pairs)], axis=1)
            states[d] = [res[d * n_pairs + p][1] for p in range(n_pairs)]
        return tuple(states[0]), tuple(states[1])

    init = tuple(tuple(ref[p] for p in range(n_pairs)) for ref in (stf_ref, stb_ref))
    st_f, st_b = lax.fori_loop(0, n_chunks, body, init)
    for p in range(n_pairs):
        stf_ref[p] = st_f[p]
        stb_ref[p] = st_b[p]


def rwkv_scan(loc_f, loc_b, v, batch, *, ts=512):
    T, W = v.shape
    S = T // batch
    ts = min(ts, S)
    ns = S // ts
    bw = LANE * SCAN_PAIRS
    fwd = lambda rows: pl.BlockSpec((rows, bw), lambda b, h, s: (b * ns + s, h))
    bwd = lambda rows: pl.BlockSpec((rows, bw), lambda b, h, s: (b * ns + ns - 1 - s, h))
    specs = lambda mk: [mk(ts)] * (N_LOCAL - 1) + [mk(ts // 8), mk(ts)]
    out = jax.ShapeDtypeStruct((T, W), F32)
    state = pltpu.VMEM((SCAN_PAIRS, LANE, LANE), F32)
    return pl.pallas_call(
        functools.partial(_rwkv_scan_kernel, n_chunks=ts // CHUNK, n_pairs=SCAN_PAIRS),
        out_shape=(out, out),
        grid=(batch, W // bw, ns),
        in_specs=specs(fwd) + specs(bwd),
        out_specs=(fwd(ts), bwd(ts)),
        scratch_shapes=[state, state],
        compiler_params=_cparams("parallel", "parallel", "arbitrary"),
        name="rwkv_scan",
    )(*loc_f, v, *loc_b, v)


def _rwkv_out_kernel(yf_ref, yb_ref, bon_ref, g_ref, gw_ref, gb_ref, bd_ref, o_ref, *, width):
    bd = bd_ref[...]
    inv_n = 1.0 / HEAD_A
    for c in range(width // LANE):
        sl = slice(c * LANE, (c + 1) * LANE)
        y = yf_ref[:, sl] + yb_ref[:, sl]
        mean = _head_sum(y, bd) * inv_n
        d = y - mean
        var = _head_sum(d * d, bd) * inv_n
        yn = d * lax.rsqrt(var + GN_EPS) * gw_ref[:, sl] + gb_ref[:, sl] + bon_ref[:, sl]
        o_ref[:, sl] = (yn * g_ref[:, sl]).astype(o_ref.dtype)


def rwkv_out(y_f, y_b, bonus, g, gn_w, gn_b, *, tm=512):
    T, W = y_f.shape
    bd = np.kron(np.eye(LANE // HEAD_A, dtype=np.float32), np.ones((HEAD_A, HEAD_A), np.float32))
    row = pl.BlockSpec((tm, W), lambda i: (i, 0))
    vec = pl.BlockSpec((1, W), lambda i: (0, 0))
    return pl.pallas_call(
        functools.partial(_rwkv_out_kernel, width=W),
        out_shape=jax.ShapeDtypeStruct((T, W), BF16),
        grid=(T // tm,),
        in_specs=[row, row, row, row, vec, vec, pl.BlockSpec((LANE, LANE), lambda i: (0, 0))],
        out_specs=row,
        compiler_params=_cparams("parallel"),
        name="rwkv_out",
    )(y_f, y_b, bonus, g, gn_w.reshape(1, W), gn_b.reshape(1, W), jnp.asarray(bd, BF16))


def rwkv7_mixer(z, mu, w0, w2, a0, a2, g2, k_k, k_a, r_k, gn_w, gn_b, batch):
    r, v, kap, kd_f, kd_b, al_f, al_b, lw_f, lw_b, g, bonus = rwkv_prep(
        z, mu, w0, w2, a0, a2, g2, k_k, k_a, r_k.reshape(-1), batch)
    loc_f, loc_b = rwkv_local(r, v, kap, kd_f, kd_b, al_f, al_b, lw_f, lw_b)
    y_f, y_b = rwkv_scan(loc_f, loc_b, v, batch)
    return rwkv_out(y_f, y_b, bonus, g, gn_w, gn_b)


def _relu2(acc):
    r = jnp.maximum(acc, 0.0)
    return r * r


def _add(acc, res):
    return acc + res


def _pad_cols(w, mult):
    return jnp.pad(w, ((0, 0), (0, (-w.shape[1]) % mult)))


def kernel(x, p, positions, ln1_g, w_in, rwkv_mu, rwkv_w0, rwkv_w2, rwkv_a0, rwkv_a2, rwkv_g2, rwkv_kk, rwkv_ka, rwkv_rk, rwkv_gn_w, rwkv_gn_b, hgrn_lb, hgrn_norm_g, mla_q_norm_g, mla_kv_norm_g, mla_w_uq, mla_w_ukv, w_branch, w_o, ln2_g, w_mlp1, w_mlp2, w_pe, w_pg, final_g):
    Bn, S, D = x.shape
    L = w_in.shape[0]
    T = Bn * S
    W = rwkv_w0.shape[-1]
    q_lora, kv_lora = mla_q_norm_g.shape[-1], mla_kv_norm_g.shape[-1]
    n_heads_c = mla_w_ukv.shape[-1] // (QK_NOPE + V_HEAD)
    rwkv_w = 3 * W + 2 * LORA_DECAY + 2 * LORA_AAA + LORA_GATE
    hgrn_w = 5 * W
    o_hgrn = rwkv_w
    o_cq = o_hgrn + hgrn_w
    o_ckv = o_cq + q_lora
    o_kr = o_ckv + kv_lora
    o_gate = o_kr + QK_ROPE
    half = QK_ROPE // 2

    lb_w = jax.nn.softmax(hgrn_lb.astype(F32), axis=0)
    lower_bounds = jnp.cumsum(lb_w, axis=0) - lb_w[0]
    cq_tab, sq_tab, ck_tab, sk_tab = _rope_tables(positions)

    h = x.reshape(T, D)
    for l in range(L):
        wl = w_in[l]
        w_rwkv = _pad_cols(wl[:, :rwkv_w], 512).astype(BF16)
        w_hgrn = wl[:, o_hgrn:o_cq].astype(BF16)
        w_cq = wl[:, o_cq:o_ckv].astype(BF16)
        w_kr = wl[:, o_kr:o_gate]
        w_kv = jnp.concatenate(
            [wl[:, o_ckv:o_kr], _pad_cols(w_kr, LANE),
             _pad_cols(jnp.concatenate([w_kr[:, half:], w_kr[:, :half]], axis=1), LANE)], axis=1).astype(BF16)
        w_gate = wl[:, o_gate:].astype(BF16)

        hn = rmsnorm(h, ln1_g[l], BF16)
        z_rwkv = matmul(hn, w_rwkv, out_dtype=F32, tm=1024, tn=512, name="in_rwkv")
        z_hgrn = matmul(hn, w_hgrn, out_dtype=F32, tm=1024, tn=512, name="in_hgrn")
        z_cq = matmul(hn, w_cq, out_dtype=F32, tm=1024, tn=w_cq.shape[1], name="in_cq")
        z_kv = matmul(hn, w_kv, out_dtype=F32, tm=1024, tn=w_kv.shape[1], name="in_kv")
        z_gate = matmul(hn, w_gate, out_dtype=F32, tm=1024, tn=512, name="in_gate")

        y_a = rwkv7_mixer(z_rwkv, rwkv_mu[l], rwkv_w0[l], rwkv_w2[l], rwkv_a0[l], rwkv_a2[l], rwkv_g2[l],
                          rwkv_kk[l], rwkv_ka[l], rwkv_rk[l], rwkv_gn_w[l], rwkv_gn_b[l], Bn)
        y_b = hgrn2_mixer(z_hgrn, lower_bounds[l], hgrn_norm_g[l], Bn)
        wqa, wqb = _mla_q_weights(mla_w_uq[l], n_heads_c)
        q = mla_q_proj(z_cq, mla_q_norm_g[l], wqa, wqb, cq_tab, sq_tab, n_heads_c)
        k, v = mla_kv_proj(z_kv, mla_kv_norm_g[l], mla_w_ukv[l].astype(BF16), ck_tab, sk_tab, n_heads_c)
        y_c = mla_attention(q, k, v, Bn, n_heads_c)

        mixed = branch_mix(y_a, y_b, y_c, w_branch[l].astype(BF16), z_gate)
        h = matmul(mixed, w_o[l].astype(BF16), out_dtype=F32, tm=1024, tn=1024, epilogue=_add,
                   extras=(h,), alias_extra=0, name="w_o")
        hn = rmsnorm(h, ln2_g[l], BF16)
        hid = matmul(hn, w_mlp1[l].astype(BF16), out_dtype=BF16, tm=1024, tn=1024, epilogue=_relu2,
                     name="mlp1")
        h = matmul(hid, w_mlp2[l].astype(BF16), out_dtype=F32, tm=1024, tn=1024, tk=2048, epilogue=_add,
                   extras=(h,), alias_extra=0, name="mlp2")
        h = ple_update(h, w_pg[l].astype(BF16), p[l].reshape(T, -1), w_pe[l].astype(BF16))
    return rmsnorm(h, final_g, F32).reshape(Bn, S, D)
```

```python
import functools

import numpy as np
import jax
import jax.numpy as jnp
from jax import lax
from jax.experimental import pallas as pl
from jax.experimental.pallas import tpu as pltpu

F32 = jnp.float32
BF16 = jnp.bfloat16

LANE = 128
VMEM_LIMIT = 48 * 2**20

HEAD_A = 64
LORA_DECAY = 64
LORA_AAA = 64
LORA_GATE = 160
DECAY_SCALE = 0.606531
GN_EPS = 64e-5
HEAD_B = 128
F_TINY = 1e-30
QK_NOPE = 128
QK_ROPE = 64
V_HEAD = 128
ROPE_THETA = 10000.0
NORM_EPS = 1e-6
CHUNK = 64
LOG2_E = 1.4426950408889634
Q_SCALE = (QK_NOPE + QK_ROPE) ** -0.5 * LOG2_E


def _cparams(*sem, flags=None):
    return pltpu.CompilerParams(dimension_semantics=sem, vmem_limit_bytes=VMEM_LIMIT, flags=flags)


def _sigmoid(x):
    return 1.0 / (1.0 + jnp.exp(-x))


def _dot(a, b):
    return jnp.dot(a, b, preferred_element_type=F32)


def _dot_nt(a, b):
    return lax.dot_general(a, b, (((1,), (1,)), ((), ())), preferred_element_type=F32)


def _dot_tn(a, b):
    return lax.dot_general(a, b, (((0,), (0,)), ((), ())), preferred_element_type=F32)


def _split3(x):
    x1 = x.astype(BF16)
    r1 = x - x1.astype(F32)
    x2 = r1.astype(BF16)
    x3 = (r1 - x2.astype(F32)).astype(BF16)
    return x1, x2, x3


def _rmsnorm_kernel(x_ref, g_ref, o_ref):
    x = x_ref[...]
    ms = jnp.mean(x * x, axis=-1, keepdims=True)
    o_ref[...] = (x * lax.rsqrt(ms + NORM_EPS) * g_ref[...]).astype(o_ref.dtype)


def rmsnorm(x, g, out_dtype, tm=512):
    T, D = x.shape
    return pl.pallas_call(
        _rmsnorm_kernel,
        out_shape=jax.ShapeDtypeStruct((T, D), out_dtype),
        grid=(T // tm,),
        in_specs=[pl.BlockSpec((tm, D), lambda i: (i, 0)),
                  pl.BlockSpec((1, D), lambda i: (0, 0))],
        out_specs=pl.BlockSpec((tm, D), lambda i: (i, 0)),
        compiler_params=_cparams("parallel"),
        name="rmsnorm",
    )(x, g.reshape(1, D))


def _mm_kernel(a_ref, w_ref, *rest, nk, epilogue, n_extra):
    extras = rest[:n_extra]
    o_ref = rest[n_extra]

    def finish(acc):
        o_ref[...] = epilogue(acc, *[e[...] for e in extras]).astype(o_ref.dtype)

    if nk == 1:
        finish(_dot(a_ref[...], w_ref[...]))
    else:
        acc_ref = rest[n_extra + 1]
        k = pl.program_id(2)

        @pl.when(k == 0)
        def _():
            acc_ref[...] = jnp.zeros_like(acc_ref)

        acc_ref[...] += _dot(a_ref[...], w_ref[...])

        @pl.when(k == nk - 1)
        def _():
            finish(acc_ref[...])


def matmul(a, w, *, out_dtype, tm, tn, tk=None, epilogue=None, extras=(), alias_extra=None,
           name="matmul"):
    M, K = a.shape
    N = w.shape[1]
    tk = K if tk is None else tk
    nk = K // tk
    epilogue = epilogue or (lambda acc: acc)
    kern = functools.partial(_mm_kernel, nk=nk, epilogue=epilogue, n_extra=len(extras))
    in_specs = [pl.BlockSpec((tm, tk), lambda i, j, k: (i, k)),
                pl.BlockSpec((tk, tn), lambda i, j, k: (k, j))]
    in_specs += [pl.BlockSpec((tm, tn), lambda i, j, k: (i, j)) for _ in extras]
    aliases = {} if alias_extra is None else {2 + alias_extra: 0}
    return pl.pallas_call(
        kern,
        out_shape=jax.ShapeDtypeStruct((M, N), out_dtype),
        grid=(M // tm, N // tn, nk),
        in_specs=in_specs,
        out_specs=pl.BlockSpec((tm, tn), lambda i, j, k: (i, j)),
        scratch_shapes=[pltpu.VMEM((tm, tn), F32)] if nk > 1 else [],
        input_output_aliases=aliases,
        compiler_params=_cparams("parallel", "parallel", "arbitrary"),
        name=name,
    )(a, w, *extras)


def _branch_kernel(ya_ref, yb_ref, yc_ref, p_ref, ga_ref, gb_ref, gc_ref, o_ref):
    acc = _sigmoid(ga_ref[...]) * _dot(ya_ref[...], p_ref[0])
    acc += _sigmoid(gb_ref[...]) * _dot(yb_ref[...], p_ref[1])
    acc += _sigmoid(gc_ref[...]) * _dot(yc_ref[...], p_ref[2])
    o_ref[...] = acc.astype(o_ref.dtype)


def branch_mix(ya, yb, yc, p, zg, *, tm=1024, tn=512):
    T, W = ya.shape
    D = p.shape[2]
    nj = D // tn
    y_spec = pl.BlockSpec((tm, W), lambda i, j: (i, 0))
    g_specs = [pl.BlockSpec((tm, tn), functools.partial(lambda i, j, n: (i, n * nj + j), n=n))
               for n in range(3)]
    return pl.pallas_call(
        _branch_kernel,
        out_shape=jax.ShapeDtypeStruct((T, D), BF16),
        grid=(T // tm, nj),
        in_specs=[y_spec, y_spec, y_spec,
                  pl.BlockSpec((3, W, tn), lambda i, j: (0, 0, j))] + g_specs,
        out_specs=pl.BlockSpec((tm, tn), lambda i, j: (i, j)),
        compiler_params=_cparams("parallel", "parallel"),
        name="branch_mix",
    )(ya, yb, yc, p, zg, zg, zg)


def _ple_kernel(h_ref, wpg_ref, p_ref, wpe_ref, hres_ref, o_ref, hb_ref, pb_ref):
    @pl.when(pl.program_id(1) == 0)
    def _():
        hb_ref[...] = h_ref[...].astype(BF16)
        pb_ref[...] = p_ref[...].astype(BF16)

    gate = _sigmoid(_dot(hb_ref[...], wpg_ref[...]))
    emb = _dot(pb_ref[...], wpe_ref[...])
    o_ref[...] = hres_ref[...] + gate * emb


def ple_update(h, wpg, p, wpe, *, tm=512, tn=512):
    T, D = h.shape
    E = p.shape[1]
    return pl.pallas_call(
        _ple_kernel,
        out_shape=jax.ShapeDtypeStruct((T, D), F32),
        grid=(T // tm, D // tn),
        in_specs=[pl.BlockSpec((tm, D), lambda i, j: (i, 0)),
                  pl.BlockSpec((D, tn), lambda i, j: (0, j)),
                  pl.BlockSpec((tm, E), lambda i, j: (i, 0)),
                  pl.BlockSpec((E, tn), lambda i, j: (0, j)),
                  pl.BlockSpec((tm, tn), lambda i, j: (i, j))],
        out_specs=pl.BlockSpec((tm, tn), lambda i, j: (i, j)),
        scratch_shapes=[pltpu.VMEM((tm, D), BF16), pltpu.VMEM((tm, E), BF16)],
        compiler_params=_cparams("parallel", "arbitrary"),
        name="ple_update",
    )(h, wpg, p, wpe, h)


def _mla_q_kernel(cq_ref, g_ref, w_ref, c_ref, s_ref, o_ref, xn_ref, *, scale):
    @pl.when(pl.program_id(1) == 0)
    def _():
        x = cq_ref[...]
        ms = jnp.mean(x * x, axis=-1, keepdims=True)
        xn_ref[...] = (x * lax.rsqrt(ms + NORM_EPS) * g_ref[...]).astype(BF16)

    x = _dot(xn_ref[...], w_ref[...])
    hi = x[:, QK_NOPE:]
    o_ref[:, :QK_NOPE] = (x[:, :QK_NOPE] * scale).astype(BF16)
    o_ref[:, QK_NOPE:] = (hi * c_ref[...] + pltpu.roll(hi, LANE // 2, 1) * s_ref[...]).astype(BF16)


def mla_q_proj(cq, g, w, ctab, stab, n_heads, scale, *, tm=512):
    T, R = cq.shape
    HW = w.shape[1] // n_heads
    return pl.pallas_call(
        functools.partial(_mla_q_kernel, scale=scale),
        out_shape=jax.ShapeDtypeStruct((T, n_heads * HW), BF16),
        grid=(T // tm, n_heads),
        in_specs=[pl.BlockSpec((tm, R), lambda i, h: (i, 0)),
                  pl.BlockSpec((1, R), lambda i, h: (0, 0)),
                  pl.BlockSpec((R, HW), lambda i, h: (0, h)),
                  pl.BlockSpec((tm, LANE), lambda i, h: (i, 0)),
                  pl.BlockSpec((tm, LANE), lambda i, h: (i, 0))],
        out_specs=pl.BlockSpec((tm, HW), lambda i, h: (i, h)),
        scratch_shapes=[pltpu.VMEM((tm, R), BF16)],
        compiler_params=_cparams("parallel", "arbitrary"),
        name="mla_q_proj",
    )(cq, g.reshape(1, R), w, ctab, stab)


def _mla_kv_kernel(z_ref, g_ref, w_ref, c_ref, s_ref, k_ref, v_ref, xn_ref, *, kv_lora):
    @pl.when(pl.program_id(1) == 0)
    def _():
        x = z_ref[:, :kv_lora]
        ms = jnp.mean(x * x, axis=-1, keepdims=True)
        xn_ref[...] = (x * lax.rsqrt(ms + NORM_EPS) * g_ref[...]).astype(BF16)

    kv = _dot(xn_ref[...], w_ref[...])
    kr = z_ref[:, kv_lora:kv_lora + LANE] * c_ref[...] + z_ref[:, kv_lora + LANE:] * s_ref[...]
    k_ref[:, :QK_NOPE] = kv[:, :QK_NOPE].astype(BF16)
    k_ref[:, QK_NOPE:] = kr.astype(BF16)
    v_ref[...] = kv[:, QK_NOPE:].T.astype(BF16)


def mla_kv_proj(zkv, g, w, ctab, stab, n_heads, batch, *, tm=512):
    T, ZW = zkv.shape
    R = ZW - 2 * LANE
    S = T // batch
    tm = min(tm, S)
    nsb = S // tm
    return pl.pallas_call(
        functools.partial(_mla_kv_kernel, kv_lora=R),
        out_shape=(jax.ShapeDtypeStruct((T, n_heads * 2 * LANE), BF16),
                   jax.ShapeDtypeStruct((batch * n_heads * V_HEAD, S), BF16)),
        grid=(T // tm, n_heads),
        in_specs=[pl.BlockSpec((tm, ZW), lambda i, h: (i, 0)),
                  pl.BlockSpec((1, R), lambda i, h: (0, 0)),
                  pl.BlockSpec((R, QK_NOPE + V_HEAD), lambda i, h: (0, h)),
                  pl.BlockSpec((tm, LANE), lambda i, h: (i, 0)),
                  pl.BlockSpec((tm, LANE), lambda i, h: (i, 0))],
        out_specs=(pl.BlockSpec((tm, 2 * LANE), lambda i, h: (i, h)),
                   pl.BlockSpec((V_HEAD, tm), lambda i, h: ((i // nsb) * n_heads + h, i % nsb))),
        scratch_shapes=[pltpu.VMEM((tm, R), BF16)],
        compiler_params=_cparams("parallel", "arbitrary"),
        name="mla_kv_proj",
    )(zkv, g.reshape(1, R), w, ctab, stab)


ATTN_TILES = 4


def _attn_kernel(q_ref, k_ref, vt_ref, o_ref, *, tq):
    k = k_ref[...]
    tiles = [slice(j * tq, (j + 1) * tq) for j in range(ATTN_TILES)]
    scores = lambda t: _dot_nt(k, q_ref[t, :])

    def finish(t, st):
        p = jnp.exp2(st - jnp.max(st, axis=0, keepdims=True))
        ot = _dot(vt_ref[...], p.astype(BF16))
        o_ref[t, :] = (ot / jnp.sum(p, axis=0, keepdims=True)).T.astype(o_ref.dtype)

    st = scores(tiles[0])
    for j, t in enumerate(tiles):
        st_next = scores(tiles[j + 1]) if j + 1 < len(tiles) else None
        finish(t, st)
        st = st_next


def mla_attention(q, k, vt, batch, n_heads, *, tq=256):
    T = q.shape[0]
    S = T // batch
    tb = tq * ATTN_TILES
    nq = S // tb
    QW = q.shape[1] // n_heads
    return pl.pallas_call(
        functools.partial(_attn_kernel, tq=tq),
        out_shape=jax.ShapeDtypeStruct((T, n_heads * V_HEAD), BF16),
        grid=(batch, n_heads, nq),
        in_specs=[pl.BlockSpec((tb, QW), lambda b, h, i: (b * nq + i, h)),
                  pl.BlockSpec((S, QW), lambda b, h, i: (b, h)),
                  pl.BlockSpec((V_HEAD, S), lambda b, h, i: (b * n_heads + h, 0))],
        out_specs=pl.BlockSpec((tb, V_HEAD), lambda b, h, i: (b * nq + i, h)),
        compiler_params=_cparams("parallel", "parallel", "arbitrary"),
        name="mla_attention",
    )(q, k, vt)


def _rope_tables(positions):
    inv_freq = 1.0 / (ROPE_THETA ** (jnp.arange(0, QK_ROPE, 2, dtype=F32) / QK_ROPE))
    ang = positions.astype(F32).reshape(-1, 1) * inv_freq
    cos, sin = jnp.cos(ang), jnp.sin(ang)
    T = ang.shape[0]
    z64 = jnp.zeros((T, LANE - QK_ROPE), F32)
    ck = jnp.concatenate([cos, cos, z64], axis=1)
    sk = jnp.concatenate([-sin, sin, z64], axis=1)
    return ck * Q_SCALE, sk * Q_SCALE, ck, sk


def _mla_q_weights(w_uq, n_heads):
    R = w_uq.shape[0]
    w = w_uq.reshape(R, n_heads, QK_NOPE + QK_ROPE)
    half = QK_ROPE // 2
    w = jnp.concatenate([w, w[:, :, QK_NOPE + half:], w[:, :, QK_NOPE:QK_NOPE + half]], axis=2)
    return w.reshape(R, n_heads * 2 * LANE).astype(BF16)


N_LEVELS = 6
HGRN_BATCH = 4


def _gla_constants():
    C = CHUNK
    t = np.arange(C)[:, None]
    u = np.arange(C)[None, :]
    mk = np.zeros((N_LEVELS + 1, C, C), np.float32)
    for l in range(N_LEVELS):
        m = C >> (l + 1)
        mk[l] = (t // (2 * m) == u // (2 * m)) & ((t % (2 * m)) >= m) & ((u % (2 * m)) < m)
    mk[N_LEVELS] = t == u
    tri = (u <= t).astype(np.float32)
    to2d = lambda a: np.ascontiguousarray(a).reshape(-1, C)
    return tri, np.ascontiguousarray(tri[::-1, ::-1]), to2d(mk), to2d(mk[:, ::-1, ::-1])


def _level_exponent(b, g, m, rev, row):
    C = CHUNK
    if m >= 4:
        r = m if rev else m - 1
        b_mid = jnp.concatenate([jnp.broadcast_to(b[s + r:s + r + 1, :], (2 * m, b.shape[1]))
                                 for s in range(0, C, 2 * m)], axis=0)
        return -jnp.abs(b - b_mid)
    g_next = pltpu.roll(g, C - 1, 0)
    g_prev = pltpu.roll(g, 1, 0)
    if m == 2:
        p4 = row % 4
        if rev:
            return jnp.where(p4 == 3, g_prev, jnp.where(p4 == 2, 0.0, jnp.where(p4 == 1, g, g + g_next)))
        return jnp.where(p4 == 0, g_next, jnp.where(p4 == 1, 0.0, jnp.where(p4 == 2, g, g + g_prev)))
    return jnp.where((row % 2 == 1) != rev, g, 0.0)


def _gla_local(chunks):
    C = CHUNK
    row = lax.broadcasted_iota(jnp.int32, (C, 1), 0)
    gs = [c[3] * LOG2_E for c in chunks]
    parts = [_split3(g) for g in gs]
    bs = [_dot(c[4], p[0]) + _dot(c[4], p[1]) + _dot(c[4], p[2]) for c, p in zip(chunks, parts)]
    qbs = [c[0].astype(BF16) for c in chunks]
    kbs = [c[1].astype(BF16) for c in chunks]
    scores = [c[5][N_LEVELS * C:(N_LEVELS + 1) * C] * _dot_nt(qb, kb) for c, qb, kb in zip(chunks, qbs, kbs)]
    for l in range(N_LEVELS):
        pls = [jnp.exp2(_level_exponent(b, g, C >> (l + 1), c[6], row)).astype(BF16)
               for c, b, g in zip(chunks, bs, gs)]
        prods = [_dot_nt(qb * p, kb * p) for qb, kb, p in zip(qbs, kbs, pls)]
        scores = [s + c[5][l * C:(l + 1) * C] * a for s, c, a in zip(scores, chunks, prods)]
    out = []
    for c, b, s in zip(chunks, bs, scores):
        q, k, v, rev = c[0], c[1], c[2], c[6]
        b_tot = b[0:1] if rev else b[C - 1:C]
        vb = v.astype(BF16)
        out.append((_dot(s.astype(BF16), vb), (q * jnp.exp2(b)).astype(BF16),
                    _dot_tn(vb, (k * jnp.exp2(b_tot - b)).astype(BF16)), jnp.exp2(b_tot)))
    return out


def _hgrn_kernel(q_ref, zf_ref, zb_ref, i_ref, g_ref, lb_ref, ng_ref, trif_ref, trib_ref, mkf_ref, mkb_ref,
                 o_ref, of_ref, ob_ref, *, n_chunks):
    C = CHUNK
    lb = lb_ref[...]
    one_m_lb = 1.0 - lb
    trif, trib, mkf, mkb = trif_ref[...], trib_ref[...], mkf_ref[...], mkb_ref[...]

    def gate(zz):
        w = one_m_lb * _sigmoid(zz)
        return one_m_lb - w, jnp.log(jnp.maximum(lb + w, F_TINY))

    def load(ref, c):
        return ref[pl.ds(pl.multiple_of(c * C, C), C), :]

    def chunk_inputs(c, z_ref, tri, mk, rev):
        q = load(q_ref, c)
        k, g = gate(load(z_ref, c))
        return (q * _sigmoid(q), k, load(i_ref, c), g, tri, mk, rev)

    def body(it, carry):
        states = list(carry)
        ids = [[it * HGRN_BATCH + j for j in range(HGRN_BATCH)]]
        ids.append([n_chunks - 1 - c for c in ids[0]])
        loc = _gla_local([chunk_inputs(c, zf_ref, trif, mkf, False) for c in ids[0]]
                         + [chunk_inputs(c, zb_ref, trib, mkb, True) for c in ids[1]])
        for d, out_ref in enumerate((of_ref, ob_ref)):
            for j, c in enumerate(ids[d]):
                o_intra, q_dec, st_inc, p_tot = loc[d * HGRN_BATCH + j]
                out_ref[pl.ds(pl.multiple_of(c * C, C), C), :] = o_intra + _dot_nt(q_dec, states[d].astype(BF16))
                states[d] = states[d] * p_tot + st_inc
        return tuple(states)

    dv, dk = i_ref.shape[1], q_ref.shape[1]
    z = jnp.zeros((dv, dk), F32)
    lax.fori_loop(0, n_chunks // HGRN_BATCH, body, (z, z))

    o = of_ref[...] + ob_ref[...]
    ms = jnp.mean(o * o, axis=-1, keepdims=True)
    gg = g_ref[...]
    o_ref[...] = (o * lax.rsqrt(ms + NORM_EPS) * ng_ref[...] * (gg * _sigmoid(gg))).astype(o_ref.dtype)


def hgrn2_mixer(z, lb, norm_g, batch, *, col0=0):
    T = z.shape[0]
    W = lb.shape[-1]
    S = T // batch
    H = W // HEAD_B
    c0 = col0 // HEAD_B
    trif, trib, mkf, mkb = _gla_constants()
    part = lambda n: pl.BlockSpec((S, HEAD_B), functools.partial(lambda b, h, n: (b, c0 + n * H + h), n=n))
    const = lambda a: pl.BlockSpec(a.shape, lambda b, h: (0, 0))
    return pl.pallas_call(
        functools.partial(_hgrn_kernel, n_chunks=S // CHUNK),
        out_shape=jax.ShapeDtypeStruct((T, W), BF16),
        grid=(batch, H),
        in_specs=[part(0), part(1), part(2), part(3), part(4),
                  pl.BlockSpec((1, HEAD_B), lambda b, h: (0, h)),
                  pl.BlockSpec((1, HEAD_B), lambda b, h: (0, 0)),
                  const(trif), const(trib), const(mkf), const(mkb)],
        out_specs=pl.BlockSpec((S, HEAD_B), lambda b, h: (b, h)),
        scratch_shapes=[pltpu.VMEM((S, HEAD_B), F32), pltpu.VMEM((S, HEAD_B), F32)],
        compiler_params=_cparams("parallel", "parallel"),
        name="hgrn2_mixer",
    )(z, z, z, z, z, lb.reshape(1, W), norm_g.reshape(1, HEAD_B),
      jnp.asarray(trif, BF16), jnp.asarray(trib, BF16), jnp.asarray(mkf), jnp.asarray(mkb))


def _head_sum(x, bd):
    x1, x2, x3 = _split3(x)
    return _dot(x1, bd) + _dot(x2, bd) + _dot(x3, bd)


def _rwkv_prep_kernel(x_ref, xp_ref, xn_ref, mu_ref, w0_ref, w2_ref, a0_ref, a2_ref, g2_ref, kk_ref,
                      ka_ref, rk_ref, bd_ref,
                      r_ref, v_ref, kap_ref, kdf_ref, kdb_ref, alf_ref, alb_ref, lwf_ref, lwb_ref,
                      g_ref, bon_ref, *, tm, seq, width):
    W = width
    i = pl.program_id(0)
    x = x_ref[...]
    at_start = (i * tm) % seq == 0
    at_end = ((i + 1) * tm) % seq == 0
    prev_row = jnp.where(at_start, 0.0, xp_ref[7:8, :])
    next_row = jnp.where(at_end, 0.0, xn_ref[0:1, :])
    row = lax.broadcasted_iota(jnp.int32, (tm, 1), 0)
    x_prev = jnp.where(row == 0, prev_row, pltpu.roll(x, 1, 0))
    x_next = jnp.where(row == tm - 1, next_row, pltpu.roll(x, tm - 1, 0))
    u = x + mu_ref[0:1, :] * (x_prev - x) + mu_ref[1:2, :] * (x_next - x)

    r, k, v = u[:, :W], u[:, W:2 * W], u[:, 2 * W:3 * W]
    o = 3 * W
    wd = jnp.tanh(u[:, o:o + 2 * LORA_DECAY]).astype(BF16)
    o += 2 * LORA_DECAY
    ad = u[:, o:o + 2 * LORA_AAA].astype(BF16)
    o += 2 * LORA_AAA
    gd = _sigmoid(u[:, o:]).astype(BF16)

    bd = bd_ref[...]
    kkr = k * kk_ref[...]
    sq = kkr * kkr
    rks = []
    a_dir = []
    for n in range(2):
        w_raw = w0_ref[n:n + 1, :] + _dot(wd[:, n * LORA_DECAY:(n + 1) * LORA_DECAY], w2_ref[n])
        lw = -DECAY_SCALE * _sigmoid(w_raw)
        a = _sigmoid(a0_ref[n:n + 1, :] + _dot(ad[:, n * LORA_AAA:(n + 1) * LORA_AAA], a2_ref[n]))
        a_dir.append(a)
        (lwf_ref, lwb_ref)[n][...] = lw
    kd = [k * (1.0 + (a - 1.0) * ka_ref[...]) for a in a_dir]
    kdf_ref[...] = kd[0].astype(kdf_ref.dtype)
    kdb_ref[...] = kd[1].astype(kdb_ref.dtype)
    rkk = r * (kd[0] + kd[1]) * rk_ref[...]
    for c in range(W // LANE):
        sl = slice(c * LANE, (c + 1) * LANE)
        nrm = jnp.maximum(jnp.sqrt(_head_sum(sq[:, sl], bd)), 1e-12)
        kap = kkr[:, sl] / nrm
        kap_ref[:, sl] = kap.astype(kap_ref.dtype)
        alf_ref[:, sl] = (kap * a_dir[0][:, sl]).astype(alf_ref.dtype)
        alb_ref[:, sl] = (kap * a_dir[1][:, sl]).astype(alb_ref.dtype)
        bon_ref[:, sl] = _head_sum(rkk[:, sl], bd) * v[:, sl]
    r_ref[...] = r.astype(r_ref.dtype)
    v_ref[...] = v.astype(v_ref.dtype)
    g_ref[...] = _dot(gd, g2_ref[...]).astype(g_ref.dtype)


def rwkv_prep(z, mu, w0, w2, a0, a2, g2, k_k, k_a, r_k, batch, *, tm=256):
    T, ZW = z.shape
    W = w0.shape[-1]
    S = T // batch
    nb8 = tm // 8
    used = 3 * W + 2 * LORA_DECAY + 2 * LORA_AAA + LORA_GATE
    mu_p = jnp.pad(mu, ((0, 0), (0, ZW - used)))
    g2_p = jnp.pad(g2, ((0, ZW - used), (0, 0))).astype(BF16)
    bd = np.kron(np.eye(LANE // HEAD_A, dtype=np.float32), np.ones((HEAD_A, HEAD_A), np.float32))
    full = lambda a: pl.BlockSpec(a.shape, lambda i: (0,) * a.ndim)
    vec = lambda a: a.reshape(1, W)
    args = [z, z, z, mu_p, w0, w2.astype(BF16), a0, a2.astype(BF16), g2_p, vec(k_k), vec(k_a),
            vec(r_k), jnp.asarray(bd, BF16)]
    in_specs = [pl.BlockSpec((tm, ZW), lambda i: (i, 0)),
                pl.BlockSpec((8, ZW), lambda i: (jnp.maximum(i * nb8 - 1, 0), 0)),
                pl.BlockSpec((8, ZW), lambda i: (jnp.minimum((i + 1) * nb8, T // 8 - 1), 0))]
    in_specs += [full(a) for a in args[3:]]
    b16 = jax.ShapeDtypeStruct((T, W), BF16)
    f32 = jax.ShapeDtypeStruct((T, W), F32)
    return pl.pallas_call(
        functools.partial(_rwkv_prep_kernel, tm=tm, seq=S, width=W),
        out_shape=(b16,) * 7 + (f32, f32, b16, f32),
        grid=(T // tm,),
        in_specs=in_specs,
        out_specs=(pl.BlockSpec((tm, W), lambda i: (i, 0)),) * 11,
        compiler_params=_cparams("parallel"),
        name="rwkv_prep",
    )(*args)


RWKV_GROUP = 256


def _rwkv_constants():
    G, C = RWKV_GROUP, CHUNK
    t = np.arange(G)[:, None]
    u = np.arange(G)[None, :]
    dd = (((t // C) == (u // C)) & (u <= t)).astype(np.float32)
    return dd, np.ascontiguousarray(dd[::-1, ::-1])


def _bdot(a, b):
    return _dot(a.astype(BF16), b.astype(BF16))


def _unit_tri_inverses(ms, eye, diag_blocks):
    each = lambda f, *ls: [f(*a) for a in zip(*ls)]
    mds = [jnp.where(diag_blocks, m, 0.0) for m in ms]
    mos = each(lambda m, md: m - md, ms, mds)
    xs = [eye - md for md in mds]
    ps = each(_bdot, mds, mds)
    for _ in range(2):
        xs = each(lambda x, p: x + _bdot(x, p), xs, ps)
        ps = each(_bdot, ps, ps)
    tds = each(lambda x, p: x + _bdot(x, p), xs, ps)
    gs = each(_bdot, tds, mos)
    g2s = each(_bdot, gs, gs)
    ys = each(lambda g, g2: (eye - g) + _bdot(eye - g, g2), gs, g2s)
    return each(_bdot, ys, tds)


def _rwkv_groups(r, v, kap, dirs, eye, diag_blocks, head_masks):
    G = r.shape[0]
    pre = []
    for kd, al, lw, dd, rev, causal_incl, causal_strict in dirs:
        l1, l2, l3 = _split3(lw)
        b_in = _dot(dd, l1) + _dot(dd, l2) + _dot(dd, l3)
        r_tot = 0 if rev else CHUNK - 1
        b_tot = jnp.concatenate([jnp.broadcast_to(b_in[s + r_tot:s + r_tot + 1], (CHUNK, b_in.shape[1]))
                                 for s in range(0, G, CHUNK)], axis=0)
        b_ex, b_out = b_in - lw, b_tot - b_in
        p_neg = jnp.exp(-b_in)
        p_out = jnp.exp(b_out)
        pre.append(dict(kap_h=kap * jnp.exp(b_ex), r_h=r * jnp.exp(b_in), al_n=(al * p_neg).astype(BF16),
                        kd_n=(kd * p_neg).astype(BF16), kout=kd * p_out, aout=al * p_out, p_tot=jnp.exp(b_tot),
                        incl=causal_incl, strict=causal_strict))
    chains = [(p, hm) for p in pre for hm in head_masks]
    kap_hb = [jnp.where(hm, p["kap_h"], 0.0).astype(BF16) for p, hm in chains]
    r_hb = [jnp.where(hm, p["r_h"], 0.0).astype(BF16) for p, hm in chains]
    v_h = [jnp.where(hm, v, 0.0).astype(BF16) for _, hm in chains]
    ms = [jnp.where(p["strict"], _dot_nt(k_, p["al_n"]), 0.0) for (p, _), k_ in zip(chains, kap_hb)]
    ns = [jnp.where(p["strict"], _dot_nt(k_, p["kd_n"]), 0.0) for (p, _), k_ in zip(chains, kap_hb)]
    ras = [jnp.where(p["incl"], _dot_nt(r_, p["al_n"]), 0.0) for (p, _), r_ in zip(chains, r_hb)]
    rks = [jnp.where(p["incl"], _dot_nt(r_, p["kd_n"]), 0.0) for (p, _), r_ in zip(chains, r_hb)]
    nvs = [_bdot(n, vh).astype(BF16) for n, vh in zip(ns, v_h)]
    y0s = [_bdot(rk, vh) for rk, vh in zip(rks, v_h)]
    racs = [sum(ra[:, s:s + CHUNK] for s in range(0, G, CHUNK)) for ra in ras]
    tinvs = [t.astype(BF16) for t in _unit_tri_inverses(ms, eye, diag_blocks)]
    kw_uv = [_dot(t, jnp.concatenate([k_, nv], axis=1)) for t, k_, nv in zip(tinvs, kap_hb, nvs)]
    out = []
    for d, p in enumerate(pre):
        a, b = kw_uv[2 * d], kw_uv[2 * d + 1]
        out.append((a[:, :LANE] + b[:, :LANE], p["r_h"], jnp.concatenate(racs[2 * d:2 * d + 2], axis=1),
                    p["kout"], p["aout"], a[:, LANE:] + b[:, LANE:], y0s[2 * d] + y0s[2 * d + 1], p["p_tot"]))
    return out


N_LOCAL = 8


def _rwkv_local_kernel(r_ref, v_ref, kap_ref, kdf_ref, kdb_ref, alf_ref, alb_ref, lwf_ref, lwb_ref,
                       ddf_ref, ddb_ref, *out_refs):
    G, C = RWKV_GROUP, CHUNK
    ti = lax.broadcasted_iota(jnp.int32, (G, G), 0)
    si = lax.broadcasted_iota(jnp.int32, (G, G), 1)
    same_chunk = (ti // C) == (si // C)
    eye = (ti == si).astype(F32)
    diag_blocks = (ti // 16) == (si // 16)
    lane = lax.broadcasted_iota(jnp.int32, (1, LANE), 1)
    head_masks = [lane < HEAD_A, lane >= HEAD_A]
    r, v, kap = r_ref[...], v_ref[...], kap_ref[...]
    dirs = []
    for d, (kd_ref, al_ref, lw_ref, dd_ref) in enumerate(((kdf_ref, alf_ref, lwf_ref, ddf_ref),
                                                          (kdb_ref, alb_ref, lwb_ref, ddb_ref))):
        incl = same_chunk & ((si <= ti) if d == 0 else (si >= ti))
        strict = same_chunk & ((si < ti) if d == 0 else (si > ti))
        dirs.append((kd_ref[...], al_ref[...], lw_ref[...], dd_ref[...], d == 1, incl, strict))
    for d, res in enumerate(_rwkv_groups(r, v, kap, dirs, eye, diag_blocks, head_masks)):
        outs = out_refs[d * N_LOCAL:(d + 1) * N_LOCAL]
        for o_ref, val in zip(outs[:-1], res[:-1]):
            o_ref[...] = val.astype(o_ref.dtype)
        p_tot = res[-1]
        outs[-1][...] = jnp.concatenate([p_tot[c * C:c * C + 8] for c in range(G // C)], axis=0)


def rwkv_local(r, v, kap, kd_f, kd_b, al_f, al_b, lw_f, lw_b):
    T, W = r.shape
    G = RWKV_GROUP
    ddf, ddb = _rwkv_constants()
    blk = pl.BlockSpec((G, LANE), lambda i, h: (i, h))
    pblk = pl.BlockSpec((G // 8, LANE), lambda i, h: (i, h))
    const = lambda a: pl.BlockSpec(a.shape, lambda i, h: (0, 0))
    b16 = jax.ShapeDtypeStruct((T, W), BF16)
    f32 = jax.ShapeDtypeStruct((T, W), F32)
    per_dir = (b16, b16, b16, b16, b16, f32, f32, jax.ShapeDtypeStruct((T // 8, W), F32))
    outs = pl.pallas_call(
        _rwkv_local_kernel,
        out_shape=per_dir * 2,
        grid=(T // G, W // LANE),
        in_specs=[blk] * 9 + [const(ddf), const(ddb)],
        out_specs=((blk,) * 7 + (pblk,)) * 2,
        compiler_params=_cparams("parallel", "parallel"),
        name="rwkv_local",
    )(r, v, kap, kd_f, kd_b, al_f, al_b, lw_f, lw_b,
      jnp.asarray(ddf, BF16), jnp.asarray(ddb, BF16))
    return outs[:N_LOCAL], outs[N_LOCAL:]


SCAN_PAIRS = 4


def _rwkv_scan_chunks(chains, head_masks, same_head):
    C = CHUNK
    m1s = [_dot_nt(jnp.concatenate([c[0], c[1]], axis=0), c[9].astype(BF16)) for c in chains]
    us = [m1[:C] + c[5] for m1, c in zip(m1s, chains)]
    incs = [_dot_tn(jnp.concatenate([c[7].astype(BF16), (-u).astype(BF16)], axis=0),
                    jnp.concatenate([c[3], c[4]], axis=0)) for u, c in zip(us, chains)]
    sts = [jnp.where(same_head, c[9] * c[8] + inc, 0.0) for c, inc in zip(chains, incs)]
    u_cats = [jnp.concatenate([jnp.where(hm, u, 0.0).astype(BF16) for hm in head_masks], axis=0) for u in us]
    ys = [m1[C:] + c[6] - _dot(c[2], u_cat) for m1, c, u_cat in zip(m1s, chains, u_cats)]
    return list(zip(ys, sts))


def _rwkv_scan_kernel(*refs, n_chunks, n_pairs):
    C = CHUNK
    f_refs, vf_ref = refs[:N_LOCAL], refs[N_LOCAL]
    b_refs, vb_ref = refs[N_LOCAL + 1:2 * N_LOCAL + 1], refs[2 * N_LOCAL + 1]
    yf_ref, yb_ref, stf_ref, stb_ref = refs[2 * N_LOCAL + 2:]

    @pl.when(pl.program_id(2) == 0)
    def _():
        stf_ref[...] = jnp.zeros_like(stf_ref)
        stb_ref[...] = jnp.zeros_like(stb_ref)

    lane = lax.broadcasted_iota(jnp.int32, (1, LANE), 1)
    head_masks = [lane < HEAD_A, lane >= HEAD_A]
    vi = lax.broadcasted_iota(jnp.int32, (LANE, LANE), 0)
    ki = lax.broadcasted_iota(jnp.int32, (LANE, LANE), 1)
    same_head = (vi // HEAD_A) == (ki // HEAD_A)

    def rows(ref, c, n):
        return ref[pl.ds(pl.multiple_of(c * n, n), n), :]

    def body(it, carry):
        states = [list(carry[0]), list(carry[1])]
        chains = []
        for d, (d_refs, v_ref) in enumerate(((f_refs, vf_ref), (b_refs, vb_ref))):
            c = it if d == 0 else n_chunks - 1 - it
            vals = [rows(ref, c, C) for ref in d_refs[:-1]] + [rows(v_ref, c, C), rows(d_refs[-1], c, 8)[0:1]]
            for p in range(n_pairs):
                chains.append(tuple(a[:, p * LANE:(p + 1) * LANE] for a in vals) + (states[d][p],))
        res = _rwkv_scan_chunks(chains, head_masks, same_head)
        for d, y_ref in enumerate((yf_ref, yb_ref)):
            c = it if d == 0 else n_chunks - 1 - it
            y_ref[pl.ds(pl.multiple_of(c * C, C), C), :] = jnp.concatenate(
                [res[d * n_pairs + p][0] for p in range(n_pairs)], axis=1)
            states[d] = [res[d * n_pairs + p][1] for p in range(n_pairs)]
        return tuple(states[0]), tuple(states[1])

    init = tuple(tuple(ref[p] for p in range(n_pairs)) for ref in (stf_ref, stb_ref))
    st_f, st_b = lax.fori_loop(0, n_chunks, body, init)
    for p in range(n_pairs):
        stf_ref[p] = st_f[p]
        stb_ref[p] = st_b[p]


def rwkv_scan(loc_f, loc_b, v, batch, *, ts=512):
    T, W = v.shape
    S = T // batch
    ts = min(ts, S)
    ns = S // ts
    bw = LANE * SCAN_PAIRS
    fwd = lambda rows: pl.BlockSpec((rows, bw), lambda b, h, s: (b * ns + s, h))
    bwd = lambda rows: pl.BlockSpec((rows, bw), lambda b, h, s: (b * ns + ns - 1 - s, h))
    specs = lambda mk: [mk(ts)] * (N_LOCAL - 1) + [mk(ts // 8), mk(ts)]
    out = jax.ShapeDtypeStruct((T, W), F32)
    state = pltpu.VMEM((SCAN_PAIRS, LANE, LANE), F32)
    return pl.pallas_call(
        functools.partial(_rwkv_scan_kernel, n_chunks=ts // CHUNK, n_pairs=SCAN_PAIRS),
        out_shape=(out, out),
        grid=(batch, W // bw, ns),
        in_specs=specs(fwd) + specs(bwd),
        out_specs=(fwd(ts), bwd(ts)),
        scratch_shapes=[state, state],
        compiler_params=_cparams("parallel", "parallel", "arbitrary"),
        name="rwkv_scan",
    )(*loc_f, v, *loc_b, v)


def _rwkv_out_kernel(yf_ref, yb_ref, bon_ref, g_ref, gw_ref, gb_ref, bd_ref, o_ref, *, width):
    bd = bd_ref[...]
    inv_n = 1.0 / HEAD_A
    for c in range(width // LANE):
        sl = slice(c * LANE, (c + 1) * LANE)
        y = yf_ref[:, sl] + yb_ref[:, sl]
        mean = _head_sum(y, bd) * inv_n
        d = y - mean
        var = _head_sum(d * d, bd) * inv_n
        yn = d * lax.rsqrt(var + GN_EPS) * gw_ref[:, sl] + gb_ref[:, sl] + bon_ref[:, sl]
        o_ref[:, sl] = (yn * g_ref[:, sl]).astype(o_ref.dtype)


def rwkv_out(y_f, y_b, bonus, g, gn_w, gn_b, *, tm=512):
    T, W = y_f.shape
    bd = np.kron(np.eye(LANE // HEAD_A, dtype=np.float32), np.ones((HEAD_A, HEAD_A), np.float32))
    row = pl.BlockSpec((tm, W), lambda i: (i, 0))
    vec = pl.BlockSpec((1, W), lambda i: (0, 0))
    return pl.pallas_call(
        functools.partial(_rwkv_out_kernel, width=W),
        out_shape=jax.ShapeDtypeStruct((T, W), BF16),
        grid=(T // tm,),
        in_specs=[row, row, row, row, vec, vec, pl.BlockSpec((LANE, LANE), lambda i: (0, 0))],
        out_specs=row,
        compiler_params=_cparams("parallel"),
        name="rwkv_out",
    )(y_f, y_b, bonus, g, gn_w.reshape(1, W), gn_b.reshape(1, W), jnp.asarray(bd, BF16))


def rwkv7_mixer(z, mu, w0, w2, a0, a2, g2, k_k, k_a, r_k, gn_w, gn_b, batch):
    r, v, kap, kd_f, kd_b, al_f, al_b, lw_f, lw_b, g, bonus = rwkv_prep(
        z, mu, w0, w2, a0, a2, g2, k_k, k_a, r_k.reshape(-1), batch)
    loc_f, loc_b = rwkv_local(r, v, kap, kd_f, kd_b, al_f, al_b, lw_f, lw_b)
    y_f, y_b = rwkv_scan(loc_f, loc_b, v, batch)
    return rwkv_out(y_f, y_b, bonus, g, gn_w, gn_b)


def _relu2(acc):
    r = jnp.maximum(acc, 0.0)
    return r * r


def _add(acc, res):
    return acc + res


def _pad_cols(w, mult):
    return jnp.pad(w, ((0, 0), (0, (-w.shape[1]) % mult)))


def kernel(x, p, positions, ln1_g, w_in, rwkv_mu, rwkv_w0, rwkv_w2, rwkv_a0, rwkv_a2, rwkv_g2, rwkv_kk, rwkv_ka, rwkv_rk, rwkv_gn_w, rwkv_gn_b, hgrn_lb, hgrn_norm_g, mla_q_norm_g, mla_kv_norm_g, mla_w_uq, mla_w_ukv, w_branch, w_o, ln2_g, w_mlp1, w_mlp2, w_pe, w_pg, final_g):
    Bn, S, D = x.shape
    L = w_in.shape[0]
    T = Bn * S
    W = rwkv_w0.shape[-1]
    q_lora, kv_lora = mla_q_norm_g.shape[-1], mla_kv_norm_g.shape[-1]
    n_heads_c = mla_w_ukv.shape[-1] // (QK_NOPE + V_HEAD)
    rwkv_w = 3 * W + 2 * LORA_DECAY + 2 * LORA_AAA + LORA_GATE
    hgrn_w = 5 * W
    o_hgrn = rwkv_w
    o_cq = o_hgrn + hgrn_w
    o_ckv = o_cq + q_lora
    o_kr = o_ckv + kv_lora
    o_gate = o_kr + QK_ROPE
    half = QK_ROPE // 2

    lb_w = jax.nn.softmax(hgrn_lb.astype(F32), axis=0)
    lower_bounds = jnp.cumsum(lb_w, axis=0) - lb_w[0]
    cq_tab, sq_tab, ck_tab, sk_tab = _rope_tables(positions)

    h = x.reshape(T, D)
    for l in range(L):
        wl = w_in[l]
        w_rwkv = _pad_cols(wl[:, :rwkv_w], 512).astype(BF16)
        w_hgrn = wl[:, o_hgrn:o_cq].astype(BF16)
        w_cq = wl[:, o_cq:o_ckv].astype(BF16)
        w_kr = wl[:, o_kr:o_gate]
        w_kv = jnp.concatenate(
            [wl[:, o_ckv:o_kr], _pad_cols(w_kr, LANE),
             _pad_cols(jnp.concatenate([w_kr[:, half:], w_kr[:, :half]], axis=1), LANE)], axis=1).astype(BF16)
        w_gate = wl[:, o_gate:].astype(BF16)

        hn = rmsnorm(h, ln1_g[l], BF16)
        z_rwkv = matmul(hn, w_rwkv, out_dtype=F32, tm=1024, tn=512, name="in_rwkv")
        z_hgrn = matmul(hn, w_hgrn, out_dtype=F32, tm=1024, tn=512, name="in_hgrn")
        z_cq = matmul(hn, w_cq, out_dtype=F32, tm=1024, tn=w_cq.shape[1], name="in_cq")
        z_kv = matmul(hn, w_kv, out_dtype=F32, tm=1024, tn=w_kv.shape[1], name="in_kv")
        z_gate = matmul(hn, w_gate, out_dtype=F32, tm=1024, tn=512, name="in_gate")

        y_a = rwkv7_mixer(z_rwkv, rwkv_mu[l], rwkv_w0[l], rwkv_w2[l], rwkv_a0[l], rwkv_a2[l], rwkv_g2[l],
                          rwkv_kk[l], rwkv_ka[l], rwkv_rk[l], rwkv_gn_w[l], rwkv_gn_b[l], Bn)
        y_b = hgrn2_mixer(z_hgrn, lower_bounds[l], hgrn_norm_g[l], Bn)
        q = mla_q_proj(z_cq, mla_q_norm_g[l], _mla_q_weights(mla_w_uq[l], n_heads_c), cq_tab, sq_tab,
                       n_heads_c, Q_SCALE)
        k, vt = mla_kv_proj(z_kv, mla_kv_norm_g[l], mla_w_ukv[l].astype(BF16), ck_tab, sk_tab, n_heads_c, Bn)
        y_c = mla_attention(q, k, vt, Bn, n_heads_c)

        mixed = branch_mix(y_a, y_b, y_c, w_branch[l].astype(BF16), z_gate)
        h = matmul(mixed, w_o[l].astype(BF16), out_dtype=F32, tm=1024, tn=1024, epilogue=_add,
                   extras=(h,), alias_extra=0, name="w_o")
        hn = rmsnorm(h, ln2_g[l], BF16)
        hid = matmul(hn, w_mlp1[l].astype(BF16), out_dtype=BF16, tm=1024, tn=1024, epilogue=_relu2,
                     name="mlp1")
        h = matmul(hid, w_mlp2[l].astype(BF16), out_dtype=F32, tm=1024, tn=1024, tk=2048, epilogue=_add,
                   extras=(h,), alias_extra=0, name="mlp2")
        h = ple_update(h, w_pg[l].astype(BF16), p[l].reshape(T, -1), w_pe[l].astype(BF16))
    return rmsnorm(h, final_g, F32).reshape(Bn, S, D)
```

```python
import functools

import numpy as np
import jax
import jax.numpy as jnp
from jax import lax
from jax.experimental import pallas as pl
from jax.experimental.pallas import tpu as pltpu

F32 = jnp.float32
BF16 = jnp.bfloat16

LANE = 128
VMEM_LIMIT = 48 * 2**20

HEAD_A = 64
LORA_DECAY = 64
LORA_AAA = 64
LORA_GATE = 160
DECAY_SCALE = 0.606531
GN_EPS = 64e-5
HEAD_B = 128
F_TINY = 1e-30
QK_NOPE = 128
QK_ROPE = 64
V_HEAD = 128
ROPE_THETA = 10000.0
NORM_EPS = 1e-6
CHUNK = 64
LOG2_E = 1.4426950408889634
Q_SCALE = (QK_NOPE + QK_ROPE) ** -0.5 * LOG2_E


def _cparams(*sem, flags=None):
    return pltpu.CompilerParams(dimension_semantics=sem, vmem_limit_bytes=VMEM_LIMIT, flags=flags)


def _sigmoid(x):
    return 1.0 / (1.0 + jnp.exp(-x))


def _dot(a, b):
    return jnp.dot(a, b, preferred_element_type=F32)


def _dot_nt(a, b):
    return lax.dot_general(a, b, (((1,), (1,)), ((), ())), preferred_element_type=F32)


def _dot_tn(a, b):
    return lax.dot_general(a, b, (((0,), (0,)), ((), ())), preferred_element_type=F32)


def _split3(x):
    x1 = x.astype(BF16)
    r1 = x - x1.astype(F32)
    x2 = r1.astype(BF16)
    x3 = (r1 - x2.astype(F32)).astype(BF16)
    return x1, x2, x3


def _rmsnorm_kernel(x_ref, g_ref, o_ref):
    x = x_ref[...]
    ms = jnp.mean(x * x, axis=-1, keepdims=True)
    o_ref[...] = (x * lax.rsqrt(ms + NORM_EPS) * g_ref[...]).astype(o_ref.dtype)


def rmsnorm(x, g, out_dtype, tm=512):
    T, D = x.shape
    return pl.pallas_call(
        _rmsnorm_kernel,
        out_shape=jax.ShapeDtypeStruct((T, D), out_dtype),
        grid=(T // tm,),
        in_specs=[pl.BlockSpec((tm, D), lambda i: (i, 0)),
                  pl.BlockSpec((1, D), lambda i: (0, 0))],
        out_specs=pl.BlockSpec((tm, D), lambda i: (i, 0)),
        compiler_params=_cparams("parallel"),
        name="rmsnorm",
    )(x, g.reshape(1, D))


def _mm_kernel(a_ref, w_ref, *rest, nk, epilogue, n_extra):
    extras = rest[:n_extra]
    o_ref = rest[n_extra]

    def finish(acc):
        o_ref[...] = epilogue(acc, *[e[...] for e in extras]).astype(o_ref.dtype)

    if nk == 1:
        finish(_dot(a_ref[...], w_ref[...]))
    else:
        acc_ref = rest[n_extra + 1]
        k = pl.program_id(2)

        @pl.when(k == 0)
        def _():
            acc_ref[...] = jnp.zeros_like(acc_ref)

        acc_ref[...] += _dot(a_ref[...], w_ref[...])

        @pl.when(k == nk - 1)
        def _():
            finish(acc_ref[...])


def matmul(a, w, *, out_dtype, tm, tn, tk=None, epilogue=None, extras=(), alias_extra=None,
           name="matmul"):
    M, K = a.shape
    N = w.shape[1]
    tk = K if tk is None else tk
    nk = K // tk
    epilogue = epilogue or (lambda acc: acc)
    kern = functools.partial(_mm_kernel, nk=nk, epilogue=epilogue, n_extra=len(extras))
    in_specs = [pl.BlockSpec((tm, tk), lambda i, j, k: (i, k)),
                pl.BlockSpec((tk, tn), lambda i, j, k: (k, j))]
    in_specs += [pl.BlockSpec((tm, tn), lambda i, j, k: (i, j)) for _ in extras]
    aliases = {} if alias_extra is None else {2 + alias_extra: 0}
    return pl.pallas_call(
        kern,
        out_shape=jax.ShapeDtypeStruct((M, N), out_dtype),
        grid=(M // tm, N // tn, nk),
        in_specs=in_specs,
        out_specs=pl.BlockSpec((tm, tn), lambda i, j, k: (i, j)),
        scratch_shapes=[pltpu.VMEM((tm, tn), F32)] if nk > 1 else [],
        input_output_aliases=aliases,
        compiler_params=_cparams("parallel", "parallel", "arbitrary"),
        name=name,
    )(a, w, *extras)


def _branch_kernel(ya_ref, yb_ref, yc_ref, p_ref, ga_ref, gb_ref, gc_ref, o_ref):
    acc = _sigmoid(ga_ref[...].astype(F32)) * _dot(ya_ref[...], p_ref[0])
    acc += _sigmoid(gb_ref[...].astype(F32)) * _dot(yb_ref[...], p_ref[1])
    acc += _sigmoid(gc_ref[...].astype(F32)) * _dot(yc_ref[...], p_ref[2])
    o_ref[...] = acc.astype(o_ref.dtype)


def branch_mix(ya, yb, yc, p, zg, *, tm=1024, tn=512):
    T, W = ya.shape
    D = p.shape[2]
    nj = D // tn
    y_spec = pl.BlockSpec((tm, W), lambda i, j: (i, 0))
    g_specs = [pl.BlockSpec((tm, tn), functools.partial(lambda i, j, n: (i, n * nj + j), n=n))
               for n in range(3)]
    return pl.pallas_call(
        _branch_kernel,
        out_shape=jax.ShapeDtypeStruct((T, D), BF16),
        grid=(T // tm, nj),
        in_specs=[y_spec, y_spec, y_spec,
                  pl.BlockSpec((3, W, tn), lambda i, j: (0, 0, j))] + g_specs,
        out_specs=pl.BlockSpec((tm, tn), lambda i, j: (i, j)),
        compiler_params=_cparams("parallel", "parallel"),
        name="branch_mix",
    )(ya, yb, yc, p, zg, zg, zg)


def _rms_rows(h, g):
    return h * lax.rsqrt(jnp.mean(h * h, axis=-1, keepdims=True) + NORM_EPS) * g


def _wo_ln_kernel(a_ref, w_ref, h_ref, g_ref, h_out_ref, hn_ref):
    h = h_ref[...] + _dot(a_ref[...], w_ref[...])
    h_out_ref[...] = h
    hn_ref[...] = _rms_rows(h, g_ref[...]).astype(hn_ref.dtype)


def out_proj_norm(a, w, h, g, *, tm=512):
    T, D = h.shape
    K = a.shape[1]
    row = lambda width: pl.BlockSpec((tm, width), lambda i: (i, 0))
    return pl.pallas_call(
        _wo_ln_kernel,
        out_shape=(jax.ShapeDtypeStruct((T, D), F32), jax.ShapeDtypeStruct((T, D), BF16)),
        grid=(T // tm,),
        in_specs=[row(K), pl.BlockSpec((K, D), lambda i: (0, 0)), row(D), pl.BlockSpec((1, D), lambda i: (0, 0))],
        out_specs=(row(D), row(D)),
        input_output_aliases={2: 0},
        compiler_params=_cparams("parallel"),
        name="out_proj_norm",
    )(a, w, h, g.reshape(1, D))


def _ple_ln_kernel(h_ref, wpg_ref, p_ref, wpe_ref, g_ref, h_out_ref, hn_ref):
    h = h_ref[...]
    gate = _sigmoid(_dot(h.astype(BF16), wpg_ref[...]))
    h = h + gate * _dot(p_ref[...].astype(BF16), wpe_ref[...])
    h_out_ref[...] = h
    hn_ref[...] = _rms_rows(h, g_ref[...]).astype(hn_ref.dtype)


def ple_update_norm(h, wpg, p, wpe, g, norm_dtype, *, tm=512):
    T, D = h.shape
    E = p.shape[1]
    row = lambda width: pl.BlockSpec((tm, width), lambda i: (i, 0))
    full = lambda r, c: pl.BlockSpec((r, c), lambda i: (0, 0))
    return pl.pallas_call(
        _ple_ln_kernel,
        out_shape=(jax.ShapeDtypeStruct((T, D), F32), jax.ShapeDtypeStruct((T, D), norm_dtype)),
        grid=(T // tm,),
        in_specs=[row(D), full(D, D), row(E), full(E, D), full(1, D)],
        out_specs=(row(D), row(D)),
        input_output_aliases={0: 0},
        compiler_params=_cparams("parallel"),
        name="ple_update_norm",
    )(h, wpg, p, wpe, g.reshape(1, D))


def _mla_q_kernel(cq_ref, g_ref, w_ref, c_ref, s_ref, o_ref, xn_ref, *, scale):
    @pl.when(pl.program_id(1) == 0)
    def _():
        x = cq_ref[...]
        ms = jnp.mean(x * x, axis=-1, keepdims=True)
        xn_ref[...] = (x * lax.rsqrt(ms + NORM_EPS) * g_ref[...]).astype(BF16)

    x = _dot(xn_ref[...], w_ref[...])
    hi = x[:, QK_NOPE:]
    o_ref[:, :QK_NOPE] = (x[:, :QK_NOPE] * scale).astype(BF16)
    o_ref[:, QK_NOPE:] = (hi * c_ref[...] + pltpu.roll(hi, LANE // 2, 1) * s_ref[...]).astype(BF16)


def mla_q_proj(cq, g, w, ctab, stab, n_heads, scale, *, tm=512):
    T, R = cq.shape
    HW = w.shape[1] // n_heads
    return pl.pallas_call(
        functools.partial(_mla_q_kernel, scale=scale),
        out_shape=jax.ShapeDtypeStruct((T, n_heads * HW), BF16),
        grid=(T // tm, n_heads),
        in_specs=[pl.BlockSpec((tm, R), lambda i, h: (i, 0)),
                  pl.BlockSpec((1, R), lambda i, h: (0, 0)),
                  pl.BlockSpec((R, HW), lambda i, h: (0, h)),
                  pl.BlockSpec((tm, LANE), lambda i, h: (i, 0)),
                  pl.BlockSpec((tm, LANE), lambda i, h: (i, 0))],
        out_specs=pl.BlockSpec((tm, HW), lambda i, h: (i, h)),
        scratch_shapes=[pltpu.VMEM((tm, R), BF16)],
        compiler_params=_cparams("parallel", "arbitrary"),
        name="mla_q_proj",
    )(cq, g.reshape(1, R), w, ctab, stab)


def _mla_kv_kernel(z_ref, g_ref, w_ref, c_ref, s_ref, k_ref, v_ref, xn_ref, *, kv_lora):
    @pl.when(pl.program_id(1) == 0)
    def _():
        x = z_ref[:, :kv_lora]
        ms = jnp.mean(x * x, axis=-1, keepdims=True)
        xn_ref[...] = (x * lax.rsqrt(ms + NORM_EPS) * g_ref[...]).astype(BF16)

    kv = _dot(xn_ref[...], w_ref[...])
    kr = z_ref[:, kv_lora:kv_lora + LANE] * c_ref[...] + z_ref[:, kv_lora + LANE:] * s_ref[...]
    k_ref[:, :QK_NOPE] = kv[:, :QK_NOPE].astype(BF16)
    k_ref[:, QK_NOPE:] = kr.astype(BF16)
    v_ref[...] = kv[:, QK_NOPE:].T.astype(BF16)


def mla_kv_proj(zkv, g, w, ctab, stab, n_heads, batch, *, tm=512):
    T, ZW = zkv.shape
    R = ZW - 2 * LANE
    S = T // batch
    tm = min(tm, S)
    nsb = S // tm
    return pl.pallas_call(
        functools.partial(_mla_kv_kernel, kv_lora=R),
        out_shape=(jax.ShapeDtypeStruct((T, n_heads * 2 * LANE), BF16),
                   jax.ShapeDtypeStruct((batch * n_heads * V_HEAD, S), BF16)),
        grid=(T // tm, n_heads),
        in_specs=[pl.BlockSpec((tm, ZW), lambda i, h: (i, 0)),
                  pl.BlockSpec((1, R), lambda i, h: (0, 0)),
                  pl.BlockSpec((R, QK_NOPE + V_HEAD), lambda i, h: (0, h)),
                  pl.BlockSpec((tm, LANE), lambda i, h: (i, 0)),
                  pl.BlockSpec((tm, LANE), lambda i, h: (i, 0))],
        out_specs=(pl.BlockSpec((tm, 2 * LANE), lambda i, h: (i, h)),
                   pl.BlockSpec((V_HEAD, tm), lambda i, h: ((i // nsb) * n_heads + h, i % nsb))),
        scratch_shapes=[pltpu.VMEM((tm, R), BF16)],
        compiler_params=_cparams("parallel", "arbitrary"),
        name="mla_kv_proj",
    )(zkv, g.reshape(1, R), w, ctab, stab)


ATTN_TILES = 4


def _attn_kernel(q_ref, k_ref, vt_ref, o_ref, *, tq):
    k = k_ref[...]
    tiles = [slice(j * tq, (j + 1) * tq) for j in range(ATTN_TILES)]
    scores = lambda t: _dot_nt(k, q_ref[t, :])

    def finish(t, st):
        p = jnp.exp2(st - jnp.max(st, axis=0, keepdims=True))
        ot = _dot(vt_ref[...], p.astype(BF16))
        o_ref[t, :] = (ot / jnp.sum(p, axis=0, keepdims=True)).T.astype(o_ref.dtype)

    st = scores(tiles[0])
    for j, t in enumerate(tiles):
        st_next = scores(tiles[j + 1]) if j + 1 < len(tiles) else None
        finish(t, st)
        st = st_next


def mla_attention(q, k, vt, batch, n_heads, *, tq=256):
    T = q.shape[0]
    S = T // batch
    tb = tq * ATTN_TILES
    nq = S // tb
    QW = q.shape[1] // n_heads
    return pl.pallas_call(
        functools.partial(_attn_kernel, tq=tq),
        out_shape=jax.ShapeDtypeStruct((T, n_heads * V_HEAD), BF16),
        grid=(batch, n_heads, nq),
        in_specs=[pl.BlockSpec((tb, QW), lambda b, h, i: (b * nq + i, h)),
                  pl.BlockSpec((S, QW), lambda b, h, i: (b, h)),
                  pl.BlockSpec((V_HEAD, S), lambda b, h, i: (b * n_heads + h, 0))],
        out_specs=pl.BlockSpec((tb, V_HEAD), lambda b, h, i: (b * nq + i, h)),
        compiler_params=_cparams("parallel", "parallel", "arbitrary"),
        name="mla_attention",
    )(q, k, vt)


def _rope_tables(positions):
    inv_freq = 1.0 / (ROPE_THETA ** (jnp.arange(0, QK_ROPE, 2, dtype=F32) / QK_ROPE))
    ang = positions.astype(F32).reshape(-1, 1) * inv_freq
    cos, sin = jnp.cos(ang), jnp.sin(ang)
    T = ang.shape[0]
    z64 = jnp.zeros((T, LANE - QK_ROPE), F32)
    ck = jnp.concatenate([cos, cos, z64], axis=1)
    sk = jnp.concatenate([-sin, sin, z64], axis=1)
    return ck * Q_SCALE, sk * Q_SCALE, ck, sk


def _mla_q_weights(w_uq, n_heads):
    R = w_uq.shape[0]
    w = w_uq.reshape(R, n_heads, QK_NOPE + QK_ROPE)
    half = QK_ROPE // 2
    w = jnp.concatenate([w, w[:, :, QK_NOPE + half:], w[:, :, QK_NOPE:QK_NOPE + half]], axis=2)
    return w.reshape(R, n_heads * 2 * LANE).astype(BF16)


N_LEVELS = 6
HGRN_BATCH = 4


def _gla_constants():
    C = CHUNK
    t = np.arange(C)[:, None]
    u = np.arange(C)[None, :]
    mk = np.zeros((N_LEVELS + 1, C, C), np.float32)
    for l in range(N_LEVELS):
        m = C >> (l + 1)
        mk[l] = (t // (2 * m) == u // (2 * m)) & ((t % (2 * m)) >= m) & ((u % (2 * m)) < m)
    mk[N_LEVELS] = t == u
    tri = (u <= t).astype(np.float32)
    to2d = lambda a: np.ascontiguousarray(a).reshape(-1, C)
    return tri, np.ascontiguousarray(tri[::-1, ::-1]), to2d(mk), to2d(mk[:, ::-1, ::-1])


def _level_exponent(b, g, m, rev, row):
    C = CHUNK
    if m >= 4:
        r = m if rev else m - 1
        b_mid = jnp.concatenate([jnp.broadcast_to(b[s + r:s + r + 1, :], (2 * m, b.shape[1]))
                                 for s in range(0, C, 2 * m)], axis=0)
        return -jnp.abs(b - b_mid)
    g_next = pltpu.roll(g, C - 1, 0)
    g_prev = pltpu.roll(g, 1, 0)
    if m == 2:
        p4 = row % 4
        if rev:
            return jnp.where(p4 == 3, g_prev, jnp.where(p4 == 2, 0.0, jnp.where(p4 == 1, g, g + g_next)))
        return jnp.where(p4 == 0, g_next, jnp.where(p4 == 1, 0.0, jnp.where(p4 == 2, g, g + g_prev)))
    return jnp.where((row % 2 == 1) != rev, g, 0.0)


def _gla_local(chunks):
    C = CHUNK
    row = lax.broadcasted_iota(jnp.int32, (C, 1), 0)
    gs = [c[3] * LOG2_E for c in chunks]
    parts = [_split3(g) for g in gs]
    bs = [_dot(c[4], p[0]) + _dot(c[4], p[1]) + _dot(c[4], p[2]) for c, p in zip(chunks, parts)]
    qbs = [c[0].astype(BF16) for c in chunks]
    kbs = [c[1].astype(BF16) for c in chunks]
    scores = [c[5][N_LEVELS * C:(N_LEVELS + 1) * C] * _dot_nt(qb, kb) for c, qb, kb in zip(chunks, qbs, kbs)]
    for l in range(N_LEVELS):
        pls = [jnp.exp2(_level_exponent(b, g, C >> (l + 1), c[6], row)).astype(BF16)
               for c, b, g in zip(chunks, bs, gs)]
        prods = [_dot_nt(qb * p, kb * p) for qb, kb, p in zip(qbs, kbs, pls)]
        scores = [s + c[5][l * C:(l + 1) * C] * a for s, c, a in zip(scores, chunks, prods)]
    out = []
    for c, b, s in zip(chunks, bs, scores):
        q, k, v, rev = c[0], c[1], c[2], c[6]
        b_tot = b[0:1] if rev else b[C - 1:C]
        vb = v.astype(BF16)
        out.append((_dot(s.astype(BF16), vb), (q * jnp.exp2(b)).astype(BF16),
                    _dot_tn(vb, (k * jnp.exp2(b_tot - b)).astype(BF16)), jnp.exp2(b_tot)))
    return out


def _hgrn_kernel(q_ref, zf_ref, zb_ref, i_ref, g_ref, lb_ref, ng_ref, trif_ref, trib_ref, mkf_ref, mkb_ref,
                 o_ref, of_ref, ob_ref, *, n_chunks):
    C = CHUNK
    lb = lb_ref[...]
    one_m_lb = 1.0 - lb
    trif, trib, mkf, mkb = trif_ref[...], trib_ref[...], mkf_ref[...], mkb_ref[...]

    def gate(zz):
        w = one_m_lb * _sigmoid(zz)
        return one_m_lb - w, jnp.log(jnp.maximum(lb + w, F_TINY))

    def load(ref, c):
        return ref[pl.ds(pl.multiple_of(c * C, C), C), :]

    def chunk_inputs(c, z_ref, tri, mk, rev):
        q = load(q_ref, c)
        k, g = gate(load(z_ref, c))
        return (q * _sigmoid(q), k, load(i_ref, c), g, tri, mk, rev)

    def body(it, carry):
        states = list(carry)
        ids = [[it * HGRN_BATCH + j for j in range(HGRN_BATCH)]]
        ids.append([n_chunks - 1 - c for c in ids[0]])
        loc = _gla_local([chunk_inputs(c, zf_ref, trif, mkf, False) for c in ids[0]]
                         + [chunk_inputs(c, zb_ref, trib, mkb, True) for c in ids[1]])
        for d, out_ref in enumerate((of_ref, ob_ref)):
            for j, c in enumerate(ids[d]):
                o_intra, q_dec, st_inc, p_tot = loc[d * HGRN_BATCH + j]
                out_ref[pl.ds(pl.multiple_of(c * C, C), C), :] = o_intra + _dot_nt(q_dec, states[d].astype(BF16))
                states[d] = states[d] * p_tot + st_inc
        return tuple(states)

    dv, dk = i_ref.shape[1], q_ref.shape[1]
    z = jnp.zeros((dv, dk), F32)
    lax.fori_loop(0, n_chunks // HGRN_BATCH, body, (z, z))

    o = of_ref[...] + ob_ref[...]
    ms = jnp.mean(o * o, axis=-1, keepdims=True)
    gg = g_ref[...]
    o_ref[...] = (o * lax.rsqrt(ms + NORM_EPS) * ng_ref[...] * (gg * _sigmoid(gg))).astype(o_ref.dtype)


def hgrn2_mixer(z, lb, norm_g, batch, *, col0=0):
    T = z.shape[0]
    W = lb.shape[-1]
    S = T // batch
    H = W // HEAD_B
    c0 = col0 // HEAD_B
    trif, trib, mkf, mkb = _gla_constants()
    part = lambda n: pl.BlockSpec((S, HEAD_B), functools.partial(lambda b, h, n: (b, c0 + n * H + h), n=n))
    const = lambda a: pl.BlockSpec(a.shape, lambda b, h: (0, 0))
    return pl.pallas_call(
        functools.partial(_hgrn_kernel, n_chunks=S // CHUNK),
        out_shape=jax.ShapeDtypeStruct((T, W), BF16),
        grid=(batch, H),
        in_specs=[part(0), part(1), part(2), part(3), part(4),
                  pl.BlockSpec((1, HEAD_B), lambda b, h: (0, h)),
                  pl.BlockSpec((1, HEAD_B), lambda b, h: (0, 0)),
                  const(trif), const(trib), const(mkf), const(mkb)],
        out_specs=pl.BlockSpec((S, HEAD_B), lambda b, h: (b, h)),
        scratch_shapes=[pltpu.VMEM((S, HEAD_B), F32), pltpu.VMEM((S, HEAD_B), F32)],
        compiler_params=_cparams("parallel", "parallel"),
        name="hgrn2_mixer",
    )(z, z, z, z, z, lb.reshape(1, W), norm_g.reshape(1, HEAD_B),
      jnp.asarray(trif, BF16), jnp.asarray(trib, BF16), jnp.asarray(mkf), jnp.asarray(mkb))


def _head_sum(x, bd):
    x1, x2, x3 = _split3(x)
    return _dot(x1, bd) + _dot(x2, bd) + _dot(x3, bd)


def _rwkv_prep_kernel(x_ref, xp_ref, xn_ref, mu_ref, w0_ref, w2_ref, a0_ref, a2_ref, g2_ref, kk_ref,
                      ka_ref, rk_ref, bd_ref,
                      r_ref, v_ref, kap_ref, kdf_ref, kdb_ref, alf_ref, alb_ref, lwf_ref, lwb_ref,
                      g_ref, bon_ref, *, tm, seq, width):
    W = width
    i = pl.program_id(0)
    x = x_ref[...]
    at_start = (i * tm) % seq == 0
    at_end = ((i + 1) * tm) % seq == 0
    prev_row = jnp.where(at_start, 0.0, xp_ref[7:8, :])
    next_row = jnp.where(at_end, 0.0, xn_ref[0:1, :])
    row = lax.broadcasted_iota(jnp.int32, (tm, 1), 0)
    x_prev = jnp.where(row == 0, prev_row, pltpu.roll(x, 1, 0))
    x_next = jnp.where(row == tm - 1, next_row, pltpu.roll(x, tm - 1, 0))
    u = x + mu_ref[0:1, :] * (x_prev - x) + mu_ref[1:2, :] * (x_next - x)

    r, k, v = u[:, :W], u[:, W:2 * W], u[:, 2 * W:3 * W]
    o = 3 * W
    wd = jnp.tanh(u[:, o:o + 2 * LORA_DECAY]).astype(BF16)
    o += 2 * LORA_DECAY
    ad = u[:, o:o + 2 * LORA_AAA].astype(BF16)
    o += 2 * LORA_AAA
    gd = _sigmoid(u[:, o:]).astype(BF16)

    bd = bd_ref[...]
    kkr = k * kk_ref[...]
    sq = kkr * kkr
    rks = []
    a_dir = []
    for n in range(2):
        w_raw = w0_ref[n:n + 1, :] + _dot(wd[:, n * LORA_DECAY:(n + 1) * LORA_DECAY], w2_ref[n])
        lw = -DECAY_SCALE * _sigmoid(w_raw)
        a = _sigmoid(a0_ref[n:n + 1, :] + _dot(ad[:, n * LORA_AAA:(n + 1) * LORA_AAA], a2_ref[n]))
        a_dir.append(a)
        (lwf_ref, lwb_ref)[n][...] = lw
    kd = [k * (1.0 + (a - 1.0) * ka_ref[...]) for a in a_dir]
    kdf_ref[...] = kd[0].astype(kdf_ref.dtype)
    kdb_ref[...] = kd[1].astype(kdb_ref.dtype)
    rkk = r * (kd[0] + kd[1]) * rk_ref[...]
    for c in range(W // LANE):
        sl = slice(c * LANE, (c + 1) * LANE)
        nrm = jnp.maximum(jnp.sqrt(_head_sum(sq[:, sl], bd)), 1e-12)
        kap = kkr[:, sl] / nrm
        kap_ref[:, sl] = kap.astype(kap_ref.dtype)
        alf_ref[:, sl] = (kap * a_dir[0][:, sl]).astype(alf_ref.dtype)
        alb_ref[:, sl] = (kap * a_dir[1][:, sl]).astype(alb_ref.dtype)
        bon_ref[:, sl] = _head_sum(rkk[:, sl], bd) * v[:, sl]
    r_ref[...] = r.astype(r_ref.dtype)
    v_ref[...] = v.astype(v_ref.dtype)
    g_ref[...] = _dot(gd, g2_ref[...]).astype(g_ref.dtype)


def rwkv_prep(z, mu, w0, w2, a0, a2, g2, k_k, k_a, r_k, batch, *, tm=256):
    T, ZW = z.shape
    W = w0.shape[-1]
    S = T // batch
    nb8 = tm // 8
    used = 3 * W + 2 * LORA_DECAY + 2 * LORA_AAA + LORA_GATE
    mu_p = jnp.pad(mu, ((0, 0), (0, ZW - used)))
    g2_p = jnp.pad(g2, ((0, ZW - used), (0, 0))).astype(BF16)
    bd = np.kron(np.eye(LANE // HEAD_A, dtype=np.float32), np.ones((HEAD_A, HEAD_A), np.float32))
    full = lambda a: pl.BlockSpec(a.shape, lambda i: (0,) * a.ndim)
    vec = lambda a: a.reshape(1, W)
    args = [z, z, z, mu_p, w0, w2.astype(BF16), a0, a2.astype(BF16), g2_p, vec(k_k), vec(k_a),
            vec(r_k), jnp.asarray(bd, BF16)]
    in_specs = [pl.BlockSpec((tm, ZW), lambda i: (i, 0)),
                pl.BlockSpec((8, ZW), lambda i: (jnp.maximum(i * nb8 - 1, 0), 0)),
                pl.BlockSpec((8, ZW), lambda i: (jnp.minimum((i + 1) * nb8, T // 8 - 1), 0))]
    in_specs += [full(a) for a in args[3:]]
    b16 = jax.ShapeDtypeStruct((T, W), BF16)
    f32 = jax.ShapeDtypeStruct((T, W), F32)
    return pl.pallas_call(
        functools.partial(_rwkv_prep_kernel, tm=tm, seq=S, width=W),
        out_shape=(b16,) * 7 + (f32, f32, b16, f32),
        grid=(T // tm,),
        in_specs=in_specs,
        out_specs=(pl.BlockSpec((tm, W), lambda i: (i, 0)),) * 11,
        compiler_params=_cparams("parallel"),
        name="rwkv_prep",
    )(*args)


RWKV_GROUP = 256


def _rwkv_constants():
    G, C = RWKV_GROUP, CHUNK
    t = np.arange(G)[:, None]
    u = np.arange(G)[None, :]
    dd = (((t // C) == (u // C)) & (u <= t)).astype(np.float32)
    return dd, np.ascontiguousarray(dd[::-1, ::-1])


def _bdot(a, b):
    return _dot(a.astype(BF16), b.astype(BF16))


def _unit_tri_inverses(ms, eye, diag_blocks):
    each = lambda f, *ls: [f(*a) for a in zip(*ls)]
    mds = [jnp.where(diag_blocks, m, 0.0) for m in ms]
    mos = each(lambda m, md: m - md, ms, mds)
    xs = [eye - md for md in mds]
    ps = each(_bdot, mds, mds)
    for _ in range(2):
        xs = each(lambda x, p: x + _bdot(x, p), xs, ps)
        ps = each(_bdot, ps, ps)
    tds = each(lambda x, p: x + _bdot(x, p), xs, ps)
    gs = each(_bdot, tds, mos)
    g2s = each(_bdot, gs, gs)
    ys = each(lambda g, g2: (eye - g) + _bdot(eye - g, g2), gs, g2s)
    return each(_bdot, ys, tds)


def _rwkv_groups(r, v, kap, dirs, eye, diag_blocks, head_masks):
    G = r.shape[0]
    pre = []
    for kd, al, lw, dd, rev, causal_incl, causal_strict in dirs:
        l1, l2, l3 = _split3(lw)
        b_in = _dot(dd, l1) + _dot(dd, l2) + _dot(dd, l3)
        r_tot = 0 if rev else CHUNK - 1
        b_tot = jnp.concatenate([jnp.broadcast_to(b_in[s + r_tot:s + r_tot + 1], (CHUNK, b_in.shape[1]))
                                 for s in range(0, G, CHUNK)], axis=0)
        b_ex, b_out = b_in - lw, b_tot - b_in
        p_neg = jnp.exp(-b_in)
        p_out = jnp.exp(b_out)
        pre.append(dict(kap_h=kap * jnp.exp(b_ex), r_h=r * jnp.exp(b_in), al_n=(al * p_neg).astype(BF16),
                        kd_n=(kd * p_neg).astype(BF16), kout=kd * p_out, aout=al * p_out, p_tot=jnp.exp(b_tot),
                        incl=causal_incl, strict=causal_strict))
    chains = [(p, hm) for p in pre for hm in head_masks]
    kap_hb = [jnp.where(hm, p["kap_h"], 0.0).astype(BF16) for p, hm in chains]
    r_hb = [jnp.where(hm, p["r_h"], 0.0).astype(BF16) for p, hm in chains]
    v_h = [jnp.where(hm, v, 0.0).astype(BF16) for _, hm in chains]
    ms = [jnp.where(p["strict"], _dot_nt(k_, p["al_n"]), 0.0) for (p, _), k_ in zip(chains, kap_hb)]
    ns = [jnp.where(p["strict"], _dot_nt(k_, p["kd_n"]), 0.0) for (p, _), k_ in zip(chains, kap_hb)]
    ras = [jnp.where(p["incl"], _dot_nt(r_, p["al_n"]), 0.0) for (p, _), r_ in zip(chains, r_hb)]
    rks = [jnp.where(p["incl"], _dot_nt(r_, p["kd_n"]), 0.0) for (p, _), r_ in zip(chains, r_hb)]
    nvs = [_bdot(n, vh).astype(BF16) for n, vh in zip(ns, v_h)]
    y0s = [_bdot(rk, vh) for rk, vh in zip(rks, v_h)]
    racs = [sum(ra[:, s:s + CHUNK] for s in range(0, G, CHUNK)) for ra in ras]
    tinvs = [t.astype(BF16) for t in _unit_tri_inverses(ms, eye, diag_blocks)]
    kw_uv = [_dot(t, jnp.concatenate([k_, nv], axis=1)) for t, k_, nv in zip(tinvs, kap_hb, nvs)]
    out = []
    for d, p in enumerate(pre):
        a, b = kw_uv[2 * d], kw_uv[2 * d + 1]
        out.append((a[:, :LANE] + b[:, :LANE], p["r_h"], jnp.concatenate(racs[2 * d:2 * d + 2], axis=1),
                    p["kout"], p["aout"], a[:, LANE:] + b[:, LANE:], y0s[2 * d] + y0s[2 * d + 1], p["p_tot"]))
    return out


N_LOCAL = 8


def _rwkv_local_kernel(r_ref, v_ref, kap_ref, kdf_ref, kdb_ref, alf_ref, alb_ref, lwf_ref, lwb_ref,
                       ddf_ref, ddb_ref, *out_refs):
    G, C = RWKV_GROUP, CHUNK
    ti = lax.broadcasted_iota(jnp.int32, (G, G), 0)
    si = lax.broadcasted_iota(jnp.int32, (G, G), 1)
    same_chunk = (ti // C) == (si // C)
    eye = (ti == si).astype(F32)
    diag_blocks = (ti // 16) == (si // 16)
    lane = lax.broadcasted_iota(jnp.int32, (1, LANE), 1)
    head_masks = [lane < HEAD_A, lane >= HEAD_A]
    r, v, kap = r_ref[...], v_ref[...], kap_ref[...]
    dirs = []
    for d, (kd_ref, al_ref, lw_ref, dd_ref) in enumerate(((kdf_ref, alf_ref, lwf_ref, ddf_ref),
                                                          (kdb_ref, alb_ref, lwb_ref, ddb_ref))):
        incl = same_chunk & ((si <= ti) if d == 0 else (si >= ti))
        strict = same_chunk & ((si < ti) if d == 0 else (si > ti))
        dirs.append((kd_ref[...], al_ref[...], lw_ref[...], dd_ref[...], d == 1, incl, strict))
    for d, res in enumerate(_rwkv_groups(r, v, kap, dirs, eye, diag_blocks, head_masks)):
        outs = out_refs[d * N_LOCAL:(d + 1) * N_LOCAL]
        for o_ref, val in zip(outs[:-1], res[:-1]):
            o_ref[...] = val.astype(o_ref.dtype)
        p_tot = res[-1]
        outs[-1][...] = jnp.concatenate([p_tot[c * C:c * C + 8] for c in range(G // C)], axis=0)


def rwkv_local(r, v, kap, kd_f, kd_b, al_f, al_b, lw_f, lw_b):
    T, W = r.shape
    G = RWKV_GROUP
    ddf, ddb = _rwkv_constants()
    blk = pl.BlockSpec((G, LANE), lambda i, h: (i, h))
    pblk = pl.BlockSpec((G // 8, LANE), lambda i, h: (i, h))
    const = lambda a: pl.BlockSpec(a.shape, lambda i, h: (0, 0))
    b16 = jax.ShapeDtypeStruct((T, W), BF16)
    per_dir = (b16,) * 7 + (jax.ShapeDtypeStruct((T // 8, W), F32),)
    outs = pl.pallas_call(
        _rwkv_local_kernel,
        out_shape=per_dir * 2,
        grid=(T // G, W // LANE),
        in_specs=[blk] * 9 + [const(ddf), const(ddb)],
        out_specs=((blk,) * 7 + (pblk,)) * 2,
        compiler_params=_cparams("parallel", "parallel"),
        name="rwkv_local",
    )(r, v, kap, kd_f, kd_b, al_f, al_b, lw_f, lw_b,
      jnp.asarray(ddf, BF16), jnp.asarray(ddb, BF16))
    return outs[:N_LOCAL], outs[N_LOCAL:]


SCAN_PAIRS = 4


def _rwkv_scan_chunks(chains, head_masks, same_head):
    C = CHUNK
    m1s = [_dot_nt(jnp.concatenate([c[0], c[1]], axis=0), c[9].astype(BF16)) for c in chains]
    us = [m1[:C] + c[5] for m1, c in zip(m1s, chains)]
    incs = [_dot_tn(jnp.concatenate([c[7].astype(BF16), (-u).astype(BF16)], axis=0),
                    jnp.concatenate([c[3], c[4]], axis=0)) for u, c in zip(us, chains)]
    sts = [jnp.where(same_head, c[9] * c[8] + inc, 0.0) for c, inc in zip(chains, incs)]
    u_cats = [jnp.concatenate([jnp.where(hm, u, 0.0).astype(BF16) for hm in head_masks], axis=0) for u in us]
    ys = [m1[C:] + c[6] - _dot(c[2], u_cat) for m1, c, u_cat in zip(m1s, chains, u_cats)]
    return list(zip(ys, sts))


def _rwkv_scan_kernel(*refs, n_chunks, n_pairs):
    C = CHUNK
    f_refs, vf_ref = refs[:N_LOCAL], refs[N_LOCAL]
    b_refs, vb_ref = refs[N_LOCAL + 1:2 * N_LOCAL + 1], refs[2 * N_LOCAL + 1]
    yf_ref, yb_ref, stf_ref, stb_ref = refs[2 * N_LOCAL + 2:]

    @pl.when(pl.program_id(2) == 0)
    def _():
        stf_ref[...] = jnp.zeros_like(stf_ref)
        stb_ref[...] = jnp.zeros_like(stb_ref)

    lane = lax.broadcasted_iota(jnp.int32, (1, LANE), 1)
    head_masks = [lane < HEAD_A, lane >= HEAD_A]
    vi = lax.broadcasted_iota(jnp.int32, (LANE, LANE), 0)
    ki = lax.broadcasted_iota(jnp.int32, (LANE, LANE), 1)
    same_head = (vi // HEAD_A) == (ki // HEAD_A)

    def rows(ref, c, n):
        return ref[pl.ds(pl.multiple_of(c * n, n), n), :]

    def body(it, carry):
        states = [list(carry[0]), list(carry[1])]
        chains = []
        for d, (d_refs, v_ref) in enumerate(((f_refs, vf_ref), (b_refs, vb_ref))):
            c = it if d == 0 else n_chunks - 1 - it
            vals = [rows(ref, c, C) for ref in d_refs[:-1]] + [rows(v_ref, c, C), rows(d_refs[-1], c, 8)[0:1]]
            for p in range(n_pairs):
                chains.append(tuple(a[:, p * LANE:(p + 1) * LANE] for a in vals) + (states[d][p],))
        res = _rwkv_scan_chunks(chains, head_masks, same_head)
        for d, y_ref in enumerate((yf_ref, yb_ref)):
            c = it if d == 0 else n_chunks - 1 - it
            y_ref[pl.ds(pl.multiple_of(c * C, C), C), :] = jnp.concatenate(
                [res[d * n_pairs + p][0] for p in range(n_pairs)], axis=1)
            states[d] = [res[d * n_pairs + p][1] for p in range(n_pairs)]
        return tuple(states[0]), tuple(states[1])

    init = tuple(tuple(ref[p] for p in range(n_pairs)) for ref in (stf_ref, stb_ref))
    st_f, st_b = lax.fori_loop(0, n_chunks, body, init)
    for p in range(n_pairs):
        stf_ref[p] = st_f[p]
        stb_ref[p] = st_b[p]


def rwkv_scan(loc_f, loc_b, v, batch, *, ts=512):
    T, W = v.shape
    S = T // batch
    ts = min(ts, S)
    ns = S // ts
    bw = LANE * SCAN_PAIRS
    fwd = lambda rows: pl.BlockSpec((rows, bw), lambda b, h, s: (b * ns + s, h))
    bwd = lambda rows: pl.BlockSpec((rows, bw), lambda b, h, s: (b * ns + ns - 1 - s, h))
    specs = lambda mk: [mk(ts)] * (N_LOCAL - 1) + [mk(ts // 8), mk(ts)]
    out = jax.ShapeDtypeStruct((T, W), F32)
    state = pltpu.VMEM((SCAN_PAIRS, LANE, LANE), F32)
    return pl.pallas_call(
        functools.partial(_rwkv_scan_kernel, n_chunks=ts // CHUNK, n_pairs=SCAN_PAIRS),
        out_shape=(out, out),
        grid=(batch, W // bw, ns),
        in_specs=specs(fwd) + specs(bwd),
        out_specs=(fwd(ts), bwd(ts)),
        scratch_shapes=[state, state],
        compiler_params=_cparams("parallel", "parallel", "arbitrary"),
        name="rwkv_scan",
    )(*loc_f, v, *loc_b, v)


def _rwkv_out_kernel(yf_ref, yb_ref, bon_ref, g_ref, gw_ref, gb_ref, bd_ref, o_ref, *, width):
    bd = bd_ref[...]
    inv_n = 1.0 / HEAD_A
    for c in range(width // LANE):
        sl = slice(c * LANE, (c + 1) * LANE)
        y = yf_ref[:, sl] + yb_ref[:, sl]
        mean = _head_sum(y, bd) * inv_n
        d = y - mean
        var = _head_sum(d * d, bd) * inv_n
        yn = d * lax.rsqrt(var + GN_EPS) * gw_ref[:, sl] + gb_ref[:, sl] + bon_ref[:, sl]
        o_ref[:, sl] = (yn * g_ref[:, sl]).astype(o_ref.dtype)


def rwkv_out(y_f, y_b, bonus, g, gn_w, gn_b, *, tm=512):
    T, W = y_f.shape
    bd = np.kron(np.eye(LANE // HEAD_A, dtype=np.float32), np.ones((HEAD_A, HEAD_A), np.float32))
    row = pl.BlockSpec((tm, W), lambda i: (i, 0))
    vec = pl.BlockSpec((1, W), lambda i: (0, 0))
    return pl.pallas_call(
        functools.partial(_rwkv_out_kernel, width=W),
        out_shape=jax.ShapeDtypeStruct((T, W), BF16),
        grid=(T // tm,),
        in_specs=[row, row, row, row, vec, vec, pl.BlockSpec((LANE, LANE), lambda i: (0, 0))],
        out_specs=row,
        compiler_params=_cparams("parallel"),
        name="rwkv_out",
    )(y_f, y_b, bonus, g, gn_w.reshape(1, W), gn_b.reshape(1, W), jnp.asarray(bd, BF16))


def rwkv7_mixer(z, mu, w0, w2, a0, a2, g2, k_k, k_a, r_k, gn_w, gn_b, batch):
    r, v, kap, kd_f, kd_b, al_f, al_b, lw_f, lw_b, g, bonus = rwkv_prep(
        z, mu, w0, w2, a0, a2, g2, k_k, k_a, r_k.reshape(-1), batch)
    loc_f, loc_b = rwkv_local(r, v, kap, kd_f, kd_b, al_f, al_b, lw_f, lw_b)
    y_f, y_b = rwkv_scan(loc_f, loc_b, v, batch)
    return rwkv_out(y_f, y_b, bonus, g, gn_w, gn_b)


def _relu2(acc):
    r = jnp.maximum(acc, 0.0)
    return r * r


def _add(acc, res):
    return acc + res


def _pad_cols(w, mult):
    return jnp.pad(w, ((0, 0), (0, (-w.shape[1]) % mult)))


def kernel(x, p, positions, ln1_g, w_in, rwkv_mu, rwkv_w0, rwkv_w2, rwkv_a0, rwkv_a2, rwkv_g2, rwkv_kk, rwkv_ka, rwkv_rk, rwkv_gn_w, rwkv_gn_b, hgrn_lb, hgrn_norm_g, mla_q_norm_g, mla_kv_norm_g, mla_w_uq, mla_w_ukv, w_branch, w_o, ln2_g, w_mlp1, w_mlp2, w_pe, w_pg, final_g):
    Bn, S, D = x.shape
    L = w_in.shape[0]
    T = Bn * S
    W = rwkv_w0.shape[-1]
    q_lora, kv_lora = mla_q_norm_g.shape[-1], mla_kv_norm_g.shape[-1]
    n_heads_c = mla_w_ukv.shape[-1] // (QK_NOPE + V_HEAD)
    rwkv_w = 3 * W + 2 * LORA_DECAY + 2 * LORA_AAA + LORA_GATE
    hgrn_w = 5 * W
    o_hgrn = rwkv_w
    o_cq = o_hgrn + hgrn_w
    o_ckv = o_cq + q_lora
    o_kr = o_ckv + kv_lora
    o_gate = o_kr + QK_ROPE
    half = QK_ROPE // 2

    lb_w = jax.nn.softmax(hgrn_lb.astype(F32), axis=0)
    lower_bounds = jnp.cumsum(lb_w, axis=0) - lb_w[0]
    cq_tab, sq_tab, ck_tab, sk_tab = _rope_tables(positions)

    h = x.reshape(T, D)
    hn = rmsnorm(h, ln1_g[0], BF16)
    for l in range(L):
        wl = w_in[l]
        w_rwkv = _pad_cols(wl[:, :rwkv_w], 512).astype(BF16)
        w_hgrn = wl[:, o_hgrn:o_cq].astype(BF16)
        w_cq = wl[:, o_cq:o_ckv].astype(BF16)
        w_kr = wl[:, o_kr:o_gate]
        w_kv = jnp.concatenate(
            [wl[:, o_ckv:o_kr], _pad_cols(w_kr, LANE),
             _pad_cols(jnp.concatenate([w_kr[:, half:], w_kr[:, :half]], axis=1), LANE)], axis=1).astype(BF16)
        w_gate = wl[:, o_gate:].astype(BF16)

        z_rwkv = matmul(hn, w_rwkv, out_dtype=F32, tm=1024, tn=w_rwkv.shape[1] // 2, name="in_rwkv")
        z_hgrn = matmul(hn, w_hgrn, out_dtype=F32, tm=1024, tn=1024, name="in_hgrn")
        z_cq = matmul(hn, w_cq, out_dtype=F32, tm=1024, tn=w_cq.shape[1], name="in_cq")
        z_kv = matmul(hn, w_kv, out_dtype=F32, tm=1024, tn=w_kv.shape[1], name="in_kv")
        z_gate = matmul(hn, w_gate, out_dtype=BF16, tm=1024, tn=1024, name="in_gate")

        y_a = rwkv7_mixer(z_rwkv, rwkv_mu[l], rwkv_w0[l], rwkv_w2[l], rwkv_a0[l], rwkv_a2[l], rwkv_g2[l],
                          rwkv_kk[l], rwkv_ka[l], rwkv_rk[l], rwkv_gn_w[l], rwkv_gn_b[l], Bn)
        y_b = hgrn2_mixer(z_hgrn, lower_bounds[l], hgrn_norm_g[l], Bn)
        q = mla_q_proj(z_cq, mla_q_norm_g[l], _mla_q_weights(mla_w_uq[l], n_heads_c), cq_tab, sq_tab,
                       n_heads_c, Q_SCALE)
        k, vt = mla_kv_proj(z_kv, mla_kv_norm_g[l], mla_w_ukv[l].astype(BF16), ck_tab, sk_tab, n_heads_c, Bn)
        y_c = mla_attention(q, k, vt, Bn, n_heads_c)

        mixed = branch_mix(y_a, y_b, y_c, w_branch[l].astype(BF16), z_gate)
        h, hn2 = out_proj_norm(mixed, w_o[l].astype(BF16), h, ln2_g[l])
        hid = matmul(hn2, w_mlp1[l].astype(BF16), out_dtype=BF16, tm=1024, tn=1024, epilogue=_relu2,
                     name="mlp1")
        h = matmul(hid, w_mlp2[l].astype(BF16), out_dtype=F32, tm=1024, tn=1024, tk=2048, epilogue=_add,
                   extras=(h,), alias_extra=0, name="mlp2")
        last = l == L - 1
        h, hn = ple_update_norm(h, w_pg[l].astype(BF16), p[l].reshape(T, -1), w_pe[l].astype(BF16),
                                final_g if last else ln1_g[l + 1], F32 if last else BF16)
    return hn.reshape(Bn, S, D)
```

```python
import functools

import numpy as np
import jax
import jax.numpy as jnp
from jax import lax
from jax.experimental import pallas as pl
from jax.experimental.pallas import tpu as pltpu

F32 = jnp.float32
BF16 = jnp.bfloat16

LANE = 128
VMEM_LIMIT = 48 * 2**20

HEAD_A = 64
LORA_DECAY = 64
LORA_AAA = 64
LORA_GATE = 160
DECAY_SCALE = 0.606531
GN_EPS = 64e-5
HEAD_B = 128
F_TINY = 1e-30
QK_NOPE = 128
QK_ROPE = 64
V_HEAD = 128
ROPE_THETA = 10000.0
NORM_EPS = 1e-6
CHUNK = 64
LOG2_E = 1.4426950408889634
Q_SCALE = (QK_NOPE + QK_ROPE) ** -0.5 * LOG2_E


def _cparams(*sem, flags=None):
    return pltpu.CompilerParams(dimension_semantics=sem, vmem_limit_bytes=VMEM_LIMIT, flags=flags)


def _sigmoid(x):
    return 1.0 / (1.0 + jnp.exp(-x))


def _dot(a, b):
    return jnp.dot(a, b, preferred_element_type=F32)


def _dot_nt(a, b):
    return lax.dot_general(a, b, (((1,), (1,)), ((), ())), preferred_element_type=F32)


def _dot_tn(a, b):
    return lax.dot_general(a, b, (((0,), (0,)), ((), ())), preferred_element_type=F32)


def _split3(x):
    x1 = x.astype(BF16)
    r1 = x - x1.astype(F32)
    x2 = r1.astype(BF16)
    x3 = (r1 - x2.astype(F32)).astype(BF16)
    return x1, x2, x3


def _rmsnorm_kernel(x_ref, g_ref, o_ref):
    x = x_ref[...]
    ms = jnp.mean(x * x, axis=-1, keepdims=True)
    o_ref[...] = (x * lax.rsqrt(ms + NORM_EPS) * g_ref[...]).astype(o_ref.dtype)


def rmsnorm(x, g, out_dtype, tm=512):
    T, D = x.shape
    return pl.pallas_call(
        _rmsnorm_kernel,
        out_shape=jax.ShapeDtypeStruct((T, D), out_dtype),
        grid=(T // tm,),
        in_specs=[pl.BlockSpec((tm, D), lambda i: (i, 0)),
                  pl.BlockSpec((1, D), lambda i: (0, 0))],
        out_specs=pl.BlockSpec((tm, D), lambda i: (i, 0)),
        compiler_params=_cparams("parallel"),
        name="rmsnorm",
    )(x, g.reshape(1, D))


def _mm_kernel(a_ref, w_ref, *rest, nk, epilogue, n_extra):
    extras = rest[:n_extra]
    o_ref = rest[n_extra]

    def finish(acc):
        o_ref[...] = epilogue(acc, *[e[...] for e in extras]).astype(o_ref.dtype)

    if nk == 1:
        finish(_dot(a_ref[...], w_ref[...]))
    else:
        acc_ref = rest[n_extra + 1]
        k = pl.program_id(2)

        @pl.when(k == 0)
        def _():
            acc_ref[...] = jnp.zeros_like(acc_ref)

        acc_ref[...] += _dot(a_ref[...], w_ref[...])

        @pl.when(k == nk - 1)
        def _():
            finish(acc_ref[...])


def matmul(a, w, *, out_dtype, tm, tn, tk=None, epilogue=None, extras=(), alias_extra=None,
           name="matmul"):
    M, K = a.shape
    N = w.shape[1]
    tk = K if tk is None else tk
    nk = K // tk
    epilogue = epilogue or (lambda acc: acc)
    kern = functools.partial(_mm_kernel, nk=nk, epilogue=epilogue, n_extra=len(extras))
    in_specs = [pl.BlockSpec((tm, tk), lambda i, j, k: (i, k)),
                pl.BlockSpec((tk, tn), lambda i, j, k: (k, j))]
    in_specs += [pl.BlockSpec((tm, tn), lambda i, j, k: (i, j)) for _ in extras]
    aliases = {} if alias_extra is None else {2 + alias_extra: 0}
    return pl.pallas_call(
        kern,
        out_shape=jax.ShapeDtypeStruct((M, N), out_dtype),
        grid=(M // tm, N // tn, nk),
        in_specs=in_specs,
        out_specs=pl.BlockSpec((tm, tn), lambda i, j, k: (i, j)),
        scratch_shapes=[pltpu.VMEM((tm, tn), F32)] if nk > 1 else [],
        input_output_aliases=aliases,
        compiler_params=_cparams("parallel", "parallel", "arbitrary"),
        name=name,
    )(a, w, *extras)


def _branch_kernel(ya_ref, yb_ref, yc_ref, p_ref, ga_ref, gb_ref, gc_ref, o_ref):
    acc = _sigmoid(ga_ref[...].astype(F32)) * _dot(ya_ref[...], p_ref[0])
    acc += _sigmoid(gb_ref[...].astype(F32)) * _dot(yb_ref[...], p_ref[1])
    acc += _sigmoid(gc_ref[...].astype(F32)) * _dot(yc_ref[...], p_ref[2])
    o_ref[...] = acc.astype(o_ref.dtype)


def branch_mix(ya, yb, yc, p, zg, *, tm=1024, tn=512):
    T, W = ya.shape
    D = p.shape[2]
    nj = D // tn
    y_spec = pl.BlockSpec((tm, W), lambda i, j: (i, 0))
    g_specs = [pl.BlockSpec((tm, tn), functools.partial(lambda i, j, n: (i, n * nj + j), n=n))
               for n in range(3)]
    return pl.pallas_call(
        _branch_kernel,
        out_shape=jax.ShapeDtypeStruct((T, D), BF16),
        grid=(T // tm, nj),
        in_specs=[y_spec, y_spec, y_spec,
                  pl.BlockSpec((3, W, tn), lambda i, j: (0, 0, j))] + g_specs,
        out_specs=pl.BlockSpec((tm, tn), lambda i, j: (i, j)),
        compiler_params=_cparams("parallel", "parallel"),
        name="branch_mix",
    )(ya, yb, yc, p, zg, zg, zg)


def _rms_rows(h, g):
    return h * lax.rsqrt(jnp.mean(h * h, axis=-1, keepdims=True) + NORM_EPS) * g


def _wo_ln_kernel(a_ref, w_ref, h_ref, g_ref, h_out_ref, hn_ref):
    h = h_ref[...] + _dot(a_ref[...], w_ref[...])
    h_out_ref[...] = h
    hn_ref[...] = _rms_rows(h, g_ref[...]).astype(hn_ref.dtype)


def out_proj_norm(a, w, h, g, *, tm=512):
    T, D = h.shape
    K = a.shape[1]
    row = lambda width: pl.BlockSpec((tm, width), lambda i: (i, 0))
    return pl.pallas_call(
        _wo_ln_kernel,
        out_shape=(jax.ShapeDtypeStruct((T, D), F32), jax.ShapeDtypeStruct((T, D), BF16)),
        grid=(T // tm,),
        in_specs=[row(K), pl.BlockSpec((K, D), lambda i: (0, 0)), row(D), pl.BlockSpec((1, D), lambda i: (0, 0))],
        out_specs=(row(D), row(D)),
        input_output_aliases={2: 0},
        compiler_params=_cparams("parallel"),
        name="out_proj_norm",
    )(a, w, h, g.reshape(1, D))


def _ple_ln_kernel(h_ref, wpg_ref, p_ref, wpe_ref, g_ref, h_out_ref, hn_ref):
    h = h_ref[...]
    gate = _sigmoid(_dot(h.astype(BF16), wpg_ref[...]))
    h = h + gate * _dot(p_ref[...].astype(BF16), wpe_ref[...])
    h_out_ref[...] = h
    hn_ref[...] = _rms_rows(h, g_ref[...]).astype(hn_ref.dtype)


def ple_update_norm(h, wpg, p, wpe, g, norm_dtype, *, tm=512):
    T, D = h.shape
    E = p.shape[1]
    row = lambda width: pl.BlockSpec((tm, width), lambda i: (i, 0))
    full = lambda r, c: pl.BlockSpec((r, c), lambda i: (0, 0))
    return pl.pallas_call(
        _ple_ln_kernel,
        out_shape=(jax.ShapeDtypeStruct((T, D), F32), jax.ShapeDtypeStruct((T, D), norm_dtype)),
        grid=(T // tm,),
        in_specs=[row(D), full(D, D), row(E), full(E, D), full(1, D)],
        out_specs=(row(D), row(D)),
        input_output_aliases={0: 0},
        compiler_params=_cparams("parallel"),
        name="ple_update_norm",
    )(h, wpg, p, wpe, g.reshape(1, D))


def _mla_q_kernel(cq_ref, g_ref, w_ref, c_ref, s_ref, o_ref, *, scale, n_heads):
    xn = _rms_rows(cq_ref[...], g_ref[...]).astype(BF16)
    x = _dot(xn, w_ref[...])
    c, s = c_ref[...], s_ref[...]
    for h in range(n_heads):
        lo = 2 * LANE * h
        hi = x[:, lo + QK_NOPE:lo + 2 * LANE]
        o_ref[:, lo:lo + QK_NOPE] = (x[:, lo:lo + QK_NOPE] * scale).astype(BF16)
        o_ref[:, lo + QK_NOPE:lo + 2 * LANE] = (hi * c + pltpu.roll(hi, LANE // 2, 1) * s).astype(BF16)


def mla_q_proj(cq, g, w, ctab, stab, n_heads, scale, *, tm=512):
    T, R = cq.shape
    N = w.shape[1]
    return pl.pallas_call(
        functools.partial(_mla_q_kernel, scale=scale, n_heads=n_heads),
        out_shape=jax.ShapeDtypeStruct((T, N), BF16),
        grid=(T // tm,),
        in_specs=[pl.BlockSpec((tm, R), lambda i: (i, 0)),
                  pl.BlockSpec((1, R), lambda i: (0, 0)),
                  pl.BlockSpec((R, N), lambda i: (0, 0)),
                  pl.BlockSpec((tm, LANE), lambda i: (i, 0)),
                  pl.BlockSpec((tm, LANE), lambda i: (i, 0))],
        out_specs=pl.BlockSpec((tm, N), lambda i: (i, 0)),
        compiler_params=_cparams("parallel"),
        name="mla_q_proj",
    )(cq, g.reshape(1, R), w, ctab, stab)


def _mla_kv_kernel(z_ref, g_ref, w_ref, c_ref, s_ref, k_ref, v_ref, *, kv_lora, n_heads):
    xn = _rms_rows(z_ref[:, :kv_lora], g_ref[...]).astype(BF16)
    kv = _dot(xn, w_ref[...])
    kr = (z_ref[:, kv_lora:kv_lora + LANE] * c_ref[...] + z_ref[:, kv_lora + LANE:] * s_ref[...]).astype(BF16)
    for h in range(n_heads):
        lo = 2 * LANE * h
        k_ref[:, lo:lo + QK_NOPE] = kv[:, lo:lo + QK_NOPE].astype(BF16)
        k_ref[:, lo + QK_NOPE:lo + 2 * LANE] = kr
        v_ref[h * V_HEAD:(h + 1) * V_HEAD, :] = kv[:, lo + QK_NOPE:lo + 2 * LANE].T.astype(BF16)


def mla_kv_proj(zkv, g, w, ctab, stab, n_heads, batch, *, tm=512):
    T, ZW = zkv.shape
    R = ZW - 2 * LANE
    S = T // batch
    tm = min(tm, S)
    nsb = S // tm
    N = w.shape[1]
    return pl.pallas_call(
        functools.partial(_mla_kv_kernel, kv_lora=R, n_heads=n_heads),
        out_shape=(jax.ShapeDtypeStruct((T, n_heads * 2 * LANE), BF16),
                   jax.ShapeDtypeStruct((batch * n_heads * V_HEAD, S), BF16)),
        grid=(T // tm,),
        in_specs=[pl.BlockSpec((tm, ZW), lambda i: (i, 0)),
                  pl.BlockSpec((1, R), lambda i: (0, 0)),
                  pl.BlockSpec((R, N), lambda i: (0, 0)),
                  pl.BlockSpec((tm, LANE), lambda i: (i, 0)),
                  pl.BlockSpec((tm, LANE), lambda i: (i, 0))],
        out_specs=(pl.BlockSpec((tm, n_heads * 2 * LANE), lambda i: (i, 0)),
                   pl.BlockSpec((n_heads * V_HEAD, tm), lambda i: (i // nsb, i % nsb))),
        compiler_params=_cparams("parallel"),
        name="mla_kv_proj",
    )(zkv, g.reshape(1, R), w, ctab, stab)


ATTN_TILES = 4


def _attn_kernel(q_ref, k_ref, vt_ref, o_ref, *, tq):
    k = k_ref[...]
    tiles = [slice(j * tq, (j + 1) * tq) for j in range(ATTN_TILES)]
    scores = lambda t: _dot_nt(k, q_ref[t, :])

    def finish(t, st):
        p = jnp.exp2(st - jnp.max(st, axis=0, keepdims=True))
        ot = _dot(vt_ref[...], p.astype(BF16))
        o_ref[t, :] = (ot / jnp.sum(p, axis=0, keepdims=True)).T.astype(o_ref.dtype)

    st = scores(tiles[0])
    for j, t in enumerate(tiles):
        st_next = scores(tiles[j + 1]) if j + 1 < len(tiles) else None
        finish(t, st)
        st = st_next


def mla_attention(q, k, vt, batch, n_heads, *, tq=256):
    T = q.shape[0]
    S = T // batch
    tb = tq * ATTN_TILES
    nq = S // tb
    QW = q.shape[1] // n_heads
    return pl.pallas_call(
        functools.partial(_attn_kernel, tq=tq),
        out_shape=jax.ShapeDtypeStruct((T, n_heads * V_HEAD), BF16),
        grid=(batch, n_heads, nq),
        in_specs=[pl.BlockSpec((tb, QW), lambda b, h, i: (b * nq + i, h)),
                  pl.BlockSpec((S, QW), lambda b, h, i: (b, h)),
                  pl.BlockSpec((V_HEAD, S), lambda b, h, i: (b * n_heads + h, 0))],
        out_specs=pl.BlockSpec((tb, V_HEAD), lambda b, h, i: (b * nq + i, h)),
        compiler_params=_cparams("parallel", "parallel", "arbitrary"),
        name="mla_attention",
    )(q, k, vt)


def _rope_tables(positions):
    inv_freq = 1.0 / (ROPE_THETA ** (jnp.arange(0, QK_ROPE, 2, dtype=F32) / QK_ROPE))
    ang = positions.astype(F32).reshape(-1, 1) * inv_freq
    cos, sin = jnp.cos(ang), jnp.sin(ang)
    T = ang.shape[0]
    z64 = jnp.zeros((T, LANE - QK_ROPE), F32)
    ck = jnp.concatenate([cos, cos, z64], axis=1)
    sk = jnp.concatenate([-sin, sin, z64], axis=1)
    return ck * Q_SCALE, sk * Q_SCALE, ck, sk


def _mla_q_weights(w_uq, n_heads):
    R = w_uq.shape[0]
    w = w_uq.reshape(R, n_heads, QK_NOPE + QK_ROPE)
    half = QK_ROPE // 2
    w = jnp.concatenate([w, w[:, :, QK_NOPE + half:], w[:, :, QK_NOPE:QK_NOPE + half]], axis=2)
    return w.reshape(R, n_heads * 2 * LANE).astype(BF16)


N_LEVELS = 6
HGRN_BATCH = 4


def _gla_constants():
    C = CHUNK
    t = np.arange(C)[:, None]
    u = np.arange(C)[None, :]
    mk = np.zeros((N_LEVELS + 1, C, C), np.float32)
    for l in range(N_LEVELS):
        m = C >> (l + 1)
        mk[l] = (t // (2 * m) == u // (2 * m)) & ((t % (2 * m)) >= m) & ((u % (2 * m)) < m)
    mk[N_LEVELS] = t == u
    tri = (u <= t).astype(np.float32)
    to2d = lambda a: np.ascontiguousarray(a).reshape(-1, C)
    return tri, np.ascontiguousarray(tri[::-1, ::-1]), to2d(mk), to2d(mk[:, ::-1, ::-1])


def _level_exponent(b, g, m, rev, row):
    C = CHUNK
    if m >= 4:
        r = m if rev else m - 1
        b_mid = jnp.concatenate([jnp.broadcast_to(b[s + r:s + r + 1, :], (2 * m, b.shape[1]))
                                 for s in range(0, C, 2 * m)], axis=0)
        return -jnp.abs(b - b_mid)
    g_next = pltpu.roll(g, C - 1, 0)
    g_prev = pltpu.roll(g, 1, 0)
    if m == 2:
        p4 = row % 4
        if rev:
            return jnp.where(p4 == 3, g_prev, jnp.where(p4 == 2, 0.0, jnp.where(p4 == 1, g, g + g_next)))
        return jnp.where(p4 == 0, g_next, jnp.where(p4 == 1, 0.0, jnp.where(p4 == 2, g, g + g_prev)))
    return jnp.where((row % 2 == 1) != rev, g, 0.0)


def _gla_local(chunks):
    C = CHUNK
    row = lax.broadcasted_iota(jnp.int32, (C, 1), 0)
    gs = [c[3] * LOG2_E for c in chunks]
    parts = [_split3(g) for g in gs]
    bs = [_dot(c[4], p[0]) + _dot(c[4], p[1]) + _dot(c[4], p[2]) for c, p in zip(chunks, parts)]
    qbs = [c[0].astype(BF16) for c in chunks]
    kbs = [c[1].astype(BF16) for c in chunks]
    scores = [c[5][N_LEVELS * C:(N_LEVELS + 1) * C] * _dot_nt(qb, kb) for c, qb, kb in zip(chunks, qbs, kbs)]
    for l in range(N_LEVELS):
        pls = [jnp.exp2(_level_exponent(b, g, C >> (l + 1), c[6], row)).astype(BF16)
               for c, b, g in zip(chunks, bs, gs)]
        prods = [_dot_nt(qb * p, kb * p) for qb, kb, p in zip(qbs, kbs, pls)]
        scores = [s + c[5][l * C:(l + 1) * C] * a for s, c, a in zip(scores, chunks, prods)]
    out = []
    for c, b, s in zip(chunks, bs, scores):
        q, k, v, rev = c[0], c[1], c[2], c[6]
        b_tot = b[0:1] if rev else b[C - 1:C]
        vb = v.astype(BF16)
        out.append((_dot(s.astype(BF16), vb), (q * jnp.exp2(b)).astype(BF16),
                    _dot_tn(vb, (k * jnp.exp2(b_tot - b)).astype(BF16)), jnp.exp2(b_tot)))
    return out


def _hgrn_kernel(q_ref, zf_ref, zb_ref, i_ref, g_ref, lb_ref, ng_ref, trif_ref, trib_ref, mkf_ref, mkb_ref,
                 o_ref, of_ref, ob_ref, *, n_chunks):
    C = CHUNK
    lb = lb_ref[...]
    one_m_lb = 1.0 - lb
    trif, trib, mkf, mkb = trif_ref[...], trib_ref[...], mkf_ref[...], mkb_ref[...]

    def gate(zz):
        w = one_m_lb * _sigmoid(zz)
        return one_m_lb - w, jnp.log(jnp.maximum(lb + w, F_TINY))

    def load(ref, c):
        return ref[pl.ds(pl.multiple_of(c * C, C), C), :]

    def chunk_inputs(c, z_ref, tri, mk, rev):
        q = load(q_ref, c)
        k, g = gate(load(z_ref, c))
        return (q * _sigmoid(q), k, load(i_ref, c), g, tri, mk, rev)

    def body(it, carry):
        states = list(carry)
        ids = [[it * HGRN_BATCH + j for j in range(HGRN_BATCH)]]
        ids.append([n_chunks - 1 - c for c in ids[0]])
        loc = _gla_local([chunk_inputs(c, zf_ref, trif, mkf, False) for c in ids[0]]
                         + [chunk_inputs(c, zb_ref, trib, mkb, True) for c in ids[1]])
        for d, out_ref in enumerate((of_ref, ob_ref)):
            for j, c in enumerate(ids[d]):
                o_intra, q_dec, st_inc, p_tot = loc[d * HGRN_BATCH + j]
                out_ref[pl.ds(pl.multiple_of(c * C, C), C), :] = o_intra + _dot_nt(q_dec, states[d].astype(BF16))
                states[d] = states[d] * p_tot + st_inc
        return tuple(states)

    dv, dk = i_ref.shape[1], q_ref.shape[1]
    z = jnp.zeros((dv, dk), F32)
    lax.fori_loop(0, n_chunks // HGRN_BATCH, body, (z, z))

    o = of_ref[...] + ob_ref[...]
    ms = jnp.mean(o * o, axis=-1, keepdims=True)
    gg = g_ref[...]
    o_ref[...] = (o * lax.rsqrt(ms + NORM_EPS) * ng_ref[...] * (gg * _sigmoid(gg))).astype(o_ref.dtype)


def hgrn2_mixer(z, lb, norm_g, batch, *, col0=0):
    T = z.shape[0]
    W = lb.shape[-1]
    S = T // batch
    H = W // HEAD_B
    c0 = col0 // HEAD_B
    trif, trib, mkf, mkb = _gla_constants()
    part = lambda n: pl.BlockSpec((S, HEAD_B), functools.partial(lambda b, h, n: (b, c0 + n * H + h), n=n))
    const = lambda a: pl.BlockSpec(a.shape, lambda b, h: (0, 0))
    return pl.pallas_call(
        functools.partial(_hgrn_kernel, n_chunks=S // CHUNK),
        out_shape=jax.ShapeDtypeStruct((T, W), BF16),
        grid=(batch, H),
        in_specs=[part(0), part(1), part(2), part(3), part(4),
                  pl.BlockSpec((1, HEAD_B), lambda b, h: (0, h)),
                  pl.BlockSpec((1, HEAD_B), lambda b, h: (0, 0)),
                  const(trif), const(trib), const(mkf), const(mkb)],
        out_specs=pl.BlockSpec((S, HEAD_B), lambda b, h: (b, h)),
        scratch_shapes=[pltpu.VMEM((S, HEAD_B), F32), pltpu.VMEM((S, HEAD_B), F32)],
        compiler_params=_cparams("parallel", "parallel"),
        name="hgrn2_mixer",
    )(z, z, z, z, z, lb.reshape(1, W), norm_g.reshape(1, HEAD_B),
      jnp.asarray(trif, BF16), jnp.asarray(trib, BF16), jnp.asarray(mkf), jnp.asarray(mkb))


def _head_sum(x, bd):
    x1, x2, _ = _split3(x)
    return _dot(x1, bd) + _dot(x2, bd)


def _rwkv_prep_kernel(x_ref, xp_ref, xn_ref, mu_ref, w0_ref, w2_ref, a0_ref, a2_ref, g2_ref, kk_ref,
                      ka_ref, rk_ref, bd_ref,
                      r_ref, v_ref, kap_ref, kdf_ref, kdb_ref, alf_ref, alb_ref, lwf_ref, lwb_ref,
                      g_ref, bon_ref, *, tm, seq, width):
    W = width
    i = pl.program_id(0)
    x = x_ref[...]
    at_start = (i * tm) % seq == 0
    at_end = ((i + 1) * tm) % seq == 0
    prev_row = jnp.where(at_start, 0.0, xp_ref[7:8, :])
    next_row = jnp.where(at_end, 0.0, xn_ref[0:1, :])
    row = lax.broadcasted_iota(jnp.int32, (8, 1), 0)
    x_prev = pltpu.roll(x, 1, 0)
    x_prev = jnp.concatenate([jnp.where(row == 0, prev_row, x_prev[:8]), x_prev[8:]], axis=0)
    x_next = pltpu.roll(x, tm - 1, 0)
    x_next = jnp.concatenate([x_next[:tm - 8], jnp.where(row == 7, next_row, x_next[tm - 8:])], axis=0)
    u = x + mu_ref[0:1, :] * (x_prev - x) + mu_ref[1:2, :] * (x_next - x)

    r, k, v = u[:, :W], u[:, W:2 * W], u[:, 2 * W:3 * W]
    o = 3 * W
    wd = jnp.tanh(u[:, o:o + 2 * LORA_DECAY]).astype(BF16)
    o += 2 * LORA_DECAY
    ad = u[:, o:o + 2 * LORA_AAA].astype(BF16)
    o += 2 * LORA_AAA
    gd = _sigmoid(u[:, o:]).astype(BF16)

    bd = bd_ref[...]
    kkr = k * kk_ref[...]
    sq = kkr * kkr
    rks = []
    a_dir = []
    for n in range(2):
        w_raw = w0_ref[n:n + 1, :] + _dot(wd[:, n * LORA_DECAY:(n + 1) * LORA_DECAY], w2_ref[n])
        lw = -DECAY_SCALE * _sigmoid(w_raw)
        a = _sigmoid(a0_ref[n:n + 1, :] + _dot(ad[:, n * LORA_AAA:(n + 1) * LORA_AAA], a2_ref[n]))
        a_dir.append(a)
        (lwf_ref, lwb_ref)[n][...] = lw
    kd = [k * (1.0 + (a - 1.0) * ka_ref[...]) for a in a_dir]
    kdf_ref[...] = kd[0].astype(kdf_ref.dtype)
    kdb_ref[...] = kd[1].astype(kdb_ref.dtype)
    rkk = r * (kd[0] + kd[1]) * rk_ref[...]
    for c in range(W // LANE):
        sl = slice(c * LANE, (c + 1) * LANE)
        nrm = jnp.maximum(jnp.sqrt(_head_sum(sq[:, sl], bd)), 1e-12)
        kap = kkr[:, sl] / nrm
        kap_ref[:, sl] = kap.astype(kap_ref.dtype)
        alf_ref[:, sl] = (kap * a_dir[0][:, sl]).astype(alf_ref.dtype)
        alb_ref[:, sl] = (kap * a_dir[1][:, sl]).astype(alb_ref.dtype)
        bon_ref[:, sl] = _head_sum(rkk[:, sl], bd) * v[:, sl]
    r_ref[...] = r.astype(r_ref.dtype)
    v_ref[...] = v.astype(v_ref.dtype)
    g_ref[...] = _dot(gd, g2_ref[...]).astype(g_ref.dtype)


def rwkv_prep(z, mu, w0, w2, a0, a2, g2, k_k, k_a, r_k, batch, *, tm=256):
    T, ZW = z.shape
    W = w0.shape[-1]
    S = T // batch
    nb8 = tm // 8
    used = 3 * W + 2 * LORA_DECAY + 2 * LORA_AAA + LORA_GATE
    mu_p = jnp.pad(mu, ((0, 0), (0, ZW - used)))
    g2_p = jnp.pad(g2, ((0, ZW - used), (0, 0))).astype(BF16)
    bd = np.kron(np.eye(LANE // HEAD_A, dtype=np.float32), np.ones((HEAD_A, HEAD_A), np.float32))
    full = lambda a: pl.BlockSpec(a.shape, lambda i: (0,) * a.ndim)
    vec = lambda a: a.reshape(1, W)
    args = [z, z, z, mu_p, w0, w2.astype(BF16), a0, a2.astype(BF16), g2_p, vec(k_k), vec(k_a),
            vec(r_k), jnp.asarray(bd, BF16)]
    in_specs = [pl.BlockSpec((tm, ZW), lambda i: (i, 0)),
                pl.BlockSpec((8, ZW), lambda i: (jnp.maximum(i * nb8 - 1, 0), 0)),
                pl.BlockSpec((8, ZW), lambda i: (jnp.minimum((i + 1) * nb8, T // 8 - 1), 0))]
    in_specs += [full(a) for a in args[3:]]
    b16 = jax.ShapeDtypeStruct((T, W), BF16)
    f32 = jax.ShapeDtypeStruct((T, W), F32)
    return pl.pallas_call(
        functools.partial(_rwkv_prep_kernel, tm=tm, seq=S, width=W),
        out_shape=(b16,) * 7 + (f32, f32, b16, f32),
        grid=(T // tm,),
        in_specs=in_specs,
        out_specs=(pl.BlockSpec((tm, W), lambda i: (i, 0)),) * 11,
        compiler_params=_cparams("parallel"),
        name="rwkv_prep",
    )(*args)


RWKV_GROUP = 256


def _rwkv_constants():
    G, C = RWKV_GROUP, CHUNK
    t = np.arange(G)[:, None]
    u = np.arange(G)[None, :]
    dd = (((t // C) == (u // C)) & (u <= t)).astype(np.float32)
    return dd, np.ascontiguousarray(dd[::-1, ::-1])


def _bdot(a, b):
    return _dot(a.astype(BF16), b.astype(BF16))


def _unit_tri_inverses(ms, eye, diag_blocks):
    each = lambda f, *ls: [f(*a) for a in zip(*ls)]
    mds = [jnp.where(diag_blocks, m, 0.0) for m in ms]
    mos = each(lambda m, md: m - md, ms, mds)
    xs = [eye - md for md in mds]
    ps = each(_bdot, mds, mds)
    for _ in range(2):
        xs = each(lambda x, p: x + _bdot(x, p), xs, ps)
        ps = each(_bdot, ps, ps)
    tds = each(lambda x, p: x + _bdot(x, p), xs, ps)
    gs = each(_bdot, tds, mos)
    g2s = each(_bdot, gs, gs)
    ys = each(lambda g, g2: (eye - g) + _bdot(eye - g, g2), gs, g2s)
    return each(_bdot, ys, tds)


def _rwkv_groups(r, v, kap, dirs, eye, diag_blocks, head_masks):
    G = r.shape[0]
    pre = []
    for kd, al, lw, dd, rev, causal_incl, causal_strict in dirs:
        l1, l2, _ = _split3(lw)
        b_in = _dot(dd, l1) + _dot(dd, l2)
        r_tot = 0 if rev else CHUNK - 1
        b_tot = jnp.concatenate([jnp.broadcast_to(b_in[s + r_tot:s + r_tot + 1], (CHUNK, b_in.shape[1]))
                                 for s in range(0, G, CHUNK)], axis=0)
        b_ex, b_out = b_in - lw, b_tot - b_in
        p_neg = jnp.exp(-b_in)
        p_out = jnp.exp(b_out)
        pre.append(dict(kap_h=kap * jnp.exp(b_ex), r_h=r * jnp.exp(b_in), al_n=(al * p_neg).astype(BF16),
                        kd_n=(kd * p_neg).astype(BF16), kout=kd * p_out, aout=al * p_out, p_tot=jnp.exp(b_tot),
                        incl=causal_incl, strict=causal_strict))
    chains = [(p, hm) for p in pre for hm in head_masks]
    kap_hb = [jnp.where(hm, p["kap_h"], 0.0).astype(BF16) for p, hm in chains]
    r_hb = [jnp.where(hm, p["r_h"], 0.0).astype(BF16) for p, hm in chains]
    v_h = [jnp.where(hm, v, 0.0).astype(BF16) for _, hm in chains]
    ms = [jnp.where(p["strict"], _dot_nt(k_, p["al_n"]), 0.0) for (p, _), k_ in zip(chains, kap_hb)]
    ns = [jnp.where(p["strict"], _dot_nt(k_, p["kd_n"]), 0.0) for (p, _), k_ in zip(chains, kap_hb)]
    ras = [jnp.where(p["incl"], _dot_nt(r_, p["al_n"]), 0.0) for (p, _), r_ in zip(chains, r_hb)]
    rks = [jnp.where(p["incl"], _dot_nt(r_, p["kd_n"]), 0.0) for (p, _), r_ in zip(chains, r_hb)]
    nvs = [_bdot(n, vh).astype(BF16) for n, vh in zip(ns, v_h)]
    y0s = [_bdot(rk, vh) for rk, vh in zip(rks, v_h)]
    racs = [sum(ra[:, s:s + CHUNK] for s in range(0, G, CHUNK)) for ra in ras]
    tinvs = [t.astype(BF16) for t in _unit_tri_inverses(ms, eye, diag_blocks)]
    kw_uv = [_dot(t, jnp.concatenate([k_, nv], axis=1)) for t, k_, nv in zip(tinvs, kap_hb, nvs)]
    out = []
    for d, p in enumerate(pre):
        a, b = kw_uv[2 * d], kw_uv[2 * d + 1]
        out.append((a[:, :LANE] + b[:, :LANE], p["r_h"], jnp.concatenate(racs[2 * d:2 * d + 2], axis=1),
                    p["kout"], p["aout"], a[:, LANE:] + b[:, LANE:], y0s[2 * d] + y0s[2 * d + 1], p["p_tot"]))
    return out


N_LOCAL = 8


def _rwkv_local_kernel(r_ref, v_ref, kap_ref, kdf_ref, kdb_ref, alf_ref, alb_ref, lwf_ref, lwb_ref,
                       ddf_ref, ddb_ref, *out_refs):
    G, C = RWKV_GROUP, CHUNK
    ti = lax.broadcasted_iota(jnp.int32, (G, G), 0)
    si = lax.broadcasted_iota(jnp.int32, (G, G), 1)
    same_chunk = (ti // C) == (si // C)
    eye = (ti == si).astype(F32)
    diag_blocks = (ti // 16) == (si // 16)
    lane = lax.broadcasted_iota(jnp.int32, (1, LANE), 1)
    head_masks = [lane < HEAD_A, lane >= HEAD_A]
    r, v, kap = r_ref[...], v_ref[...], kap_ref[...]
    dirs = []
    for d, (kd_ref, al_ref, lw_ref, dd_ref) in enumerate(((kdf_ref, alf_ref, lwf_ref, ddf_ref),
                                                          (kdb_ref, alb_ref, lwb_ref, ddb_ref))):
        incl = same_chunk & ((si <= ti) if d == 0 else (si >= ti))
        strict = same_chunk & ((si < ti) if d == 0 else (si > ti))
        dirs.append((kd_ref[...], al_ref[...], lw_ref[...], dd_ref[...], d == 1, incl, strict))
    for d, res in enumerate(_rwkv_groups(r, v, kap, dirs, eye, diag_blocks, head_masks)):
        outs = out_refs[d * N_LOCAL:(d + 1) * N_LOCAL]
        for o_ref, val in zip(outs[:-1], res[:-1]):
            o_ref[...] = val.astype(o_ref.dtype)
        p_tot = res[-1]
        outs[-1][...] = jnp.concatenate([p_tot[c * C:c * C + 8] for c in range(G // C)], axis=0)


def rwkv_local(r, v, kap, kd_f, kd_b, al_f, al_b, lw_f, lw_b):
    T, W = r.shape
    G = RWKV_GROUP
    ddf, ddb = _rwkv_constants()
    blk = pl.BlockSpec((G, LANE), lambda i, h: (i, h))
    pblk = pl.BlockSpec((G // 8, LANE), lambda i, h: (i, h))
    const = lambda a: pl.BlockSpec(a.shape, lambda i, h: (0, 0))
    b16 = jax.ShapeDtypeStruct((T, W), BF16)
    per_dir = (b16,) * 7 + (jax.ShapeDtypeStruct((T // 8, W), F32),)
    outs = pl.pallas_call(
        _rwkv_local_kernel,
        out_shape=per_dir * 2,
        grid=(T // G, W // LANE),
        in_specs=[blk] * 9 + [const(ddf), const(ddb)],
        out_specs=((blk,) * 7 + (pblk,)) * 2,
        compiler_params=_cparams("parallel", "parallel"),
        name="rwkv_local",
    )(r, v, kap, kd_f, kd_b, al_f, al_b, lw_f, lw_b,
      jnp.asarray(ddf, BF16), jnp.asarray(ddb, BF16))
    return outs[:N_LOCAL], outs[N_LOCAL:]


SCAN_PAIRS = 4


def _rwkv_scan_chunks(chains, head_masks, same_head):
    C = CHUNK
    m1s = [_dot_nt(jnp.concatenate([c[0], c[1]], axis=0), c[9].astype(BF16)) for c in chains]
    us = [m1[:C] + c[5] for m1, c in zip(m1s, chains)]
    incs = [_dot_tn(jnp.concatenate([c[7].astype(BF16), (-u).astype(BF16)], axis=0),
                    jnp.concatenate([c[3], c[4]], axis=0)) for u, c in zip(us, chains)]
    sts = [jnp.where(same_head, c[9] * c[8] + inc, 0.0) for c, inc in zip(chains, incs)]
    u_cats = [jnp.concatenate([jnp.where(hm, u, 0.0).astype(BF16) for hm in head_masks], axis=0) for u in us]
    ys = [m1[C:] + c[6] - _dot(c[2], u_cat) for m1, c, u_cat in zip(m1s, chains, u_cats)]
    return list(zip(ys, sts))


def _rwkv_scan_kernel(*refs, n_chunks, n_pairs):
    C = CHUNK
    f_refs, vf_ref = refs[:N_LOCAL], refs[N_LOCAL]
    b_refs, vb_ref = refs[N_LOCAL + 1:2 * N_LOCAL + 1], refs[2 * N_LOCAL + 1]
    yf_ref, yb_ref, stf_ref, stb_ref = refs[2 * N_LOCAL + 2:]

    @pl.when(pl.program_id(2) == 0)
    def _():
        stf_ref[...] = jnp.zeros_like(stf_ref)
        stb_ref[...] = jnp.zeros_like(stb_ref)

    lane = lax.broadcasted_iota(jnp.int32, (1, LANE), 1)
    head_masks = [lane < HEAD_A, lane >= HEAD_A]
    vi = lax.broadcasted_iota(jnp.int32, (LANE, LANE), 0)
    ki = lax.broadcasted_iota(jnp.int32, (LANE, LANE), 1)
    same_head = (vi // HEAD_A) == (ki // HEAD_A)

    def rows(ref, c, n):
        return ref[pl.ds(pl.multiple_of(c * n, n), n), :]

    def body(it, carry):
        states = [list(carry[0]), list(carry[1])]
        chains = []
        for d, (d_refs, v_ref) in enumerate(((f_refs, vf_ref), (b_refs, vb_ref))):
            c = it if d == 0 else n_chunks - 1 - it
            vals = [rows(ref, c, C) for ref in d_refs[:-1]] + [rows(v_ref, c, C), rows(d_refs[-1], c, 8)[0:1]]
            for p in range(n_pairs):
                chains.append(tuple(a[:, p * LANE:(p + 1) * LANE] for a in vals) + (states[d][p],))
        res = _rwkv_scan_chunks(chains, head_masks, same_head)
        for d, y_ref in enumerate((yf_ref, yb_ref)):
            c = it if d == 0 else n_chunks - 1 - it
            y_ref[pl.ds(pl.multiple_of(c * C, C), C), :] = jnp.concatenate(
                [res[d * n_pairs + p][0] for p in range(n_pairs)], axis=1)
            states[d] = [res[d * n_pairs + p][1] for p in range(n_pairs)]
        return tuple(states[0]), tuple(states[1])

    init = tuple(tuple(ref[p] for p in range(n_pairs)) for ref in (stf_ref, stb_ref))
    st_f, st_b = lax.fori_loop(0, n_chunks, body, init)
    for p in range(n_pairs):
        stf_ref[p] = st_f[p]
        stb_ref[p] = st_b[p]


def rwkv_scan(loc_f, loc_b, v, batch, *, ts=512):
    T, W = v.shape
    S = T // batch
    ts = min(ts, S)
    ns = S // ts
    bw = LANE * SCAN_PAIRS
    fwd = lambda rows: pl.BlockSpec((rows, bw), lambda b, h, s: (b * ns + s, h))
    bwd = lambda rows: pl.BlockSpec((rows, bw), lambda b, h, s: (b * ns + ns - 1 - s, h))
    specs = lambda mk: [mk(ts)] * (N_LOCAL - 1) + [mk(ts // 8), mk(ts)]
    out = jax.ShapeDtypeStruct((T, W), F32)
    state = pltpu.VMEM((SCAN_PAIRS, LANE, LANE), F32)
    return pl.pallas_call(
        functools.partial(_rwkv_scan_kernel, n_chunks=ts // CHUNK, n_pairs=SCAN_PAIRS),
        out_shape=(out, out),
        grid=(batch, W // bw, ns),
        in_specs=specs(fwd) + specs(bwd),
        out_specs=(fwd(ts), bwd(ts)),
        scratch_shapes=[state, state],
        compiler_params=_cparams("parallel", "parallel", "arbitrary"),
        name="rwkv_scan",
    )(*loc_f, v, *loc_b, v)


def _rwkv_out_kernel(yf_ref, yb_ref, bon_ref, g_ref, gw_ref, gb_ref, bd_ref, o_ref, *, width):
    bd = bd_ref[...]
    inv_n = 1.0 / HEAD_A
    for c in range(width // LANE):
        sl = slice(c * LANE, (c + 1) * LANE)
        y = yf_ref[:, sl] + yb_ref[:, sl]
        mean = _head_sum(y, bd) * inv_n
        d = y - mean
        var = _head_sum(d * d, bd) * inv_n
        yn = d * lax.rsqrt(var + GN_EPS) * gw_ref[:, sl] + gb_ref[:, sl] + bon_ref[:, sl]
        o_ref[:, sl] = (yn * g_ref[:, sl]).astype(o_ref.dtype)


def rwkv_out(y_f, y_b, bonus, g, gn_w, gn_b, *, tm=512):
    T, W = y_f.shape
    bd = np.kron(np.eye(LANE // HEAD_A, dtype=np.float32), np.ones((HEAD_A, HEAD_A), np.float32))
    row = pl.BlockSpec((tm, W), lambda i: (i, 0))
    vec = pl.BlockSpec((1, W), lambda i: (0, 0))
    return pl.pallas_call(
        functools.partial(_rwkv_out_kernel, width=W),
        out_shape=jax.ShapeDtypeStruct((T, W), BF16),
        grid=(T // tm,),
        in_specs=[row, row, row, row, vec, vec, pl.BlockSpec((LANE, LANE), lambda i: (0, 0))],
        out_specs=row,
        compiler_params=_cparams("parallel"),
        name="rwkv_out",
    )(y_f, y_b, bonus, g, gn_w.reshape(1, W), gn_b.reshape(1, W), jnp.asarray(bd, BF16))


def rwkv7_mixer(z, mu, w0, w2, a0, a2, g2, k_k, k_a, r_k, gn_w, gn_b, batch):
    r, v, kap, kd_f, kd_b, al_f, al_b, lw_f, lw_b, g, bonus = rwkv_prep(
        z, mu, w0, w2, a0, a2, g2, k_k, k_a, r_k.reshape(-1), batch)
    loc_f, loc_b = rwkv_local(r, v, kap, kd_f, kd_b, al_f, al_b, lw_f, lw_b)
    y_f, y_b = rwkv_scan(loc_f, loc_b, v, batch)
    return rwkv_out(y_f, y_b, bonus, g, gn_w, gn_b)


def _relu2(acc):
    r = jnp.maximum(acc, 0.0)
    return r * r


def _add(acc, res):
    return acc + res


def _pad_cols(w, mult):
    return jnp.pad(w, ((0, 0), (0, (-w.shape[1]) % mult)))


def kernel(x, p, positions, ln1_g, w_in, rwkv_mu, rwkv_w0, rwkv_w2, rwkv_a0, rwkv_a2, rwkv_g2, rwkv_kk, rwkv_ka, rwkv_rk, rwkv_gn_w, rwkv_gn_b, hgrn_lb, hgrn_norm_g, mla_q_norm_g, mla_kv_norm_g, mla_w_uq, mla_w_ukv, w_branch, w_o, ln2_g, w_mlp1, w_mlp2, w_pe, w_pg, final_g):
    Bn, S, D = x.shape
    L = w_in.shape[0]
    T = Bn * S
    W = rwkv_w0.shape[-1]
    q_lora, kv_lora = mla_q_norm_g.shape[-1], mla_kv_norm_g.shape[-1]
    n_heads_c = mla_w_ukv.shape[-1] // (QK_NOPE + V_HEAD)
    rwkv_w = 3 * W + 2 * LORA_DECAY + 2 * LORA_AAA + LORA_GATE
    hgrn_w = 5 * W
    o_hgrn = rwkv_w
    o_cq = o_hgrn + hgrn_w
    o_ckv = o_cq + q_lora
    o_kr = o_ckv + kv_lora
    o_gate = o_kr + QK_ROPE
    half = QK_ROPE // 2

    lb_w = jax.nn.softmax(hgrn_lb.astype(F32), axis=0)
    lower_bounds = jnp.cumsum(lb_w, axis=0) - lb_w[0]
    cq_tab, sq_tab, ck_tab, sk_tab = _rope_tables(positions)

    h = x.reshape(T, D)
    hn = rmsnorm(h, ln1_g[0], BF16)
    for l in range(L):
        wl = w_in[l]
        w_rwkv = _pad_cols(wl[:, :rwkv_w], 512).astype(BF16)
        w_hgrn = wl[:, o_hgrn:o_cq].astype(BF16)
        w_cq = wl[:, o_cq:o_ckv].astype(BF16)
        w_kr = wl[:, o_kr:o_gate]
        w_kv = jnp.concatenate(
            [wl[:, o_ckv:o_kr], _pad_cols(w_kr, LANE),
             _pad_cols(jnp.concatenate([w_kr[:, half:], w_kr[:, :half]], axis=1), LANE)], axis=1).astype(BF16)
        w_gate = wl[:, o_gate:].astype(BF16)

        z_rwkv = matmul(hn, w_rwkv, out_dtype=F32, tm=1024, tn=w_rwkv.shape[1] // 2, name="in_rwkv")
        z_hgrn = matmul(hn, w_hgrn, out_dtype=F32, tm=1024, tn=1024, name="in_hgrn")
        z_cq = matmul(hn, w_cq, out_dtype=F32, tm=1024, tn=w_cq.shape[1], name="in_cq")
        z_kv = matmul(hn, w_kv, out_dtype=F32, tm=1024, tn=w_kv.shape[1], name="in_kv")
        z_gate = matmul(hn, w_gate, out_dtype=BF16, tm=1024, tn=1024, name="in_gate")

        y_a = rwkv7_mixer(z_rwkv, rwkv_mu[l], rwkv_w0[l], rwkv_w2[l], rwkv_a0[l], rwkv_a2[l], rwkv_g2[l],
                          rwkv_kk[l], rwkv_ka[l], rwkv_rk[l], rwkv_gn_w[l], rwkv_gn_b[l], Bn)
        y_b = hgrn2_mixer(z_hgrn, lower_bounds[l], hgrn_norm_g[l], Bn)
        q = mla_q_proj(z_cq, mla_q_norm_g[l], _mla_q_weights(mla_w_uq[l], n_heads_c), cq_tab, sq_tab,
                       n_heads_c, Q_SCALE)
        k, vt = mla_kv_proj(z_kv, mla_kv_norm_g[l], mla_w_ukv[l].astype(BF16), ck_tab, sk_tab, n_heads_c, Bn)
        y_c = mla_attention(q, k, vt, Bn, n_heads_c)

        mixed = branch_mix(y_a, y_b, y_c, w_branch[l].astype(BF16), z_gate)
        h, hn2 = out_proj_norm(mixed, w_o[l].astype(BF16), h, ln2_g[l])
        hid = matmul(hn2, w_mlp1[l].astype(BF16), out_dtype=BF16, tm=1024, tn=1024, epilogue=_relu2,
                     name="mlp1")
        h = matmul(hid, w_mlp2[l].astype(BF16), out_dtype=F32, tm=1024, tn=1024, tk=2048, epilogue=_add,
                   extras=(h,), alias_extra=0, name="mlp2")
        last = l == L - 1
        h, hn = ple_update_norm(h, w_pg[l].astype(BF16), p[l].reshape(T, -1), w_pe[l].astype(BF16),
                                final_g if last else ln1_g[l + 1], F32 if last else BF16)
    return hn.reshape(Bn, S, D)
```

```python
import functools

import numpy as np
import jax
import jax.numpy as jnp
from jax import lax
from jax.experimental import pallas as pl
from jax.experimental.pallas import tpu as pltpu

F32 = jnp.float32
BF16 = jnp.bfloat16

LANE = 128
VMEM_LIMIT = 48 * 2**20

HEAD_A = 64
LORA_DECAY = 64
LORA_AAA = 64
LORA_GATE = 160
DECAY_SCALE = 0.606531
GN_EPS = 64e-5
HEAD_B = 128
F_TINY = 1e-30
QK_NOPE = 128
QK_ROPE = 64
V_HEAD = 128
ROPE_THETA = 10000.0
NORM_EPS = 1e-6
CHUNK = 64
LOG2_E = 1.4426950408889634
Q_SCALE = (QK_NOPE + QK_ROPE) ** -0.5 * LOG2_E


def _cparams(*sem, flags=None):
    return pltpu.CompilerParams(dimension_semantics=sem, vmem_limit_bytes=VMEM_LIMIT, flags=flags)


def _sigmoid(x):
    return 1.0 / (1.0 + jnp.exp(-x))


def _dot(a, b):
    return jnp.dot(a, b, preferred_element_type=F32)


def _dot_nt(a, b):
    return lax.dot_general(a, b, (((1,), (1,)), ((), ())), preferred_element_type=F32)


def _dot_tn(a, b):
    return lax.dot_general(a, b, (((0,), (0,)), ((), ())), preferred_element_type=F32)


def _split3(x):
    x1 = x.astype(BF16)
    r1 = x - x1.astype(F32)
    x2 = r1.astype(BF16)
    x3 = (r1 - x2.astype(F32)).astype(BF16)
    return x1, x2, x3


def _rmsnorm_kernel(x_ref, g_ref, o_ref):
    x = x_ref[...]
    ms = jnp.mean(x * x, axis=-1, keepdims=True)
    o_ref[...] = (x * lax.rsqrt(ms + NORM_EPS) * g_ref[...]).astype(o_ref.dtype)


def rmsnorm(x, g, out_dtype, tm=512):
    T, D = x.shape
    return pl.pallas_call(
        _rmsnorm_kernel,
        out_shape=jax.ShapeDtypeStruct((T, D), out_dtype),
        grid=(T // tm,),
        in_specs=[pl.BlockSpec((tm, D), lambda i: (i, 0)),
                  pl.BlockSpec((1, D), lambda i: (0, 0))],
        out_specs=pl.BlockSpec((tm, D), lambda i: (i, 0)),
        compiler_params=_cparams("parallel"),
        name="rmsnorm",
    )(x, g.reshape(1, D))


def _mm_kernel(a_ref, w_ref, *rest, nk, epilogue, n_extra):
    extras = rest[:n_extra]
    o_ref = rest[n_extra]

    def finish(acc):
        o_ref[...] = epilogue(acc, *[e[...] for e in extras]).astype(o_ref.dtype)

    if nk == 1:
        finish(_dot(a_ref[...], w_ref[...]))
    else:
        acc_ref = rest[n_extra + 1]
        k = pl.program_id(2)

        @pl.when(k == 0)
        def _():
            acc_ref[...] = jnp.zeros_like(acc_ref)

        acc_ref[...] += _dot(a_ref[...], w_ref[...])

        @pl.when(k == nk - 1)
        def _():
            finish(acc_ref[...])


def matmul(a, w, layer, *, out_dtype, tm, tn, tk=None, col0=0, n=None, epilogue=None, extras=(),
           alias_extra=None, name="matmul"):
    M, K = a.shape
    N = w.shape[2] - col0 if n is None else n
    tk = K if tk is None else tk
    nk = K // tk
    j0 = col0 // tn
    epilogue = epilogue or (lambda acc: acc)
    kern = functools.partial(_mm_kernel, nk=nk, epilogue=epilogue, n_extra=len(extras))
    in_specs = [pl.BlockSpec((tm, tk), lambda i, j, k: (i, k)),
                pl.BlockSpec((None, tk, tn), lambda i, j, k: (layer, k, j0 + j))]
    in_specs += [pl.BlockSpec((tm, tn), lambda i, j, k: (i, j)) for _ in extras]
    aliases = {} if alias_extra is None else {2 + alias_extra: 0}
    return pl.pallas_call(
        kern,
        out_shape=jax.ShapeDtypeStruct((M, N), out_dtype),
        grid=(M // tm, N // tn, nk),
        in_specs=in_specs,
        out_specs=pl.BlockSpec((tm, tn), lambda i, j, k: (i, j)),
        scratch_shapes=[pltpu.VMEM((tm, tn), F32)] if nk > 1 else [],
        input_output_aliases=aliases,
        compiler_params=_cparams("parallel", "parallel", "arbitrary"),
        name=name,
    )(a, w, *extras)


def _branch_kernel(ya_ref, yb_ref, yc_ref, p_ref, ga_ref, gb_ref, gc_ref, o_ref):
    acc = _sigmoid(ga_ref[...].astype(F32)) * _dot(ya_ref[...], p_ref[0])
    acc += _sigmoid(gb_ref[...].astype(F32)) * _dot(yb_ref[...], p_ref[1])
    acc += _sigmoid(gc_ref[...].astype(F32)) * _dot(yc_ref[...], p_ref[2])
    o_ref[...] = acc.astype(o_ref.dtype)


def branch_mix(ya, yb, yc, p, layer, zg, *, tm=1024, tn=512):
    T, W = ya.shape
    D = p.shape[3]
    nj = D // tn
    y_spec = pl.BlockSpec((tm, W), lambda i, j: (i, 0))
    g_specs = [pl.BlockSpec((tm, tn), functools.partial(lambda i, j, n: (i, n * nj + j), n=n))
               for n in range(3)]
    return pl.pallas_call(
        _branch_kernel,
        out_shape=jax.ShapeDtypeStruct((T, D), BF16),
        grid=(T // tm, nj),
        in_specs=[y_spec, y_spec, y_spec,
                  pl.BlockSpec((None, 3, W, tn), lambda i, j: (layer, 0, 0, j))] + g_specs,
        out_specs=pl.BlockSpec((tm, tn), lambda i, j: (i, j)),
        compiler_params=_cparams("parallel", "parallel"),
        name="branch_mix",
    )(ya, yb, yc, p, zg, zg, zg)


def _rms_rows(h, g):
    return h * lax.rsqrt(jnp.mean(h * h, axis=-1, keepdims=True) + NORM_EPS) * g


def _wo_ln_kernel(a_ref, w_ref, h_ref, g_ref, h_out_ref, hn_ref):
    h = h_ref[...] + _dot(a_ref[...], w_ref[...])
    h_out_ref[...] = h
    hn_ref[...] = _rms_rows(h, g_ref[...]).astype(hn_ref.dtype)


def out_proj_norm(a, w, layer, h, g, *, tm=512):
    T, D = h.shape
    K = a.shape[1]
    row = lambda width: pl.BlockSpec((tm, width), lambda i: (i, 0))
    return pl.pallas_call(
        _wo_ln_kernel,
        out_shape=(jax.ShapeDtypeStruct((T, D), F32), jax.ShapeDtypeStruct((T, D), BF16)),
        grid=(T // tm,),
        in_specs=[row(K), pl.BlockSpec((None, K, D), lambda i: (layer, 0, 0)), row(D),
                  pl.BlockSpec((1, D), lambda i: (0, 0))],
        out_specs=(row(D), row(D)),
        input_output_aliases={2: 0},
        compiler_params=_cparams("parallel"),
        name="out_proj_norm",
    )(a, w, h, g.reshape(1, D))


def _ple_ln_kernel(h_ref, wpg_ref, p_ref, wpe_ref, g_ref, h_out_ref, hn_ref):
    h = h_ref[...]
    gate = _sigmoid(_dot(h.astype(BF16), wpg_ref[...]))
    h = h + gate * _dot(p_ref[...].astype(BF16), wpe_ref[...])
    h_out_ref[...] = h
    hn_ref[...] = _rms_rows(h, g_ref[...]).astype(hn_ref.dtype)


def ple_update_norm(h, wpg, p, wpe, layer, g, norm_dtype, *, tm=512):
    T, D = h.shape
    E = p.shape[2]
    row = lambda width: pl.BlockSpec((tm, width), lambda i: (i, 0))
    stacked = lambda r, c: pl.BlockSpec((None, r, c), lambda i: (layer, 0, 0))
    return pl.pallas_call(
        _ple_ln_kernel,
        out_shape=(jax.ShapeDtypeStruct((T, D), F32), jax.ShapeDtypeStruct((T, D), norm_dtype)),
        grid=(T // tm,),
        in_specs=[row(D), stacked(D, D), pl.BlockSpec((None, tm, E), lambda i: (layer, i, 0)), stacked(E, D),
                  pl.BlockSpec((1, D), lambda i: (0, 0))],
        out_specs=(row(D), row(D)),
        input_output_aliases={0: 0},
        compiler_params=_cparams("parallel"),
        name="ple_update_norm",
    )(h, wpg, p, wpe, g.reshape(1, D))


def _mla_q_kernel(cq_ref, g_ref, w_ref, c_ref, s_ref, o_ref, *, scale, n_heads):
    xn = _rms_rows(cq_ref[...], g_ref[...]).astype(BF16)
    x = _dot(xn, w_ref[...])
    c, s = c_ref[...], s_ref[...]
    for h in range(n_heads):
        lo = 2 * LANE * h
        hi = x[:, lo + QK_NOPE:lo + 2 * LANE]
        o_ref[:, lo:lo + QK_NOPE] = (x[:, lo:lo + QK_NOPE] * scale).astype(BF16)
        o_ref[:, lo + QK_NOPE:lo + 2 * LANE] = (hi * c + pltpu.roll(hi, LANE // 2, 1) * s).astype(BF16)


def mla_q_proj(z, g, w, ctab, stab, n_heads, scale, *, tm=512):
    T, R = z.shape[0], g.shape[-1]
    N = w.shape[1]
    cq = z
    return pl.pallas_call(
        functools.partial(_mla_q_kernel, scale=scale, n_heads=n_heads),
        out_shape=jax.ShapeDtypeStruct((T, N), BF16),
        grid=(T // tm,),
        in_specs=[pl.BlockSpec((tm, R), lambda i: (i, 0)),
                  pl.BlockSpec((1, R), lambda i: (0, 0)),
                  pl.BlockSpec((R, N), lambda i: (0, 0)),
                  pl.BlockSpec((tm, LANE), lambda i: (i, 0)),
                  pl.BlockSpec((tm, LANE), lambda i: (i, 0))],
        out_specs=pl.BlockSpec((tm, N), lambda i: (i, 0)),
        compiler_params=_cparams("parallel"),
        name="mla_q_proj",
    )(cq, g.reshape(1, R), w, ctab, stab)


def _mla_kv_kernel(z_ref, g_ref, w_ref, c_ref, s_ref, k_ref, v_ref, *, kv_lora, n_heads):
    xn = _rms_rows(z_ref[:, :kv_lora], g_ref[...]).astype(BF16)
    kv = _dot(xn, w_ref[...])
    kr = (z_ref[:, kv_lora:kv_lora + LANE] * c_ref[...] + z_ref[:, kv_lora + LANE:] * s_ref[...]).astype(BF16)
    for h in range(n_heads):
        lo = 2 * LANE * h
        k_ref[:, lo:lo + QK_NOPE] = kv[:, lo:lo + QK_NOPE].astype(BF16)
        k_ref[:, lo + QK_NOPE:lo + 2 * LANE] = kr
        v_ref[h * V_HEAD:(h + 1) * V_HEAD, :] = kv[:, lo + QK_NOPE:lo + 2 * LANE].T.astype(BF16)


def mla_kv_proj(zkv, col0, g, w, ctab, stab, n_heads, batch, *, tm=512):
    T = zkv.shape[0]
    R = g.shape[-1]
    ZW = R + 2 * LANE
    cb = col0 // ZW
    S = T // batch
    tm = min(tm, S)
    nsb = S // tm
    N = w.shape[1]
    return pl.pallas_call(
        functools.partial(_mla_kv_kernel, kv_lora=R, n_heads=n_heads),
        out_shape=(jax.ShapeDtypeStruct((T, n_heads * 2 * LANE), BF16),
                   jax.ShapeDtypeStruct((batch * n_heads * V_HEAD, S), BF16)),
        grid=(T // tm,),
        in_specs=[pl.BlockSpec((tm, ZW), lambda i: (i, cb)),
                  pl.BlockSpec((1, R), lambda i: (0, 0)),
                  pl.BlockSpec((R, N), lambda i: (0, 0)),
                  pl.BlockSpec((tm, LANE), lambda i: (i, 0)),
                  pl.BlockSpec((tm, LANE), lambda i: (i, 0))],
        out_specs=(pl.BlockSpec((tm, n_heads * 2 * LANE), lambda i: (i, 0)),
                   pl.BlockSpec((n_heads * V_HEAD, tm), lambda i: (i // nsb, i % nsb))),
        compiler_params=_cparams("parallel"),
        name="mla_kv_proj",
    )(zkv, g.reshape(1, R), w, ctab, stab)


ATTN_TILES = 4


def _attn_kernel(q_ref, k_ref, vt_ref, o_ref, *, tq):
    k = k_ref[...]
    tiles = [slice(j * tq, (j + 1) * tq) for j in range(ATTN_TILES)]
    scores = lambda t: _dot_nt(k, q_ref[t, :])

    def finish(t, st):
        p = jnp.exp2(st - jnp.max(st, axis=0, keepdims=True))
        ot = _dot(vt_ref[...], p.astype(BF16))
        o_ref[t, :] = (ot / jnp.sum(p, axis=0, keepdims=True)).T.astype(o_ref.dtype)

    st = scores(tiles[0])
    for j, t in enumerate(tiles):
        st_next = scores(tiles[j + 1]) if j + 1 < len(tiles) else None
        finish(t, st)
        st = st_next


def mla_attention(q, k, vt, batch, n_heads, *, tq=256):
    T = q.shape[0]
    S = T // batch
    tb = tq * ATTN_TILES
    nq = S // tb
    QW = q.shape[1] // n_heads
    return pl.pallas_call(
        functools.partial(_attn_kernel, tq=tq),
        out_shape=jax.ShapeDtypeStruct((T, n_heads * V_HEAD), BF16),
        grid=(batch, n_heads, nq),
        in_specs=[pl.BlockSpec((tb, QW), lambda b, h, i: (b * nq + i, h)),
                  pl.BlockSpec((S, QW), lambda b, h, i: (b, h)),
                  pl.BlockSpec((V_HEAD, S), lambda b, h, i: (b * n_heads + h, 0))],
        out_specs=pl.BlockSpec((tb, V_HEAD), lambda b, h, i: (b * nq + i, h)),
        compiler_params=_cparams("parallel", "parallel", "arbitrary"),
        name="mla_attention",
    )(q, k, vt)


def _rope_tables(positions):
    inv_freq = 1.0 / (ROPE_THETA ** (jnp.arange(0, QK_ROPE, 2, dtype=F32) / QK_ROPE))
    ang = positions.astype(F32).reshape(-1, 1) * inv_freq
    cos, sin = jnp.cos(ang), jnp.sin(ang)
    T = ang.shape[0]
    z64 = jnp.zeros((T, LANE - QK_ROPE), F32)
    ck = jnp.concatenate([cos, cos, z64], axis=1)
    sk = jnp.concatenate([-sin, sin, z64], axis=1)
    return ck * Q_SCALE, sk * Q_SCALE, ck, sk


def _mla_q_weights(w_uq, n_heads):
    R = w_uq.shape[0]
    w = w_uq.reshape(R, n_heads, QK_NOPE + QK_ROPE)
    half = QK_ROPE // 2
    w = jnp.concatenate([w, w[:, :, QK_NOPE + half:], w[:, :, QK_NOPE:QK_NOPE + half]], axis=2)
    return w.reshape(R, n_heads * 2 * LANE).astype(BF16)


N_LEVELS = 6
HGRN_BATCH = 4


def _gla_constants():
    C = CHUNK
    t = np.arange(C)[:, None]
    u = np.arange(C)[None, :]
    mk = np.zeros((N_LEVELS + 1, C, C), np.float32)
    for l in range(N_LEVELS):
        m = C >> (l + 1)
        mk[l] = (t // (2 * m) == u // (2 * m)) & ((t % (2 * m)) >= m) & ((u % (2 * m)) < m)
    mk[N_LEVELS] = t == u
    tri = (u <= t).astype(np.float32)
    to2d = lambda a: np.ascontiguousarray(a).reshape(-1, C)
    return tri, np.ascontiguousarray(tri[::-1, ::-1]), to2d(mk), to2d(mk[:, ::-1, ::-1])


def _level_exponent(b, g, m, rev, row):
    C = CHUNK
    if m >= 4:
        r = m if rev else m - 1
        b_mid = jnp.concatenate([jnp.broadcast_to(b[s + r:s + r + 1, :], (2 * m, b.shape[1]))
                                 for s in range(0, C, 2 * m)], axis=0)
        return -jnp.abs(b - b_mid)
    g_next = pltpu.roll(g, C - 1, 0)
    g_prev = pltpu.roll(g, 1, 0)
    if m == 2:
        p4 = row % 4
        if rev:
            return jnp.where(p4 == 3, g_prev, jnp.where(p4 == 2, 0.0, jnp.where(p4 == 1, g, g + g_next)))
        return jnp.where(p4 == 0, g_next, jnp.where(p4 == 1, 0.0, jnp.where(p4 == 2, g, g + g_prev)))
    return jnp.where((row % 2 == 1) != rev, g, 0.0)


def _gla_local(chunks):
    C = CHUNK
    row = lax.broadcasted_iota(jnp.int32, (C, 1), 0)
    gs = [c[3] * LOG2_E for c in chunks]
    parts = [_split3(g) for g in gs]
    bs = [_dot(c[4], p[0]) + _dot(c[4], p[1]) + _dot(c[4], p[2]) for c, p in zip(chunks, parts)]
    qbs = [c[0].astype(BF16) for c in chunks]
    kbs = [c[1].astype(BF16) for c in chunks]
    scores = [c[5][N_LEVELS * C:(N_LEVELS + 1) * C] * _dot_nt(qb, kb) for c, qb, kb in zip(chunks, qbs, kbs)]
    for l in range(N_LEVELS):
        pls = [jnp.exp2(_level_exponent(b, g, C >> (l + 1), c[6], row)).astype(BF16)
               for c, b, g in zip(chunks, bs, gs)]
        prods = [_dot_nt(qb * p, kb * p) for qb, kb, p in zip(qbs, kbs, pls)]
        scores = [s + c[5][l * C:(l + 1) * C] * a for s, c, a in zip(scores, chunks, prods)]
    out = []
    for c, b, s in zip(chunks, bs, scores):
        q, k, v, rev = c[0], c[1], c[2], c[6]
        b_tot = b[0:1] if rev else b[C - 1:C]
        vb = v.astype(BF16)
        out.append((_dot(s.astype(BF16), vb), (q * jnp.exp2(b)).astype(BF16),
                    _dot_tn(vb, (k * jnp.exp2(b_tot - b)).astype(BF16)), jnp.exp2(b_tot)))
    return out


def _hgrn_kernel(q_ref, zf_ref, zb_ref, i_ref, g_ref, lb_ref, ng_ref, trif_ref, trib_ref, mkf_ref, mkb_ref,
                 o_ref, of_ref, ob_ref, *, n_chunks):
    C = CHUNK
    lb = lb_ref[...]
    one_m_lb = 1.0 - lb
    trif, trib, mkf, mkb = trif_ref[...], trib_ref[...], mkf_ref[...], mkb_ref[...]

    def gate(zz):
        w = one_m_lb * _sigmoid(zz)
        return one_m_lb - w, jnp.log(jnp.maximum(lb + w, F_TINY))

    def load(ref, c):
        return ref[pl.ds(pl.multiple_of(c * C, C), C), :]

    def chunk_inputs(c, z_ref, tri, mk, rev):
        q = load(q_ref, c)
        k, g = gate(load(z_ref, c))
        return (q * _sigmoid(q), k, load(i_ref, c), g, tri, mk, rev)

    def body(it, carry):
        states = list(carry)
        ids = [[it * HGRN_BATCH + j for j in range(HGRN_BATCH)]]
        ids.append([n_chunks - 1 - c for c in ids[0]])
        loc = _gla_local([chunk_inputs(c, zf_ref, trif, mkf, False) for c in ids[0]]
                         + [chunk_inputs(c, zb_ref, trib, mkb, True) for c in ids[1]])
        for d, out_ref in enumerate((of_ref, ob_ref)):
            for j, c in enumerate(ids[d]):
                o_intra, q_dec, st_inc, p_tot = loc[d * HGRN_BATCH + j]
                out_ref[pl.ds(pl.multiple_of(c * C, C), C), :] = o_intra + _dot_nt(q_dec, states[d].astype(BF16))
                states[d] = states[d] * p_tot + st_inc
        return tuple(states)

    dv, dk = i_ref.shape[1], q_ref.shape[1]
    z = jnp.zeros((dv, dk), F32)
    lax.fori_loop(0, n_chunks // HGRN_BATCH, body, (z, z))

    o = of_ref[...] + ob_ref[...]
    ms = jnp.mean(o * o, axis=-1, keepdims=True)
    gg = g_ref[...]
    o_ref[...] = (o * lax.rsqrt(ms + NORM_EPS) * ng_ref[...] * (gg * _sigmoid(gg))).astype(o_ref.dtype)


def hgrn2_mixer(z, lb, norm_g, batch, *, col0=0):
    T = z.shape[0]
    W = lb.shape[-1]
    S = T // batch
    H = W // HEAD_B
    c0 = col0 // HEAD_B
    trif, trib, mkf, mkb = _gla_constants()
    part = lambda n: pl.BlockSpec((S, HEAD_B), functools.partial(lambda b, h, n: (b, c0 + n * H + h), n=n))
    const = lambda a: pl.BlockSpec(a.shape, lambda b, h: (0, 0))
    return pl.pallas_call(
        functools.partial(_hgrn_kernel, n_chunks=S // CHUNK),
        out_shape=jax.ShapeDtypeStruct((T, W), BF16),
        grid=(batch, H),
        in_specs=[part(0), part(1), part(2), part(3), part(4),
                  pl.BlockSpec((1, HEAD_B), lambda b, h: (0, h)),
                  pl.BlockSpec((1, HEAD_B), lambda b, h: (0, 0)),
                  const(trif), const(trib), const(mkf), const(mkb)],
        out_specs=pl.BlockSpec((S, HEAD_B), lambda b, h: (b, h)),
        scratch_shapes=[pltpu.VMEM((S, HEAD_B), F32), pltpu.VMEM((S, HEAD_B), F32)],
        compiler_params=_cparams("parallel", "parallel"),
        name="hgrn2_mixer",
    )(z, z, z, z, z, lb.reshape(1, W), norm_g.reshape(1, HEAD_B),
      jnp.asarray(trif, BF16), jnp.asarray(trib, BF16), jnp.asarray(mkf), jnp.asarray(mkb))


def _head_sum(x, bd):
    x1, x2, _ = _split3(x)
    return _dot(x1, bd) + _dot(x2, bd)


def _rwkv_prep_kernel(x_ref, xp_ref, xn_ref, mu_ref, w0_ref, w2_ref, a0_ref, a2_ref, g2_ref, kk_ref,
                      ka_ref, rk_ref, bd_ref,
                      r_ref, v_ref, kap_ref, kdf_ref, kdb_ref, alf_ref, alb_ref, lwf_ref, lwb_ref,
                      g_ref, bon_ref, *, tm, seq, width):
    W = width
    i = pl.program_id(0)
    x = x_ref[...]
    at_start = (i * tm) % seq == 0
    at_end = ((i + 1) * tm) % seq == 0
    prev_row = jnp.where(at_start, 0.0, xp_ref[7:8, :])
    next_row = jnp.where(at_end, 0.0, xn_ref[0:1, :])
    row = lax.broadcasted_iota(jnp.int32, (8, 1), 0)
    x_prev = pltpu.roll(x, 1, 0)
    x_prev = jnp.concatenate([jnp.where(row == 0, prev_row, x_prev[:8]), x_prev[8:]], axis=0)
    x_next = pltpu.roll(x, tm - 1, 0)
    x_next = jnp.concatenate([x_next[:tm - 8], jnp.where(row == 7, next_row, x_next[tm - 8:])], axis=0)
    u = x + mu_ref[0:1, :] * (x_prev - x) + mu_ref[1:2, :] * (x_next - x)

    r, k, v = u[:, :W], u[:, W:2 * W], u[:, 2 * W:3 * W]
    o = 3 * W
    wd = jnp.tanh(u[:, o:o + 2 * LORA_DECAY]).astype(BF16)
    o += 2 * LORA_DECAY
    ad = u[:, o:o + 2 * LORA_AAA].astype(BF16)
    o += 2 * LORA_AAA
    gd = _sigmoid(u[:, o:]).astype(BF16)

    bd = bd_ref[...]
    kkr = k * kk_ref[...]
    sq = kkr * kkr
    rks = []
    a_dir = []
    for n in range(2):
        w_raw = w0_ref[n:n + 1, :] + _dot(wd[:, n * LORA_DECAY:(n + 1) * LORA_DECAY], w2_ref[n])
        lw = -DECAY_SCALE * _sigmoid(w_raw)
        a = _sigmoid(a0_ref[n:n + 1, :] + _dot(ad[:, n * LORA_AAA:(n + 1) * LORA_AAA], a2_ref[n]))
        a_dir.append(a)
        (lwf_ref, lwb_ref)[n][...] = lw
    kd = [k * (1.0 + (a - 1.0) * ka_ref[...]) for a in a_dir]
    kdf_ref[...] = kd[0].astype(kdf_ref.dtype)
    kdb_ref[...] = kd[1].astype(kdb_ref.dtype)
    rkk = r * (kd[0] + kd[1]) * rk_ref[...]
    for c in range(W // LANE):
        sl = slice(c * LANE, (c + 1) * LANE)
        nrm = jnp.maximum(jnp.sqrt(_head_sum(sq[:, sl], bd)), 1e-12)
        kap = kkr[:, sl] / nrm
        kap_ref[:, sl] = kap.astype(kap_ref.dtype)
        alf_ref[:, sl] = (kap * a_dir[0][:, sl]).astype(alf_ref.dtype)
        alb_ref[:, sl] = (kap * a_dir[1][:, sl]).astype(alb_ref.dtype)
        bon_ref[:, sl] = _head_sum(rkk[:, sl], bd) * v[:, sl]
    r_ref[...] = r.astype(r_ref.dtype)
    v_ref[...] = v.astype(v_ref.dtype)
    g_ref[...] = _dot(gd, g2_ref[...]).astype(g_ref.dtype)


def rwkv_prep(z, mu, w0, w2, a0, a2, g2, k_k, k_a, r_k, batch, *, tm=256):
    T, ZW = z.shape
    W = w0.shape[-1]
    S = T // batch
    nb8 = tm // 8
    used = 3 * W + 2 * LORA_DECAY + 2 * LORA_AAA + LORA_GATE
    mu_p = jnp.pad(mu, ((0, 0), (0, ZW - used)))
    g2_p = jnp.pad(g2, ((0, ZW - used), (0, 0))).astype(BF16)
    bd = np.kron(np.eye(LANE // HEAD_A, dtype=np.float32), np.ones((HEAD_A, HEAD_A), np.float32))
    full = lambda a: pl.BlockSpec(a.shape, lambda i: (0,) * a.ndim)
    vec = lambda a: a.reshape(1, W)
    args = [z, z, z, mu_p, w0, w2.astype(BF16), a0, a2.astype(BF16), g2_p, vec(k_k), vec(k_a),
            vec(r_k), jnp.asarray(bd, BF16)]
    in_specs = [pl.BlockSpec((tm, ZW), lambda i: (i, 0)),
                pl.BlockSpec((8, ZW), lambda i: (jnp.maximum(i * nb8 - 1, 0), 0)),
                pl.BlockSpec((8, ZW), lambda i: (jnp.minimum((i + 1) * nb8, T // 8 - 1), 0))]
    in_specs += [full(a) for a in args[3:]]
    b16 = jax.ShapeDtypeStruct((T, W), BF16)
    f32 = jax.ShapeDtypeStruct((T, W), F32)
    return pl.pallas_call(
        functools.partial(_rwkv_prep_kernel, tm=tm, seq=S, width=W),
        out_shape=(b16,) * 7 + (f32, f32, b16, f32),
        grid=(T // tm,),
        in_specs=in_specs,
        out_specs=(pl.BlockSpec((tm, W), lambda i: (i, 0)),) * 11,
        compiler_params=_cparams("parallel"),
        name="rwkv_prep",
    )(*args)


RWKV_GROUP = 256


def _rwkv_constants():
    G, C = RWKV_GROUP, CHUNK
    t = np.arange(G)[:, None]
    u = np.arange(G)[None, :]
    dd = (((t // C) == (u // C)) & (u <= t)).astype(np.float32)
    return dd, np.ascontiguousarray(dd[::-1, ::-1])


def _bdot(a, b):
    return _dot(a.astype(BF16), b.astype(BF16))


def _unit_tri_inverses(ms, eye, diag_blocks):
    each = lambda f, *ls: [f(*a) for a in zip(*ls)]
    mds = [jnp.where(diag_blocks, m, 0.0) for m in ms]
    mos = each(lambda m, md: m - md, ms, mds)
    xs = [eye - md for md in mds]
    ps = each(_bdot, mds, mds)
    for _ in range(2):
        xs = each(lambda x, p: x + _bdot(x, p), xs, ps)
        ps = each(_bdot, ps, ps)
    tds = each(lambda x, p: x + _bdot(x, p), xs, ps)
    gs = each(_bdot, tds, mos)
    g2s = each(_bdot, gs, gs)
    ys = each(lambda g, g2: (eye - g) + _bdot(eye - g, g2), gs, g2s)
    return each(_bdot, ys, tds)


def _rwkv_groups(r, v, kap, dirs, eye, diag_blocks, head_masks):
    G = r.shape[0]
    pre = []
    for kd, al, lw, dd, rev, causal_incl, causal_strict in dirs:
        l1, l2, _ = _split3(lw)
        b_in = _dot(dd, l1) + _dot(dd, l2)
        r_tot = 0 if rev else CHUNK - 1
        b_tot = jnp.concatenate([jnp.broadcast_to(b_in[s + r_tot:s + r_tot + 1], (CHUNK, b_in.shape[1]))
                                 for s in range(0, G, CHUNK)], axis=0)
        b_ex, b_out = b_in - lw, b_tot - b_in
        p_neg = jnp.exp(-b_in)
        p_out = jnp.exp(b_out)
        pre.append(dict(kap_h=kap * jnp.exp(b_ex), r_h=r * jnp.exp(b_in), al_n=(al * p_neg).astype(BF16),
                        kd_n=(kd * p_neg).astype(BF16), kout=kd * p_out, aout=al * p_out, p_tot=jnp.exp(b_tot),
                        incl=causal_incl, strict=causal_strict))
    chains = [(p, hm) for p in pre for hm in head_masks]
    kap_hb = [jnp.where(hm, p["kap_h"], 0.0).astype(BF16) for p, hm in chains]
    r_hb = [jnp.where(hm, p["r_h"], 0.0).astype(BF16) for p, hm in chains]
    v_h = [jnp.where(hm, v, 0.0).astype(BF16) for _, hm in chains]
    ms = [jnp.where(p["strict"], _dot_nt(k_, p["al_n"]), 0.0) for (p, _), k_ in zip(chains, kap_hb)]
    ns = [jnp.where(p["strict"], _dot_nt(k_, p["kd_n"]), 0.0) for (p, _), k_ in zip(chains, kap_hb)]
    ras = [jnp.where(p["incl"], _dot_nt(r_, p["al_n"]), 0.0) for (p, _), r_ in zip(chains, r_hb)]
    rks = [jnp.where(p["incl"], _dot_nt(r_, p["kd_n"]), 0.0) for (p, _), r_ in zip(chains, r_hb)]
    nvs = [_bdot(n, vh).astype(BF16) for n, vh in zip(ns, v_h)]
    y0s = [_bdot(rk, vh) for rk, vh in zip(rks, v_h)]
    racs = [sum(ra[:, s:s + CHUNK] for s in range(0, G, CHUNK)) for ra in ras]
    tinvs = [t.astype(BF16) for t in _unit_tri_inverses(ms, eye, diag_blocks)]
    kw_uv = [_dot(t, jnp.concatenate([k_, nv], axis=1)) for t, k_, nv in zip(tinvs, kap_hb, nvs)]
    out = []
    for d, p in enumerate(pre):
        a, b = kw_uv[2 * d], kw_uv[2 * d + 1]
        out.append((a[:, :LANE] + b[:, :LANE], p["r_h"], jnp.concatenate(racs[2 * d:2 * d + 2], axis=1),
                    p["kout"], p["aout"], a[:, LANE:] + b[:, LANE:], y0s[2 * d] + y0s[2 * d + 1], p["p_tot"]))
    return out


N_LOCAL = 8


def _rwkv_local_kernel(r_ref, v_ref, kap_ref, kdf_ref, kdb_ref, alf_ref, alb_ref, lwf_ref, lwb_ref,
                       ddf_ref, ddb_ref, *out_refs):
    G, C = RWKV_GROUP, CHUNK
    ti = lax.broadcasted_iota(jnp.int32, (G, G), 0)
    si = lax.broadcasted_iota(jnp.int32, (G, G), 1)
    same_chunk = (ti // C) == (si // C)
    eye = (ti == si).astype(F32)
    diag_blocks = (ti // 16) == (si // 16)
    lane = lax.broadcasted_iota(jnp.int32, (1, LANE), 1)
    head_masks = [lane < HEAD_A, lane >= HEAD_A]
    r, v, kap = r_ref[...], v_ref[...], kap_ref[...]
    dirs = []
    for d, (kd_ref, al_ref, lw_ref, dd_ref) in enumerate(((kdf_ref, alf_ref, lwf_ref, ddf_ref),
                                                          (kdb_ref, alb_ref, lwb_ref, ddb_ref))):
        incl = same_chunk & ((si <= ti) if d == 0 else (si >= ti))
        strict = same_chunk & ((si < ti) if d == 0 else (si > ti))
        dirs.append((kd_ref[...], al_ref[...], lw_ref[...], dd_ref[...], d == 1, incl, strict))
    for d, res in enumerate(_rwkv_groups(r, v, kap, dirs, eye, diag_blocks, head_masks)):
        outs = out_refs[d * N_LOCAL:(d + 1) * N_LOCAL]
        for o_ref, val in zip(outs[:-1], res[:-1]):
            o_ref[...] = val.astype(o_ref.dtype)
        p_tot = res[-1]
        outs[-1][...] = jnp.concatenate([p_tot[c * C:c * C + 8] for c in range(G // C)], axis=0)


def rwkv_local(r, v, kap, kd_f, kd_b, al_f, al_b, lw_f, lw_b):
    T, W = r.shape
    G = RWKV_GROUP
    ddf, ddb = _rwkv_constants()
    blk = pl.BlockSpec((G, LANE), lambda i, h: (i, h))
    pblk = pl.BlockSpec((G // 8, LANE), lambda i, h: (i, h))
    const = lambda a: pl.BlockSpec(a.shape, lambda i, h: (0, 0))
    b16 = jax.ShapeDtypeStruct((T, W), BF16)
    per_dir = (b16,) * 7 + (jax.ShapeDtypeStruct((T // 8, W), F32),)
    outs = pl.pallas_call(
        _rwkv_local_kernel,
        out_shape=per_dir * 2,
        grid=(T // G, W // LANE),
        in_specs=[blk] * 9 + [const(ddf), const(ddb)],
        out_specs=((blk,) * 7 + (pblk,)) * 2,
        compiler_params=_cparams("parallel", "parallel"),
        name="rwkv_local",
    )(r, v, kap, kd_f, kd_b, al_f, al_b, lw_f, lw_b,
      jnp.asarray(ddf, BF16), jnp.asarray(ddb, BF16))
    return outs[:N_LOCAL], outs[N_LOCAL:]


SCAN_PAIRS = 4


def _rwkv_scan_chunks(chains, head_masks, same_head):
    C = CHUNK
    m1s = [_dot_nt(jnp.concatenate([c[0], c[1]], axis=0), c[9].astype(BF16)) for c in chains]
    us = [m1[:C] + c[5] for m1, c in zip(m1s, chains)]
    incs = [_dot_tn(jnp.concatenate([c[7].astype(BF16), (-u).astype(BF16)], axis=0),
                    jnp.concatenate([c[3], c[4]], axis=0)) for u, c in zip(us, chains)]
    sts = [jnp.where(same_head, c[9] * c[8] + inc, 0.0) for c, inc in zip(chains, incs)]
    u_cats = [jnp.concatenate([jnp.where(hm, u, 0.0).astype(BF16) for hm in head_masks], axis=0) for u in us]
    ys = [m1[C:] + c[6] - _dot(c[2], u_cat) for m1, c, u_cat in zip(m1s, chains, u_cats)]
    return list(zip(ys, sts))


def _rwkv_scan_kernel(*refs, n_chunks, n_pairs):
    C = CHUNK
    f_refs, vf_ref = refs[:N_LOCAL], refs[N_LOCAL]
    b_refs, vb_ref = refs[N_LOCAL + 1:2 * N_LOCAL + 1], refs[2 * N_LOCAL + 1]
    yf_ref, yb_ref, stf_ref, stb_ref = refs[2 * N_LOCAL + 2:]

    @pl.when(pl.program_id(2) == 0)
    def _():
        stf_ref[...] = jnp.zeros_like(stf_ref)
        stb_ref[...] = jnp.zeros_like(stb_ref)

    lane = lax.broadcasted_iota(jnp.int32, (1, LANE), 1)
    head_masks = [lane < HEAD_A, lane >= HEAD_A]
    vi = lax.broadcasted_iota(jnp.int32, (LANE, LANE), 0)
    ki = lax.broadcasted_iota(jnp.int32, (LANE, LANE), 1)
    same_head = (vi // HEAD_A) == (ki // HEAD_A)

    def rows(ref, c, n):
        return ref[pl.ds(pl.multiple_of(c * n, n), n), :]

    def body(it, carry):
        states = [list(carry[0]), list(carry[1])]
        chains = []
        for d, (d_refs, v_ref) in enumerate(((f_refs, vf_ref), (b_refs, vb_ref))):
            c = it if d == 0 else n_chunks - 1 - it
            vals = [rows(ref, c, C) for ref in d_refs[:-1]] + [rows(v_ref, c, C), rows(d_refs[-1], c, 8)[0:1]]
            for p in range(n_pairs):
                chains.append(tuple(a[:, p * LANE:(p + 1) * LANE] for a in vals) + (states[d][p],))
        res = _rwkv_scan_chunks(chains, head_masks, same_head)
        for d, y_ref in enumerate((yf_ref, yb_ref)):
            c = it if d == 0 else n_chunks - 1 - it
            y_ref[pl.ds(pl.multiple_of(c * C, C), C), :] = jnp.concatenate(
                [res[d * n_pairs + p][0] for p in range(n_pairs)], axis=1)
            states[d] = [res[d * n_pairs + p][1] for p in range(n_pairs)]
        return tuple(states[0]), tuple(states[1])

    init = tuple(tuple(ref[p] for p in range(n_pairs)) for ref in (stf_ref, stb_ref))
    st_f, st_b = lax.fori_loop(0, n_chunks, body, init)
    for p in range(n_pairs):
        stf_ref[p] = st_f[p]
        stb_ref[p] = st_b[p]


def rwkv_scan(loc_f, loc_b, v, batch, *, ts=512):
    T, W = v.shape
    S = T // batch
    ts = min(ts, S)
    ns = S // ts
    bw = LANE * SCAN_PAIRS
    fwd = lambda rows: pl.BlockSpec((rows, bw), lambda b, h, s: (b * ns + s, h))
    bwd = lambda rows: pl.BlockSpec((rows, bw), lambda b, h, s: (b * ns + ns - 1 - s, h))
    specs = lambda mk: [mk(ts)] * (N_LOCAL - 1) + [mk(ts // 8), mk(ts)]
    out = jax.ShapeDtypeStruct((T, W), F32)
    state = pltpu.VMEM((SCAN_PAIRS, LANE, LANE), F32)
    return pl.pallas_call(
        functools.partial(_rwkv_scan_kernel, n_chunks=ts // CHUNK, n_pairs=SCAN_PAIRS),
        out_shape=(out, out),
        grid=(batch, W // bw, ns),
        in_specs=specs(fwd) + specs(bwd),
        out_specs=(fwd(ts), bwd(ts)),
        scratch_shapes=[state, state],
        compiler_params=_cparams("parallel", "parallel", "arbitrary"),
        name="rwkv_scan",
    )(*loc_f, v, *loc_b, v)


def _rwkv_out_kernel(yf_ref, yb_ref, bon_ref, g_ref, gw_ref, gb_ref, bd_ref, o_ref, *, width):
    bd = bd_ref[...]
    inv_n = 1.0 / HEAD_A
    for c in range(width // LANE):
        sl = slice(c * LANE, (c + 1) * LANE)
        y = yf_ref[:, sl] + yb_ref[:, sl]
        mean = _head_sum(y, bd) * inv_n
        d = y - mean
        var = _head_sum(d * d, bd) * inv_n
        yn = d * lax.rsqrt(var + GN_EPS) * gw_ref[:, sl] + gb_ref[:, sl] + bon_ref[:, sl]
        o_ref[:, sl] = (yn * g_ref[:, sl]).astype(o_ref.dtype)


def rwkv_out(y_f, y_b, bonus, g, gn_w, gn_b, *, tm=512):
    T, W = y_f.shape
    bd = np.kron(np.eye(LANE // HEAD_A, dtype=np.float32), np.ones((HEAD_A, HEAD_A), np.float32))
    row = pl.BlockSpec((tm, W), lambda i: (i, 0))
    vec = pl.BlockSpec((1, W), lambda i: (0, 0))
    return pl.pallas_call(
        functools.partial(_rwkv_out_kernel, width=W),
        out_shape=jax.ShapeDtypeStruct((T, W), BF16),
        grid=(T // tm,),
        in_specs=[row, row, row, row, vec, vec, pl.BlockSpec((LANE, LANE), lambda i: (0, 0))],
        out_specs=row,
        compiler_params=_cparams("parallel"),
        name="rwkv_out",
    )(y_f, y_b, bonus, g, gn_w.reshape(1, W), gn_b.reshape(1, W), jnp.asarray(bd, BF16))


def rwkv7_mixer(z, mu, w0, w2, a0, a2, g2, k_k, k_a, r_k, gn_w, gn_b, batch):
    r, v, kap, kd_f, kd_b, al_f, al_b, lw_f, lw_b, g, bonus = rwkv_prep(
        z, mu, w0, w2, a0, a2, g2, k_k, k_a, r_k.reshape(-1), batch)
    loc_f, loc_b = rwkv_local(r, v, kap, kd_f, kd_b, al_f, al_b, lw_f, lw_b)
    y_f, y_b = rwkv_scan(loc_f, loc_b, v, batch)
    return rwkv_out(y_f, y_b, bonus, g, gn_w, gn_b)


def _relu2(acc):
    r = jnp.maximum(acc, 0.0)
    return r * r


def _add(acc, res):
    return acc + res


def kernel(x, p, positions, ln1_g, w_in, rwkv_mu, rwkv_w0, rwkv_w2, rwkv_a0, rwkv_a2, rwkv_g2, rwkv_kk, rwkv_ka, rwkv_rk, rwkv_gn_w, rwkv_gn_b, hgrn_lb, hgrn_norm_g, mla_q_norm_g, mla_kv_norm_g, mla_w_uq, mla_w_ukv, w_branch, w_o, ln2_g, w_mlp1, w_mlp2, w_pe, w_pg, final_g):
    Bn, S, D = x.shape
    L = w_in.shape[0]
    T = Bn * S
    W = rwkv_w0.shape[-1]
    q_lora, kv_lora = mla_q_norm_g.shape[-1], mla_kv_norm_g.shape[-1]
    n_heads_c = mla_w_ukv.shape[-1] // (QK_NOPE + V_HEAD)
    rwkv_w = 3 * W + 2 * LORA_DECAY + 2 * LORA_AAA + LORA_GATE
    hgrn_w = 5 * W
    o_hgrn = rwkv_w
    o_cq = o_hgrn + hgrn_w
    o_ckv = o_cq + q_lora
    o_kr = o_ckv + kv_lora
    o_gate = o_kr + QK_ROPE
    half = QK_ROPE // 2

    lb_w = jax.nn.softmax(hgrn_lb.astype(F32), axis=0)
    lower_bounds = jnp.cumsum(lb_w, axis=0) - lb_w[0]
    cq_tab, sq_tab, ck_tab, sk_tab = _rope_tables(positions)

    pad3 = lambda w, mult: jnp.pad(w, ((0, 0), (0, 0), (0, (-w.shape[2]) % mult)))
    w_kr = w_in[:, :, o_kr:o_gate]
    w_all = jnp.concatenate(
        [w_in[:, :, o_hgrn:o_cq], w_in[:, :, o_gate:], w_in[:, :, o_cq:o_kr], pad3(w_kr, LANE),
         pad3(jnp.concatenate([w_kr[:, :, half:], w_kr[:, :, :half]], axis=2), LANE),
         pad3(w_in[:, :, :rwkv_w], 512)], axis=2).astype(BF16)
    n_hgrn, n_gate = hgrn_w, w_in.shape[2] - o_gate
    n_ckv = q_lora + kv_lora + 2 * LANE
    c_gate, c_ckv, c_rwkv = n_hgrn, n_hgrn + n_gate, n_hgrn + n_gate + n_ckv
    w_branch_b, w_o_b = w_branch.astype(BF16), w_o.astype(BF16)
    w_mlp1_b, w_mlp2_b = w_mlp1.astype(BF16), w_mlp2.astype(BF16)
    w_pg_b, w_pe_b = w_pg.astype(BF16), w_pe.astype(BF16)
    p2 = p.reshape(L, T, -1)

    h = x.reshape(T, D)
    hn = rmsnorm(h, ln1_g[0], BF16)
    for l in range(L):
        z_hgrn = matmul(hn, w_all, l, out_dtype=F32, tm=1024, tn=1024, n=n_hgrn, name="in_hgrn")
        z_gate = matmul(hn, w_all, l, out_dtype=BF16, tm=1024, tn=1024, col0=c_gate, n=n_gate, name="in_gate")
        z_ckv = matmul(hn, w_all, l, out_dtype=F32, tm=1024, tn=512, col0=c_ckv, n=n_ckv, name="in_ckv")
        z_rwkv = matmul(hn, w_all, l, out_dtype=F32, tm=1024, tn=512, col0=c_rwkv, name="in_rwkv")

        y_a = rwkv7_mixer(z_rwkv, rwkv_mu[l], rwkv_w0[l], rwkv_w2[l], rwkv_a0[l], rwkv_a2[l], rwkv_g2[l],
                          rwkv_kk[l], rwkv_ka[l], rwkv_rk[l], rwkv_gn_w[l], rwkv_gn_b[l], Bn)
        y_b = hgrn2_mixer(z_hgrn, lower_bounds[l], hgrn_norm_g[l], Bn)
        q = mla_q_proj(z_ckv, mla_q_norm_g[l], _mla_q_weights(mla_w_uq[l], n_heads_c), cq_tab, sq_tab,
                       n_heads_c, Q_SCALE)
        k, vt = mla_kv_proj(z_ckv, q_lora, mla_kv_norm_g[l], mla_w_ukv[l].astype(BF16), ck_tab, sk_tab,
                            n_heads_c, Bn)
        y_c = mla_attention(q, k, vt, Bn, n_heads_c)

        mixed = branch_mix(y_a, y_b, y_c, w_branch_b, l, z_gate)
        h, hn2 = out_proj_norm(mixed, w_o_b, l, h, ln2_g[l])
        hid = matmul(hn2, w_mlp1_b, l, out_dtype=BF16, tm=1024, tn=1024, epilogue=_relu2, name="mlp1")
        h = matmul(hid, w_mlp2_b, l, out_dtype=F32, tm=1024, tn=1024, tk=2048, epilogue=_add,
                   extras=(h,), alias_extra=0, name="mlp2")
        last = l == L - 1
        h, hn = ple_update_norm(h, w_pg_b, p2, w_pe_b, l, final_g if last else ln1_g[l + 1],
                                F32 if last else BF16)
    return hn.reshape(Bn, S, D)
```

```python
import functools

import numpy as np
import jax
import jax.numpy as jnp
from jax import lax
from jax.experimental import pallas as pl
from jax.experimental.pallas import tpu as pltpu

F32 = jnp.float32
BF16 = jnp.bfloat16

LANE = 128
VMEM_LIMIT = 48 * 2**20

HEAD_A = 64
LORA_DECAY = 64
LORA_AAA = 64
LORA_GATE = 160
DECAY_SCALE = 0.606531
GN_EPS = 64e-5
HEAD_B = 128
F_TINY = 1e-30
QK_NOPE = 128
QK_ROPE = 64
V_HEAD = 128
ROPE_THETA = 10000.0
NORM_EPS = 1e-6
CHUNK = 64
LOG2_E = 1.4426950408889634
Q_SCALE = (QK_NOPE + QK_ROPE) ** -0.5 * LOG2_E


def _cparams(*sem, flags=None):
    return pltpu.CompilerParams(dimension_semantics=sem, vmem_limit_bytes=VMEM_LIMIT, flags=flags)


def _sigmoid(x):
    return 1.0 / (1.0 + jnp.exp(-x))


def _dot(a, b):
    return jnp.dot(a, b, preferred_element_type=F32)


def _dot_nt(a, b):
    return lax.dot_general(a, b, (((1,), (1,)), ((), ())), preferred_element_type=F32)


def _dot_tn(a, b):
    return lax.dot_general(a, b, (((0,), (0,)), ((), ())), preferred_element_type=F32)


def _split3(x):
    x1 = x.astype(BF16)
    r1 = x - x1.astype(F32)
    x2 = r1.astype(BF16)
    x3 = (r1 - x2.astype(F32)).astype(BF16)
    return x1, x2, x3


def _rmsnorm_kernel(x_ref, g_ref, o_ref):
    x = x_ref[...]
    ms = jnp.mean(x * x, axis=-1, keepdims=True)
    o_ref[...] = (x * lax.rsqrt(ms + NORM_EPS) * g_ref[...]).astype(o_ref.dtype)


def rmsnorm(x, g, out_dtype, tm=512):
    T, D = x.shape
    return pl.pallas_call(
        _rmsnorm_kernel,
        out_shape=jax.ShapeDtypeStruct((T, D), out_dtype),
        grid=(T // tm,),
        in_specs=[pl.BlockSpec((tm, D), lambda i: (i, 0)),
                  pl.BlockSpec((1, D), lambda i: (0, 0))],
        out_specs=pl.BlockSpec((tm, D), lambda i: (i, 0)),
        compiler_params=_cparams("parallel"),
        name="rmsnorm",
    )(x, g.reshape(1, D))


def _mm_kernel(a_ref, w_ref, *rest, nk, epilogue, n_extra):
    extras = rest[:n_extra]
    o_ref = rest[n_extra]

    def finish(acc):
        o_ref[...] = epilogue(acc, *[e[...] for e in extras]).astype(o_ref.dtype)

    if nk == 1:
        finish(_dot(a_ref[...], w_ref[...]))
    else:
        acc_ref = rest[n_extra + 1]
        k = pl.program_id(2)

        @pl.when(k == 0)
        def _():
            acc_ref[...] = jnp.zeros_like(acc_ref)

        acc_ref[...] += _dot(a_ref[...], w_ref[...])

        @pl.when(k == nk - 1)
        def _():
            finish(acc_ref[...])


def matmul(a, w, layer, *, out_dtype, tm, tn, tk=None, col0=0, n=None, epilogue=None, extras=(),
           alias_extra=None, name="matmul"):
    M, K = a.shape
    N = w.shape[2] - col0 if n is None else n
    tk = K if tk is None else tk
    nk = K // tk
    assert col0 % tn == 0 and N % tn == 0 and M % tm == 0 and K % tk == 0, (col0, N, tn, M, tm, K, tk)
    j0 = col0 // tn
    epilogue = epilogue or (lambda acc: acc)
    kern = functools.partial(_mm_kernel, nk=nk, epilogue=epilogue, n_extra=len(extras))
    in_specs = [pl.BlockSpec((tm, tk), lambda i, j, k: (i, k)),
                pl.BlockSpec((None, tk, tn), lambda i, j, k: (layer, k, j0 + j))]
    in_specs += [pl.BlockSpec((tm, tn), lambda i, j, k: (i, j)) for _ in extras]
    aliases = {} if alias_extra is None else {2 + alias_extra: 0}
    return pl.pallas_call(
        kern,
        out_shape=jax.ShapeDtypeStruct((M, N), out_dtype),
        grid=(M // tm, N // tn, nk),
        in_specs=in_specs,
        out_specs=pl.BlockSpec((tm, tn), lambda i, j, k: (i, j)),
        scratch_shapes=[pltpu.VMEM((tm, tn), F32)] if nk > 1 else [],
        input_output_aliases=aliases,
        compiler_params=_cparams("parallel", "parallel", "arbitrary"),
        name=name,
    )(a, w, *extras)


def _branch_kernel(ya_ref, yb_ref, yc_ref, p_ref, ga_ref, gb_ref, gc_ref, o_ref):
    acc = _sigmoid(ga_ref[...].astype(F32)) * _dot(ya_ref[...], p_ref[0])
    acc += _sigmoid(gb_ref[...].astype(F32)) * _dot(yb_ref[...], p_ref[1])
    acc += _sigmoid(gc_ref[...].astype(F32)) * _dot(yc_ref[...], p_ref[2])
    o_ref[...] = acc.astype(o_ref.dtype)


def branch_mix(ya, yb, yc, p, layer, zg, *, tm=1024, tn=512):
    T, W = ya.shape
    D = p.shape[3]
    nj = D // tn
    y_spec = pl.BlockSpec((tm, W), lambda i, j: (i, 0))
    g_specs = [pl.BlockSpec((tm, tn), functools.partial(lambda i, j, n: (i, n * nj + j), n=n))
               for n in range(3)]
    return pl.pallas_call(
        _branch_kernel,
        out_shape=jax.ShapeDtypeStruct((T, D), BF16),
        grid=(T // tm, nj),
        in_specs=[y_spec, y_spec, y_spec,
                  pl.BlockSpec((None, 3, W, tn), lambda i, j: (layer, 0, 0, j))] + g_specs,
        out_specs=pl.BlockSpec((tm, tn), lambda i, j: (i, j)),
        compiler_params=_cparams("parallel", "parallel"),
        name="branch_mix",
    )(ya, yb, yc, p, zg, zg, zg)


def _rms_rows(h, g):
    return h * lax.rsqrt(jnp.mean(h * h, axis=-1, keepdims=True) + NORM_EPS) * g


def _wo_ln_kernel(a_ref, w_ref, h_ref, g_ref, h_out_ref, hn_ref):
    h = h_ref[...] + _dot(a_ref[...], w_ref[...])
    h_out_ref[...] = h
    hn_ref[...] = _rms_rows(h, g_ref[...]).astype(hn_ref.dtype)


def out_proj_norm(a, w, layer, h, g, *, tm=512):
    T, D = h.shape
    K = a.shape[1]
    row = lambda width: pl.BlockSpec((tm, width), lambda i: (i, 0))
    return pl.pallas_call(
        _wo_ln_kernel,
        out_shape=(jax.ShapeDtypeStruct((T, D), F32), jax.ShapeDtypeStruct((T, D), BF16)),
        grid=(T // tm,),
        in_specs=[row(K), pl.BlockSpec((None, K, D), lambda i: (layer, 0, 0)), row(D),
                  pl.BlockSpec((1, D), lambda i: (0, 0))],
        out_specs=(row(D), row(D)),
        input_output_aliases={2: 0},
        compiler_params=_cparams("parallel"),
        name="out_proj_norm",
    )(a, w, h, g.reshape(1, D))


def _ple_ln_kernel(h_ref, wpg_ref, p_ref, wpe_ref, g_ref, h_out_ref, hn_ref):
    h = h_ref[...]
    gate = _sigmoid(_dot(h.astype(BF16), wpg_ref[...]))
    h = h + gate * _dot(p_ref[...].astype(BF16), wpe_ref[...])
    h_out_ref[...] = h
    hn_ref[...] = _rms_rows(h, g_ref[...]).astype(hn_ref.dtype)


def ple_update_norm(h, wpg, p, wpe, layer, g, norm_dtype, *, tm=512):
    T, D = h.shape
    E = p.shape[2]
    row = lambda width: pl.BlockSpec((tm, width), lambda i: (i, 0))
    stacked = lambda r, c: pl.BlockSpec((None, r, c), lambda i: (layer, 0, 0))
    return pl.pallas_call(
        _ple_ln_kernel,
        out_shape=(jax.ShapeDtypeStruct((T, D), F32), jax.ShapeDtypeStruct((T, D), norm_dtype)),
        grid=(T // tm,),
        in_specs=[row(D), stacked(D, D), pl.BlockSpec((None, tm, E), lambda i: (layer, i, 0)), stacked(E, D),
                  pl.BlockSpec((1, D), lambda i: (0, 0))],
        out_specs=(row(D), row(D)),
        input_output_aliases={0: 0},
        compiler_params=_cparams("parallel"),
        name="ple_update_norm",
    )(h, wpg, p, wpe, g.reshape(1, D))


def _mla_q_kernel(cq_ref, g_ref, w_ref, c_ref, s_ref, o_ref, *, scale, n_heads):
    xn = _rms_rows(cq_ref[...], g_ref[...]).astype(BF16)
    x = _dot(xn, w_ref[...])
    c, s = c_ref[...], s_ref[...]
    for h in range(n_heads):
        lo = 2 * LANE * h
        hi = x[:, lo + QK_NOPE:lo + 2 * LANE]
        o_ref[:, lo:lo + QK_NOPE] = (x[:, lo:lo + QK_NOPE] * scale).astype(BF16)
        o_ref[:, lo + QK_NOPE:lo + 2 * LANE] = (hi * c + pltpu.roll(hi, LANE // 2, 1) * s).astype(BF16)


def mla_q_proj(z, g, w, ctab, stab, n_heads, scale, *, tm=512):
    T, R = z.shape[0], g.shape[-1]
    N = w.shape[1]
    cq = z
    return pl.pallas_call(
        functools.partial(_mla_q_kernel, scale=scale, n_heads=n_heads),
        out_shape=jax.ShapeDtypeStruct((T, N), BF16),
        grid=(T // tm,),
        in_specs=[pl.BlockSpec((tm, R), lambda i: (i, 0)),
                  pl.BlockSpec((1, R), lambda i: (0, 0)),
                  pl.BlockSpec((R, N), lambda i: (0, 0)),
                  pl.BlockSpec((tm, LANE), lambda i: (i, 0)),
                  pl.BlockSpec((tm, LANE), lambda i: (i, 0))],
        out_specs=pl.BlockSpec((tm, N), lambda i: (i, 0)),
        compiler_params=_cparams("parallel"),
        name="mla_q_proj",
    )(cq, g.reshape(1, R), w, ctab, stab)


def _mla_kv_kernel(z_ref, g_ref, w_ref, c_ref, s_ref, k_ref, v_ref, *, kv_lora, n_heads):
    xn = _rms_rows(z_ref[:, :kv_lora], g_ref[...]).astype(BF16)
    kv = _dot(xn, w_ref[...])
    kr = (z_ref[:, kv_lora:kv_lora + LANE] * c_ref[...] + z_ref[:, kv_lora + LANE:] * s_ref[...]).astype(BF16)
    for h in range(n_heads):
        lo = 2 * LANE * h
        k_ref[:, lo:lo + QK_NOPE] = kv[:, lo:lo + QK_NOPE].astype(BF16)
        k_ref[:, lo + QK_NOPE:lo + 2 * LANE] = kr
        v_ref[h * V_HEAD:(h + 1) * V_HEAD, :] = kv[:, lo + QK_NOPE:lo + 2 * LANE].T.astype(BF16)


def mla_kv_proj(zkv, col0, g, w, ctab, stab, n_heads, batch, *, tm=512):
    T = zkv.shape[0]
    R = g.shape[-1]
    ZW = R + 2 * LANE
    cb = col0 // ZW
    S = T // batch
    tm = min(tm, S)
    nsb = S // tm
    N = w.shape[1]
    return pl.pallas_call(
        functools.partial(_mla_kv_kernel, kv_lora=R, n_heads=n_heads),
        out_shape=(jax.ShapeDtypeStruct((T, n_heads * 2 * LANE), BF16),
                   jax.ShapeDtypeStruct((batch * n_heads * V_HEAD, S), BF16)),
        grid=(T // tm,),
        in_specs=[pl.BlockSpec((tm, ZW), lambda i: (i, cb)),
                  pl.BlockSpec((1, R), lambda i: (0, 0)),
                  pl.BlockSpec((R, N), lambda i: (0, 0)),
                  pl.BlockSpec((tm, LANE), lambda i: (i, 0)),
                  pl.BlockSpec((tm, LANE), lambda i: (i, 0))],
        out_specs=(pl.BlockSpec((tm, n_heads * 2 * LANE), lambda i: (i, 0)),
                   pl.BlockSpec((n_heads * V_HEAD, tm), lambda i: (i // nsb, i % nsb))),
        compiler_params=_cparams("parallel"),
        name="mla_kv_proj",
    )(zkv, g.reshape(1, R), w, ctab, stab)


ATTN_TILES = 8


def _attn_kernel(q_ref, k_ref, vt_ref, o_ref, *, tq):
    k = k_ref[...]
    tiles = [slice(j * tq, (j + 1) * tq) for j in range(ATTN_TILES)]
    scores = lambda t: _dot_nt(k, q_ref[t, :])

    def finish(t, st):
        p = jnp.exp2(st - jnp.max(st, axis=0, keepdims=True))
        ot = _dot(vt_ref[...], p.astype(BF16))
        o_ref[t, :] = (ot / jnp.sum(p, axis=0, keepdims=True)).T.astype(o_ref.dtype)

    st = scores(tiles[0])
    for j, t in enumerate(tiles):
        st_next = scores(tiles[j + 1]) if j + 1 < len(tiles) else None
        finish(t, st)
        st = st_next


def mla_attention(q, k, vt, batch, n_heads, *, tq=256):
    T = q.shape[0]
    S = T // batch
    tb = tq * ATTN_TILES
    nq = S // tb
    QW = q.shape[1] // n_heads
    return pl.pallas_call(
        functools.partial(_attn_kernel, tq=tq),
        out_shape=jax.ShapeDtypeStruct((T, n_heads * V_HEAD), BF16),
        grid=(batch, n_heads, nq),
        in_specs=[pl.BlockSpec((tb, QW), lambda b, h, i: (b * nq + i, h)),
                  pl.BlockSpec((S, QW), lambda b, h, i: (b, h)),
                  pl.BlockSpec((V_HEAD, S), lambda b, h, i: (b * n_heads + h, 0))],
        out_specs=pl.BlockSpec((tb, V_HEAD), lambda b, h, i: (b * nq + i, h)),
        compiler_params=_cparams("parallel", "parallel", "arbitrary"),
        name="mla_attention",
    )(q, k, vt)


def _rope_tables(positions):
    inv_freq = 1.0 / (ROPE_THETA ** (jnp.arange(0, QK_ROPE, 2, dtype=F32) / QK_ROPE))
    ang = positions.astype(F32).reshape(-1, 1) * inv_freq
    cos, sin = jnp.cos(ang), jnp.sin(ang)
    T = ang.shape[0]
    z64 = jnp.zeros((T, LANE - QK_ROPE), F32)
    ck = jnp.concatenate([cos, cos, z64], axis=1)
    sk = jnp.concatenate([-sin, sin, z64], axis=1)
    return ck * Q_SCALE, sk * Q_SCALE, ck, sk


def _mla_q_weights(w_uq, n_heads):
    R = w_uq.shape[0]
    w = w_uq.reshape(R, n_heads, QK_NOPE + QK_ROPE)
    half = QK_ROPE // 2
    w = jnp.concatenate([w, w[:, :, QK_NOPE + half:], w[:, :, QK_NOPE:QK_NOPE + half]], axis=2)
    return w.reshape(R, n_heads * 2 * LANE).astype(BF16)


N_LEVELS = 6
HGRN_BATCH = 4


def _gla_constants():
    C = CHUNK
    t = np.arange(C)[:, None]
    u = np.arange(C)[None, :]
    mk = np.zeros((N_LEVELS + 1, C, C), np.float32)
    for l in range(N_LEVELS):
        m = C >> (l + 1)
        mk[l] = (t // (2 * m) == u // (2 * m)) & ((t % (2 * m)) >= m) & ((u % (2 * m)) < m)
    mk[N_LEVELS] = t == u
    tri = (u <= t).astype(np.float32)
    to2d = lambda a: np.ascontiguousarray(a).reshape(-1, C)
    return tri, np.ascontiguousarray(tri[::-1, ::-1]), to2d(mk), to2d(mk[:, ::-1, ::-1])


def _level_exponent(b, g, m, rev, row):
    C = CHUNK
    if m >= 4:
        r = m if rev else m - 1
        b_mid = jnp.concatenate([jnp.broadcast_to(b[s + r:s + r + 1, :], (2 * m, b.shape[1]))
                                 for s in range(0, C, 2 * m)], axis=0)
        return -jnp.abs(b - b_mid)
    g_next = pltpu.roll(g, C - 1, 0)
    g_prev = pltpu.roll(g, 1, 0)
    if m == 2:
        p4 = row % 4
        if rev:
            return jnp.where(p4 == 3, g_prev, jnp.where(p4 == 2, 0.0, jnp.where(p4 == 1, g, g + g_next)))
        return jnp.where(p4 == 0, g_next, jnp.where(p4 == 1, 0.0, jnp.where(p4 == 2, g, g + g_prev)))
    return jnp.where((row % 2 == 1) != rev, g, 0.0)


def _gla_local(chunks):
    C = CHUNK
    row = lax.broadcasted_iota(jnp.int32, (C, 1), 0)
    gs = [c[3] * LOG2_E for c in chunks]
    parts = [_split3(g) for g in gs]
    bs = [_dot(c[4], p[0]) + _dot(c[4], p[1]) + _dot(c[4], p[2]) for c, p in zip(chunks, parts)]
    qbs = [c[0].astype(BF16) for c in chunks]
    kbs = [c[1].astype(BF16) for c in chunks]
    scores = [c[5][N_LEVELS * C:(N_LEVELS + 1) * C] * _dot_nt(qb, kb) for c, qb, kb in zip(chunks, qbs, kbs)]
    for l in range(N_LEVELS):
        pls = [jnp.exp2(_level_exponent(b, g, C >> (l + 1), c[6], row)).astype(BF16)
               for c, b, g in zip(chunks, bs, gs)]
        prods = [_dot_nt(qb * p, kb * p) for qb, kb, p in zip(qbs, kbs, pls)]
        scores = [s + c[5][l * C:(l + 1) * C] * a for s, c, a in zip(scores, chunks, prods)]
    out = []
    for c, b, s in zip(chunks, bs, scores):
        q, k, v, rev = c[0], c[1], c[2], c[6]
        b_tot = b[0:1] if rev else b[C - 1:C]
        vb = v.astype(BF16)
        out.append((_dot(s.astype(BF16), vb), (q * jnp.exp2(b)).astype(BF16),
                    _dot_tn(vb, (k * jnp.exp2(b_tot - b)).astype(BF16)), jnp.exp2(b_tot)))
    return out


def _hgrn_kernel(q_ref, zf_ref, zb_ref, i_ref, g_ref, lb_ref, ng_ref, trif_ref, trib_ref, mkf_ref, mkb_ref,
                 o_ref, of_ref, ob_ref, *, n_chunks):
    C = CHUNK
    lb = lb_ref[...]
    one_m_lb = 1.0 - lb
    trif, trib, mkf, mkb = trif_ref[...], trib_ref[...], mkf_ref[...], mkb_ref[...]

    def gate(zz):
        w = one_m_lb * _sigmoid(zz)
        return one_m_lb - w, jnp.log(jnp.maximum(lb + w, F_TINY))

    def load(ref, c):
        return ref[pl.ds(pl.multiple_of(c * C, C), C), :]

    def chunk_inputs(c, z_ref, tri, mk, rev):
        q = load(q_ref, c)
        k, g = gate(load(z_ref, c))
        return (q * _sigmoid(q), k, load(i_ref, c), g, tri, mk, rev)

    def body(it, carry):
        states = list(carry)
        ids = [[it * HGRN_BATCH + j for j in range(HGRN_BATCH)]]
        ids.append([n_chunks - 1 - c for c in ids[0]])
        loc = _gla_local([chunk_inputs(c, zf_ref, trif, mkf, False) for c in ids[0]]
                         + [chunk_inputs(c, zb_ref, trib, mkb, True) for c in ids[1]])
        for d, out_ref in enumerate((of_ref, ob_ref)):
            for j, c in enumerate(ids[d]):
                o_intra, q_dec, st_inc, p_tot = loc[d * HGRN_BATCH + j]
                out_ref[pl.ds(pl.multiple_of(c * C, C), C), :] = o_intra + _dot_nt(q_dec, states[d].astype(BF16))
                states[d] = states[d] * p_tot + st_inc
        return tuple(states)

    dv, dk = i_ref.shape[1], q_ref.shape[1]
    z = jnp.zeros((dv, dk), F32)
    lax.fori_loop(0, n_chunks // HGRN_BATCH, body, (z, z))

    o = of_ref[...] + ob_ref[...]
    ms = jnp.mean(o * o, axis=-1, keepdims=True)
    gg = g_ref[...]
    o_ref[...] = (o * lax.rsqrt(ms + NORM_EPS) * ng_ref[...] * (gg * _sigmoid(gg))).astype(o_ref.dtype)


def hgrn2_mixer(z, lb, norm_g, batch, *, col0=0):
    T = z.shape[0]
    W = lb.shape[-1]
    S = T // batch
    H = W // HEAD_B
    c0 = col0 // HEAD_B
    trif, trib, mkf, mkb = _gla_constants()
    part = lambda n: pl.BlockSpec((S, HEAD_B), functools.partial(lambda b, h, n: (b, c0 + n * H + h), n=n))
    const = lambda a: pl.BlockSpec(a.shape, lambda b, h: (0, 0))
    return pl.pallas_call(
        functools.partial(_hgrn_kernel, n_chunks=S // CHUNK),
        out_shape=jax.ShapeDtypeStruct((T, W), BF16),
        grid=(batch, H),
        in_specs=[part(0), part(1), part(2), part(3), part(4),
                  pl.BlockSpec((1, HEAD_B), lambda b, h: (0, h)),
                  pl.BlockSpec((1, HEAD_B), lambda b, h: (0, 0)),
                  const(trif), const(trib), const(mkf), const(mkb)],
        out_specs=pl.BlockSpec((S, HEAD_B), lambda b, h: (b, h)),
        scratch_shapes=[pltpu.VMEM((S, HEAD_B), F32), pltpu.VMEM((S, HEAD_B), F32)],
        compiler_params=_cparams("parallel", "parallel"),
        name="hgrn2_mixer",
    )(z, z, z, z, z, lb.reshape(1, W), norm_g.reshape(1, HEAD_B),
      jnp.asarray(trif, BF16), jnp.asarray(trib, BF16), jnp.asarray(mkf), jnp.asarray(mkb))


def _head_sum(x, bd):
    x1, x2, _ = _split3(x)
    return _dot(x1, bd) + _dot(x2, bd)


def _rwkv_prep_kernel(x_ref, xp_ref, xn_ref, mu_ref, w0_ref, w2_ref, a0_ref, a2_ref, g2_ref, kk_ref,
                      ka_ref, rk_ref, bd_ref,
                      r_ref, v_ref, kap_ref, kdf_ref, kdb_ref, alf_ref, alb_ref, lwf_ref, lwb_ref,
                      g_ref, bon_ref, *, tm, seq, width):
    W = width
    i = pl.program_id(0)
    x = x_ref[...]
    at_start = (i * tm) % seq == 0
    at_end = ((i + 1) * tm) % seq == 0
    prev_row = jnp.where(at_start, 0.0, xp_ref[7:8, :])
    next_row = jnp.where(at_end, 0.0, xn_ref[0:1, :])
    row = lax.broadcasted_iota(jnp.int32, (8, 1), 0)
    x_prev = pltpu.roll(x, 1, 0)
    x_prev = jnp.concatenate([jnp.where(row == 0, prev_row, x_prev[:8]), x_prev[8:]], axis=0)
    x_next = pltpu.roll(x, tm - 1, 0)
    x_next = jnp.concatenate([x_next[:tm - 8], jnp.where(row == 7, next_row, x_next[tm - 8:])], axis=0)
    u = x + mu_ref[0:1, :] * (x_prev - x) + mu_ref[1:2, :] * (x_next - x)

    r, k, v = u[:, :W], u[:, W:2 * W], u[:, 2 * W:3 * W]
    o = 3 * W
    wd = jnp.tanh(u[:, o:o + 2 * LORA_DECAY]).astype(BF16)
    o += 2 * LORA_DECAY
    ad = u[:, o:o + 2 * LORA_AAA].astype(BF16)
    o += 2 * LORA_AAA
    gd = _sigmoid(u[:, o:]).astype(BF16)

    bd = bd_ref[...]
    kkr = k * kk_ref[...]
    sq = kkr * kkr
    rks = []
    a_dir = []
    for n in range(2):
        w_raw = w0_ref[n:n + 1, :] + _dot(wd[:, n * LORA_DECAY:(n + 1) * LORA_DECAY], w2_ref[n])
        lw = -DECAY_SCALE * _sigmoid(w_raw)
        a = _sigmoid(a0_ref[n:n + 1, :] + _dot(ad[:, n * LORA_AAA:(n + 1) * LORA_AAA], a2_ref[n]))
        a_dir.append(a)
        (lwf_ref, lwb_ref)[n][...] = lw
    kd = [k * (1.0 + (a - 1.0) * ka_ref[...]) for a in a_dir]
    kdf_ref[...] = kd[0].astype(kdf_ref.dtype)
    kdb_ref[...] = kd[1].astype(kdb_ref.dtype)
    rkk = r * (kd[0] + kd[1]) * rk_ref[...]
    for c in range(W // LANE):
        sl = slice(c * LANE, (c + 1) * LANE)
        nrm = jnp.maximum(jnp.sqrt(_head_sum(sq[:, sl], bd)), 1e-12)
        kap = kkr[:, sl] / nrm
        kap_ref[:, sl] = kap.astype(kap_ref.dtype)
        alf_ref[:, sl] = (kap * a_dir[0][:, sl]).astype(alf_ref.dtype)
        alb_ref[:, sl] = (kap * a_dir[1][:, sl]).astype(alb_ref.dtype)
        bon_ref[:, sl] = _head_sum(rkk[:, sl], bd) * v[:, sl]
    r_ref[...] = r.astype(r_ref.dtype)
    v_ref[...] = v.astype(v_ref.dtype)
    g_ref[...] = _dot(gd, g2_ref[...]).astype(g_ref.dtype)


def rwkv_prep(z, mu, w0, w2, a0, a2, g2, k_k, k_a, r_k, batch, *, tm=256):
    T, ZW = z.shape
    W = w0.shape[-1]
    S = T // batch
    nb8 = tm // 8
    used = 3 * W + 2 * LORA_DECAY + 2 * LORA_AAA + LORA_GATE
    mu_p = jnp.pad(mu, ((0, 0), (0, ZW - used)))
    g2_p = jnp.pad(g2, ((0, ZW - used), (0, 0))).astype(BF16)
    bd = np.kron(np.eye(LANE // HEAD_A, dtype=np.float32), np.ones((HEAD_A, HEAD_A), np.float32))
    full = lambda a: pl.BlockSpec(a.shape, lambda i: (0,) * a.ndim)
    vec = lambda a: a.reshape(1, W)
    args = [z, z, z, mu_p, w0, w2.astype(BF16), a0, a2.astype(BF16), g2_p, vec(k_k), vec(k_a),
            vec(r_k), jnp.asarray(bd, BF16)]
    in_specs = [pl.BlockSpec((tm, ZW), lambda i: (i, 0)),
                pl.BlockSpec((8, ZW), lambda i: (jnp.maximum(i * nb8 - 1, 0), 0)),
                pl.BlockSpec((8, ZW), lambda i: (jnp.minimum((i + 1) * nb8, T // 8 - 1), 0))]
    in_specs += [full(a) for a in args[3:]]
    b16 = jax.ShapeDtypeStruct((T, W), BF16)
    f32 = jax.ShapeDtypeStruct((T, W), F32)
    return pl.pallas_call(
        functools.partial(_rwkv_prep_kernel, tm=tm, seq=S, width=W),
        out_shape=(b16,) * 7 + (f32, f32, b16, f32),
        grid=(T // tm,),
        in_specs=in_specs,
        out_specs=(pl.BlockSpec((tm, W), lambda i: (i, 0)),) * 11,
        compiler_params=_cparams("parallel"),
        name="rwkv_prep",
    )(*args)


RWKV_GROUP = 256


def _rwkv_constants():
    G, C = RWKV_GROUP, CHUNK
    t = np.arange(G)[:, None]
    u = np.arange(G)[None, :]
    dd = (((t // C) == (u // C)) & (u <= t)).astype(np.float32)
    return dd, np.ascontiguousarray(dd[::-1, ::-1])


def _bdot(a, b):
    return _dot(a.astype(BF16), b.astype(BF16))


def _unit_tri_inverses(ms, eye, diag_blocks):
    each = lambda f, *ls: [f(*a) for a in zip(*ls)]
    mds = [jnp.where(diag_blocks, m, 0.0) for m in ms]
    mos = each(lambda m, md: m - md, ms, mds)
    xs = [eye - md for md in mds]
    ps = each(_bdot, mds, mds)
    for _ in range(2):
        xs = each(lambda x, p: x + _bdot(x, p), xs, ps)
        ps = each(_bdot, ps, ps)
    tds = each(lambda x, p: x + _bdot(x, p), xs, ps)
    gs = each(_bdot, tds, mos)
    g2s = each(_bdot, gs, gs)
    ys = each(lambda g, g2: (eye - g) + _bdot(eye - g, g2), gs, g2s)
    return each(_bdot, ys, tds)


def _rwkv_groups(pairs, eye, diag_blocks, head_masks):
    G = pairs[0][0].shape[0]
    pre = []
    for r, v, kap, kd, al, lw, dd, rev, causal_incl, causal_strict in [
            (r, v, kap) + tuple(d) for r, v, kap, dirs in pairs for d in dirs]:
        l1, l2, _ = _split3(lw)
        b_in = _dot(dd, l1) + _dot(dd, l2)
        r_tot = 0 if rev else CHUNK - 1
        b_tot = jnp.concatenate([jnp.broadcast_to(b_in[s + r_tot:s + r_tot + 1], (CHUNK, b_in.shape[1]))
                                 for s in range(0, G, CHUNK)], axis=0)
        b_ex, b_out = b_in - lw, b_tot - b_in
        p_neg = jnp.exp(-b_in)
        p_out = jnp.exp(b_out)
        pre.append(dict(kap_h=kap * jnp.exp(b_ex), r_h=r * jnp.exp(b_in), al_n=(al * p_neg).astype(BF16),
                        kd_n=(kd * p_neg).astype(BF16), kout=kd * p_out, aout=al * p_out, p_tot=jnp.exp(b_tot),
                        incl=causal_incl, strict=causal_strict, v=v))
    chains = [(p, hm) for p in pre for hm in head_masks]
    kap_hb = [jnp.where(hm, p["kap_h"], 0.0).astype(BF16) for p, hm in chains]
    r_hb = [jnp.where(hm, p["r_h"], 0.0).astype(BF16) for p, hm in chains]
    v_h = [jnp.where(hm, p["v"], 0.0).astype(BF16) for p, hm in chains]
    ms = [jnp.where(p["strict"], _dot_nt(k_, p["al_n"]), 0.0) for (p, _), k_ in zip(chains, kap_hb)]
    ns = [jnp.where(p["strict"], _dot_nt(k_, p["kd_n"]), 0.0) for (p, _), k_ in zip(chains, kap_hb)]
    ras = [jnp.where(p["incl"], _dot_nt(r_, p["al_n"]), 0.0) for (p, _), r_ in zip(chains, r_hb)]
    rks = [jnp.where(p["incl"], _dot_nt(r_, p["kd_n"]), 0.0) for (p, _), r_ in zip(chains, r_hb)]
    nvs = [_bdot(n, vh).astype(BF16) for n, vh in zip(ns, v_h)]
    y0s = [_bdot(rk, vh) for rk, vh in zip(rks, v_h)]
    racs = [sum(ra[:, s:s + CHUNK] for s in range(0, G, CHUNK)) for ra in ras]
    tinvs = [t.astype(BF16) for t in _unit_tri_inverses(ms, eye, diag_blocks)]
    kw_uv = [_dot(t, jnp.concatenate([k_, nv], axis=1)) for t, k_, nv in zip(tinvs, kap_hb, nvs)]
    out = []
    for d, p in enumerate(pre):
        a, b = kw_uv[2 * d], kw_uv[2 * d + 1]
        out.append((a[:, :LANE] + b[:, :LANE], p["r_h"], jnp.concatenate(racs[2 * d:2 * d + 2], axis=1),
                    p["kout"], p["aout"], a[:, LANE:] + b[:, LANE:], y0s[2 * d] + y0s[2 * d + 1], p["p_tot"]))
    return out


N_LOCAL = 8
LOCAL_PAIRS = 2


def _rwkv_local_kernel(r_ref, v_ref, kap_ref, kdf_ref, kdb_ref, alf_ref, alb_ref, lwf_ref, lwb_ref,
                       ddf_ref, ddb_ref, *out_refs):
    G, C = RWKV_GROUP, CHUNK
    ti = lax.broadcasted_iota(jnp.int32, (G, G), 0)
    si = lax.broadcasted_iota(jnp.int32, (G, G), 1)
    same_chunk = (ti // C) == (si // C)
    eye = (ti == si).astype(F32)
    diag_blocks = (ti // 16) == (si // 16)
    lane = lax.broadcasted_iota(jnp.int32, (1, LANE), 1)
    head_masks = [lane < HEAD_A, lane >= HEAD_A]
    masks = []
    for d in range(2):
        masks.append((same_chunk & ((si <= ti) if d == 0 else (si >= ti)),
                      same_chunk & ((si < ti) if d == 0 else (si > ti))))
    pairs = []
    for pr in range(LOCAL_PAIRS):
        ln = slice(pr * LANE, (pr + 1) * LANE)
        dirs = [(kd_ref[:, ln], al_ref[:, ln], lw_ref[:, ln], dd_ref[...], d == 1) + masks[d]
                for d, (kd_ref, al_ref, lw_ref, dd_ref) in enumerate(((kdf_ref, alf_ref, lwf_ref, ddf_ref),
                                                                      (kdb_ref, alb_ref, lwb_ref, ddb_ref)))]
        pairs.append((r_ref[:, ln], v_ref[:, ln], kap_ref[:, ln], dirs))
    results = _rwkv_groups(pairs, eye, diag_blocks, head_masks)
    for pr in range(LOCAL_PAIRS):
        ln = slice(pr * LANE, (pr + 1) * LANE)
        for d in range(2):
            res = results[2 * pr + d]
            outs = out_refs[d * N_LOCAL:(d + 1) * N_LOCAL]
            for o_ref, val in zip(outs[:-1], res[:-1]):
                o_ref[:, ln] = val.astype(o_ref.dtype)
            p_tot = res[-1]
            outs[-1][:, ln] = jnp.concatenate([p_tot[c * C:c * C + 8] for c in range(G // C)], axis=0)


def rwkv_local(r, v, kap, kd_f, kd_b, al_f, al_b, lw_f, lw_b):
    T, W = r.shape
    G = RWKV_GROUP
    ddf, ddb = _rwkv_constants()
    bw = LANE * LOCAL_PAIRS
    blk = pl.BlockSpec((G, bw), lambda i, h: (i, h))
    pblk = pl.BlockSpec((G // 8, bw), lambda i, h: (i, h))
    const = lambda a: pl.BlockSpec(a.shape, lambda i, h: (0, 0))
    b16 = jax.ShapeDtypeStruct((T, W), BF16)
    per_dir = (b16,) * 7 + (jax.ShapeDtypeStruct((T // 8, W), F32),)
    outs = pl.pallas_call(
        _rwkv_local_kernel,
        out_shape=per_dir * 2,
        grid=(T // G, W // bw),
        in_specs=[blk] * 9 + [const(ddf), const(ddb)],
        out_specs=((blk,) * 7 + (pblk,)) * 2,
        compiler_params=_cparams("parallel", "parallel"),
        name="rwkv_local",
    )(r, v, kap, kd_f, kd_b, al_f, al_b, lw_f, lw_b,
      jnp.asarray(ddf, BF16), jnp.asarray(ddb, BF16))
    return outs[:N_LOCAL], outs[N_LOCAL:]


SCAN_PAIRS = 8


def _rwkv_scan_chunks(chains, head_masks, same_head):
    C = CHUNK
    m1s = [_dot_nt(jnp.concatenate([c[0], c[1]], axis=0), c[9].astype(BF16)) for c in chains]
    us = [m1[:C] + c[5] for m1, c in zip(m1s, chains)]
    incs = [_dot_tn(jnp.concatenate([c[7].astype(BF16), (-u).astype(BF16)], axis=0),
                    jnp.concatenate([c[3], c[4]], axis=0)) for u, c in zip(us, chains)]
    sts = [jnp.where(same_head, c[9] * c[8] + inc, 0.0) for c, inc in zip(chains, incs)]
    u_cats = [jnp.concatenate([jnp.where(hm, u, 0.0).astype(BF16) for hm in head_masks], axis=0) for u in us]
    ys = [m1[C:] + c[6] - _dot(c[2], u_cat) for m1, c, u_cat in zip(m1s, chains, u_cats)]
    return list(zip(ys, sts))


def _rwkv_scan_kernel(*refs, n_chunks, n_pairs):
    C = CHUNK
    f_refs, vf_ref = refs[:N_LOCAL], refs[N_LOCAL]
    b_refs, vb_ref = refs[N_LOCAL + 1:2 * N_LOCAL + 1], refs[2 * N_LOCAL + 1]
    yf_ref, yb_ref, stf_ref, stb_ref = refs[2 * N_LOCAL + 2:]

    @pl.when(pl.program_id(2) == 0)
    def _():
        stf_ref[...] = jnp.zeros_like(stf_ref)
        stb_ref[...] = jnp.zeros_like(stb_ref)

    lane = lax.broadcasted_iota(jnp.int32, (1, LANE), 1)
    head_masks = [lane < HEAD_A, lane >= HEAD_A]
    vi = lax.broadcasted_iota(jnp.int32, (LANE, LANE), 0)
    ki = lax.broadcasted_iota(jnp.int32, (LANE, LANE), 1)
    same_head = (vi // HEAD_A) == (ki // HEAD_A)

    def rows(ref, c, n):
        return ref[pl.ds(pl.multiple_of(c * n, n), n), :]

    def body(it, carry):
        states = [list(carry[0]), list(carry[1])]
        chains = []
        for d, (d_refs, v_ref) in enumerate(((f_refs, vf_ref), (b_refs, vb_ref))):
            c = it if d == 0 else n_chunks - 1 - it
            vals = [rows(ref, c, C) for ref in d_refs[:-1]] + [rows(v_ref, c, C), rows(d_refs[-1], c, 8)[0:1]]
            for p in range(n_pairs):
                chains.append(tuple(a[:, p * LANE:(p + 1) * LANE] for a in vals) + (states[d][p],))
        res = _rwkv_scan_chunks(chains, head_masks, same_head)
        for d, y_ref in enumerate((yf_ref, yb_ref)):
            c = it if d == 0 else n_chunks - 1 - it
            y_ref[pl.ds(pl.multiple_of(c * C, C), C), :] = jnp.concatenate(
                [res[d * n_pairs + p][0] for p in range(n_pairs)], axis=1)
            states[d] = [res[d * n_pairs + p][1] for p in range(n_pairs)]
        return tuple(states[0]), tuple(states[1])

    init = tuple(tuple(ref[p] for p in range(n_pairs)) for ref in (stf_ref, stb_ref))
    st_f, st_b = lax.fori_loop(0, n_chunks, body, init)
    for p in range(n_pairs):
        stf_ref[p] = st_f[p]
        stb_ref[p] = st_b[p]


def rwkv_scan(loc_f, loc_b, v, batch, *, ts=256):
    T, W = v.shape
    S = T // batch
    ts = min(ts, S)
    ns = S // ts
    bw = LANE * SCAN_PAIRS
    fwd = lambda rows: pl.BlockSpec((rows, bw), lambda b, h, s: (b * ns + s, h))
    bwd = lambda rows: pl.BlockSpec((rows, bw), lambda b, h, s: (b * ns + ns - 1 - s, h))
    specs = lambda mk: [mk(ts)] * (N_LOCAL - 1) + [mk(ts // 8), mk(ts)]
    out = jax.ShapeDtypeStruct((T, W), F32)
    state = pltpu.VMEM((SCAN_PAIRS, LANE, LANE), F32)
    return pl.pallas_call(
        functools.partial(_rwkv_scan_kernel, n_chunks=ts // CHUNK, n_pairs=SCAN_PAIRS),
        out_shape=(out, out),
        grid=(batch, W // bw, ns),
        in_specs=specs(fwd) + specs(bwd),
        out_specs=(fwd(ts), bwd(ts)),
        scratch_shapes=[state, state],
        compiler_params=_cparams("parallel", "parallel", "arbitrary"),
        name="rwkv_scan",
    )(*loc_f, v, *loc_b, v)


def _rwkv_out_kernel(yf_ref, yb_ref, bon_ref, g_ref, gw_ref, gb_ref, bd_ref, o_ref, *, width):
    bd = bd_ref[...]
    inv_n = 1.0 / HEAD_A
    for c in range(width // LANE):
        sl = slice(c * LANE, (c + 1) * LANE)
        y = yf_ref[:, sl] + yb_ref[:, sl]
        mean = _head_sum(y, bd) * inv_n
        d = y - mean
        var = _head_sum(d * d, bd) * inv_n
        yn = d * lax.rsqrt(var + GN_EPS) * gw_ref[:, sl] + gb_ref[:, sl] + bon_ref[:, sl]
        o_ref[:, sl] = (yn * g_ref[:, sl]).astype(o_ref.dtype)


def rwkv_out(y_f, y_b, bonus, g, gn_w, gn_b, *, tm=512):
    T, W = y_f.shape
    bd = np.kron(np.eye(LANE // HEAD_A, dtype=np.float32), np.ones((HEAD_A, HEAD_A), np.float32))
    row = pl.BlockSpec((tm, W), lambda i: (i, 0))
    vec = pl.BlockSpec((1, W), lambda i: (0, 0))
    return pl.pallas_call(
        functools.partial(_rwkv_out_kernel, width=W),
        out_shape=jax.ShapeDtypeStruct((T, W), BF16),
        grid=(T // tm,),
        in_specs=[row, row, row, row, vec, vec, pl.BlockSpec((LANE, LANE), lambda i: (0, 0))],
        out_specs=row,
        compiler_params=_cparams("parallel"),
        name="rwkv_out",
    )(y_f, y_b, bonus, g, gn_w.reshape(1, W), gn_b.reshape(1, W), jnp.asarray(bd, BF16))


def rwkv7_mixer(z, mu, w0, w2, a0, a2, g2, k_k, k_a, r_k, gn_w, gn_b, batch):
    r, v, kap, kd_f, kd_b, al_f, al_b, lw_f, lw_b, g, bonus = rwkv_prep(
        z, mu, w0, w2, a0, a2, g2, k_k, k_a, r_k.reshape(-1), batch)
    loc_f, loc_b = rwkv_local(r, v, kap, kd_f, kd_b, al_f, al_b, lw_f, lw_b)
    y_f, y_b = rwkv_scan(loc_f, loc_b, v, batch)
    return rwkv_out(y_f, y_b, bonus, g, gn_w, gn_b)


def _relu2(acc):
    r = jnp.maximum(acc, 0.0)
    return r * r


def _add(acc, res):
    return acc + res


def kernel(x, p, positions, ln1_g, w_in, rwkv_mu, rwkv_w0, rwkv_w2, rwkv_a0, rwkv_a2, rwkv_g2, rwkv_kk, rwkv_ka, rwkv_rk, rwkv_gn_w, rwkv_gn_b, hgrn_lb, hgrn_norm_g, mla_q_norm_g, mla_kv_norm_g, mla_w_uq, mla_w_ukv, w_branch, w_o, ln2_g, w_mlp1, w_mlp2, w_pe, w_pg, final_g):
    Bn, S, D = x.shape
    L = w_in.shape[0]
    T = Bn * S
    W = rwkv_w0.shape[-1]
    q_lora, kv_lora = mla_q_norm_g.shape[-1], mla_kv_norm_g.shape[-1]
    n_heads_c = mla_w_ukv.shape[-1] // (QK_NOPE + V_HEAD)
    rwkv_w = 3 * W + 2 * LORA_DECAY + 2 * LORA_AAA + LORA_GATE
    hgrn_w = 5 * W
    o_hgrn = rwkv_w
    o_cq = o_hgrn + hgrn_w
    o_ckv = o_cq + q_lora
    o_kr = o_ckv + kv_lora
    o_gate = o_kr + QK_ROPE
    half = QK_ROPE // 2

    lb_w = jax.nn.softmax(hgrn_lb.astype(F32), axis=0)
    lower_bounds = jnp.cumsum(lb_w, axis=0) - lb_w[0]
    cq_tab, sq_tab, ck_tab, sk_tab = _rope_tables(positions)

    pad3 = lambda w, mult: jnp.pad(w, ((0, 0), (0, 0), (0, (-w.shape[2]) % mult)))
    w_kr = w_in[:, :, o_kr:o_gate]
    w_all = jnp.concatenate(
        [pad3(w_in[:, :, :rwkv_w], 1024), w_in[:, :, o_hgrn:o_cq], w_in[:, :, o_gate:], w_in[:, :, o_cq:o_kr],
         pad3(w_kr, LANE), pad3(jnp.concatenate([w_kr[:, :, half:], w_kr[:, :, :half]], axis=2), LANE)],
        axis=2).astype(BF16)
    n_rwkv = rwkv_w + (-rwkv_w) % 512
    n_hgrn, n_gate = hgrn_w, w_in.shape[2] - o_gate
    n_ckv = q_lora + kv_lora + 2 * LANE
    c_hgrn = rwkv_w + (-rwkv_w) % 1024
    c_gate, c_ckv = c_hgrn + n_hgrn, c_hgrn + n_hgrn + n_gate
    w_branch_b, w_o_b = w_branch.astype(BF16), w_o.astype(BF16)
    w_mlp1_b, w_mlp2_b = w_mlp1.astype(BF16), w_mlp2.astype(BF16)
    w_pg_b, w_pe_b = w_pg.astype(BF16), w_pe.astype(BF16)
    p2 = p.reshape(L, T, -1)

    h = x.reshape(T, D)
    hn = rmsnorm(h, ln1_g[0], BF16)
    for l in range(L):
        z_rwkv = matmul(hn, w_all, l, out_dtype=F32, tm=1024, tn=n_rwkv // 2, n=n_rwkv, name="in_rwkv")
        z_hgrn = matmul(hn, w_all, l, out_dtype=F32, tm=1024, tn=1024, col0=c_hgrn, n=n_hgrn, name="in_hgrn")
        z_gate = matmul(hn, w_all, l, out_dtype=BF16, tm=1024, tn=1024, col0=c_gate, n=n_gate, name="in_gate")
        z_ckv = matmul(hn, w_all, l, out_dtype=F32, tm=1024, tn=n_ckv // 2, col0=c_ckv, n=n_ckv, name="in_ckv")

        y_a = rwkv7_mixer(z_rwkv, rwkv_mu[l], rwkv_w0[l], rwkv_w2[l], rwkv_a0[l], rwkv_a2[l], rwkv_g2[l],
                          rwkv_kk[l], rwkv_ka[l], rwkv_rk[l], rwkv_gn_w[l], rwkv_gn_b[l], Bn)
        y_b = hgrn2_mixer(z_hgrn, lower_bounds[l], hgrn_norm_g[l], Bn)
        q = mla_q_proj(z_ckv, mla_q_norm_g[l], _mla_q_weights(mla_w_uq[l], n_heads_c), cq_tab, sq_tab,
                       n_heads_c, Q_SCALE)
        k, vt = mla_kv_proj(z_ckv, q_lora, mla_kv_norm_g[l], mla_w_ukv[l].astype(BF16), ck_tab, sk_tab,
                            n_heads_c, Bn)
        y_c = mla_attention(q, k, vt, Bn, n_heads_c)

        mixed = branch_mix(y_a, y_b, y_c, w_branch_b, l, z_gate)
        h, hn2 = out_proj_norm(mixed, w_o_b, l, h, ln2_g[l])
        hid = matmul(hn2, w_mlp1_b, l, out_dtype=BF16, tm=1024, tn=1024, epilogue=_relu2, name="mlp1")
        h = matmul(hid, w_mlp2_b, l, out_dtype=F32, tm=1024, tn=1024, tk=2048, epilogue=_add,
                   extras=(h,), alias_extra=0, name="mlp2")
        last = l == L - 1
        h, hn = ple_update_norm(h, w_pg_b, p2, w_pe_b, l, final_g if last else ln1_g[l + 1],
                                F32 if last else BF16)
    return hn.reshape(Bn, S, D)
```

```python
import functools

import numpy as np
import jax
import jax.numpy as jnp
from jax import lax
from jax.experimental import pallas as pl
from jax.experimental.pallas import tpu as pltpu

F32 = jnp.float32
BF16 = jnp.bfloat16

LANE = 128
VMEM_LIMIT = 48 * 2**20

HEAD_A = 64
LORA_DECAY = 64
LORA_AAA = 64
LORA_GATE = 160
DECAY_SCALE = 0.606531
GN_EPS = 64e-5
HEAD_B = 128
F_TINY = 1e-30
QK_NOPE = 128
QK_ROPE = 64
V_HEAD = 128
ROPE_THETA = 10000.0
NORM_EPS = 1e-6
CHUNK = 64
LOG2_E = 1.4426950408889634
Q_SCALE = (QK_NOPE + QK_ROPE) ** -0.5 * LOG2_E


def _cparams(*sem, flags=None):
    return pltpu.CompilerParams(dimension_semantics=sem, vmem_limit_bytes=VMEM_LIMIT, flags=flags)


def _sigmoid(x):
    return 1.0 / (1.0 + jnp.exp(-x))


def _dot(a, b):
    return jnp.dot(a, b, preferred_element_type=F32)


def _dot_nt(a, b):
    return lax.dot_general(a, b, (((1,), (1,)), ((), ())), preferred_element_type=F32)


def _dot_tn(a, b):
    return lax.dot_general(a, b, (((0,), (0,)), ((), ())), preferred_element_type=F32)


def _split3(x):
    x1 = x.astype(BF16)
    r1 = x - x1.astype(F32)
    x2 = r1.astype(BF16)
    x3 = (r1 - x2.astype(F32)).astype(BF16)
    return x1, x2, x3


def _rmsnorm_kernel(x_ref, g_ref, o_ref):
    x = x_ref[...]
    ms = jnp.mean(x * x, axis=-1, keepdims=True)
    o_ref[...] = (x * lax.rsqrt(ms + NORM_EPS) * g_ref[...]).astype(o_ref.dtype)


def rmsnorm(x, g, out_dtype, tm=512):
    T, D = x.shape
    return pl.pallas_call(
        _rmsnorm_kernel,
        out_shape=jax.ShapeDtypeStruct((T, D), out_dtype),
        grid=(T // tm,),
        in_specs=[pl.BlockSpec((tm, D), lambda i: (i, 0)),
                  pl.BlockSpec((1, D), lambda i: (0, 0))],
        out_specs=pl.BlockSpec((tm, D), lambda i: (i, 0)),
        compiler_params=_cparams("parallel"),
        name="rmsnorm",
    )(x, g.reshape(1, D))


def _mm_kernel(a_ref, w_ref, *rest, nk, epilogue, n_extra):
    extras = rest[:n_extra]
    o_ref = rest[n_extra]

    def finish(acc):
        o_ref[...] = epilogue(acc, *[e[...] for e in extras]).astype(o_ref.dtype)

    if nk == 1:
        finish(_dot(a_ref[...], w_ref[...]))
    else:
        acc_ref = rest[n_extra + 1]
        k = pl.program_id(2)

        @pl.when(k == 0)
        def _():
            acc_ref[...] = jnp.zeros_like(acc_ref)

        acc_ref[...] += _dot(a_ref[...], w_ref[...])

        @pl.when(k == nk - 1)
        def _():
            finish(acc_ref[...])


def matmul(a, w, layer, *, out_dtype, tm, tn, tk=None, col0=0, n=None, epilogue=None, extras=(),
           alias_extra=None, name="matmul"):
    M, K = a.shape
    N = w.shape[2] - col0 if n is None else n
    tk = K if tk is None else tk
    nk = K // tk
    assert col0 % tn == 0 and N % tn == 0 and M % tm == 0 and K % tk == 0, (col0, N, tn, M, tm, K, tk)
    j0 = col0 // tn
    epilogue = epilogue or (lambda acc: acc)
    kern = functools.partial(_mm_kernel, nk=nk, epilogue=epilogue, n_extra=len(extras))
    in_specs = [pl.BlockSpec((tm, tk), lambda i, j, k: (i, k)),
                pl.BlockSpec((None, tk, tn), lambda i, j, k: (layer, k, j0 + j))]
    in_specs += [pl.BlockSpec((tm, tn), lambda i, j, k: (i, j)) for _ in extras]
    aliases = {} if alias_extra is None else {2 + alias_extra: 0}
    return pl.pallas_call(
        kern,
        out_shape=jax.ShapeDtypeStruct((M, N), out_dtype),
        grid=(M // tm, N // tn, nk),
        in_specs=in_specs,
        out_specs=pl.BlockSpec((tm, tn), lambda i, j, k: (i, j)),
        scratch_shapes=[pltpu.VMEM((tm, tn), F32)] if nk > 1 else [],
        input_output_aliases=aliases,
        compiler_params=_cparams("parallel", "parallel", "arbitrary"),
        name=name,
    )(a, w, *extras)


def _branch_kernel(ya_ref, yb_ref, yc_ref, p_ref, ga_ref, gb_ref, gc_ref, o_ref):
    acc = _sigmoid(ga_ref[...].astype(F32)) * _dot(ya_ref[...], p_ref[0])
    acc += _sigmoid(gb_ref[...].astype(F32)) * _dot(yb_ref[...], p_ref[1])
    acc += _sigmoid(gc_ref[...].astype(F32)) * _dot(yc_ref[...], p_ref[2])
    o_ref[...] = acc.astype(o_ref.dtype)


def branch_mix(ya, yb, yc, p, layer, zg, *, tm=1024, tn=1024):
    T, W = ya.shape
    D = p.shape[3]
    nj = D // tn
    y_spec = pl.BlockSpec((tm, W), lambda i, j: (i, 0))
    g_specs = [pl.BlockSpec((tm, tn), functools.partial(lambda i, j, n: (i, n * nj + j), n=n))
               for n in range(3)]
    return pl.pallas_call(
        _branch_kernel,
        out_shape=jax.ShapeDtypeStruct((T, D), BF16),
        grid=(T // tm, nj),
        in_specs=[y_spec, y_spec, y_spec,
                  pl.BlockSpec((None, 3, W, tn), lambda i, j: (layer, 0, 0, j))] + g_specs,
        out_specs=pl.BlockSpec((tm, tn), lambda i, j: (i, j)),
        compiler_params=_cparams("parallel", "parallel"),
        name="branch_mix",
    )(ya, yb, yc, p, zg, zg, zg)


def _rms_rows(h, g):
    return h * lax.rsqrt(jnp.mean(h * h, axis=-1, keepdims=True) + NORM_EPS) * g


def _wo_ln_kernel(a_ref, w_ref, h_ref, g_ref, h_out_ref, hn_ref):
    h = h_ref[...] + _dot(a_ref[...], w_ref[...])
    h_out_ref[...] = h
    hn_ref[...] = _rms_rows(h, g_ref[...]).astype(hn_ref.dtype)


def out_proj_norm(a, w, layer, h, g, *, tm=512):
    T, D = h.shape
    K = a.shape[1]
    row = lambda width: pl.BlockSpec((tm, width), lambda i: (i, 0))
    return pl.pallas_call(
        _wo_ln_kernel,
        out_shape=(jax.ShapeDtypeStruct((T, D), F32), jax.ShapeDtypeStruct((T, D), BF16)),
        grid=(T // tm,),
        in_specs=[row(K), pl.BlockSpec((None, K, D), lambda i: (layer, 0, 0)), row(D),
                  pl.BlockSpec((1, D), lambda i: (0, 0))],
        out_specs=(row(D), row(D)),
        input_output_aliases={2: 0},
        compiler_params=_cparams("parallel"),
        name="out_proj_norm",
    )(a, w, h, g.reshape(1, D))


def _ple_ln_kernel(h_ref, wpg_ref, p_ref, wpe_ref, g_ref, h_out_ref, hn_ref):
    h = h_ref[...]
    gate = _sigmoid(_dot(h.astype(BF16), wpg_ref[...]))
    h = h + gate * _dot(p_ref[...].astype(BF16), wpe_ref[...])
    h_out_ref[...] = h
    hn_ref[...] = _rms_rows(h, g_ref[...]).astype(hn_ref.dtype)


def ple_update_norm(h, wpg, p, wpe, layer, g, norm_dtype, *, tm=512):
    T, D = h.shape
    E = p.shape[2]
    row = lambda width: pl.BlockSpec((tm, width), lambda i: (i, 0))
    stacked = lambda r, c: pl.BlockSpec((None, r, c), lambda i: (layer, 0, 0))
    return pl.pallas_call(
        _ple_ln_kernel,
        out_shape=(jax.ShapeDtypeStruct((T, D), F32), jax.ShapeDtypeStruct((T, D), norm_dtype)),
        grid=(T // tm,),
        in_specs=[row(D), stacked(D, D), pl.BlockSpec((None, tm, E), lambda i: (layer, i, 0)), stacked(E, D),
                  pl.BlockSpec((1, D), lambda i: (0, 0))],
        out_specs=(row(D), row(D)),
        input_output_aliases={0: 0},
        compiler_params=_cparams("parallel"),
        name="ple_update_norm",
    )(h, wpg, p, wpe, g.reshape(1, D))


def _mla_q_kernel(cq_ref, g_ref, w_ref, c_ref, s_ref, o_ref, *, scale, n_heads):
    xn = _rms_rows(cq_ref[...], g_ref[...]).astype(BF16)
    x = _dot(xn, w_ref[...])
    c, s = c_ref[...], s_ref[...]
    for h in range(n_heads):
        lo = 2 * LANE * h
        hi = x[:, lo + QK_NOPE:lo + 2 * LANE]
        o_ref[:, lo:lo + QK_NOPE] = (x[:, lo:lo + QK_NOPE] * scale).astype(BF16)
        o_ref[:, lo + QK_NOPE:lo + 2 * LANE] = (hi * c + pltpu.roll(hi, LANE // 2, 1) * s).astype(BF16)


def mla_q_proj(z, g, w, ctab, stab, n_heads, scale, *, tm=512):
    T, R = z.shape[0], g.shape[-1]
    N = w.shape[1]
    cq = z
    return pl.pallas_call(
        functools.partial(_mla_q_kernel, scale=scale, n_heads=n_heads),
        out_shape=jax.ShapeDtypeStruct((T, N), BF16),
        grid=(T // tm,),
        in_specs=[pl.BlockSpec((tm, R), lambda i: (i, 0)),
                  pl.BlockSpec((1, R), lambda i: (0, 0)),
                  pl.BlockSpec((R, N), lambda i: (0, 0)),
                  pl.BlockSpec((tm, LANE), lambda i: (i, 0)),
                  pl.BlockSpec((tm, LANE), lambda i: (i, 0))],
        out_specs=pl.BlockSpec((tm, N), lambda i: (i, 0)),
        compiler_params=_cparams("parallel"),
        name="mla_q_proj",
    )(cq, g.reshape(1, R), w, ctab, stab)


def _mla_kv_kernel(z_ref, g_ref, w_ref, c_ref, s_ref, k_ref, v_ref, *, kv_lora, n_heads):
    xn = _rms_rows(z_ref[:, :kv_lora], g_ref[...]).astype(BF16)
    kv = _dot(xn, w_ref[...])
    kr = (z_ref[:, kv_lora:kv_lora + LANE] * c_ref[...] + z_ref[:, kv_lora + LANE:] * s_ref[...]).astype(BF16)
    for h in range(n_heads):
        lo = 2 * LANE * h
        k_ref[:, lo:lo + QK_NOPE] = kv[:, lo:lo + QK_NOPE].astype(BF16)
        k_ref[:, lo + QK_NOPE:lo + 2 * LANE] = kr
        v_ref[h * V_HEAD:(h + 1) * V_HEAD, :] = kv[:, lo + QK_NOPE:lo + 2 * LANE].T.astype(BF16)


def mla_kv_proj(zkv, col0, g, w, ctab, stab, n_heads, batch, *, tm=512):
    T = zkv.shape[0]
    R = g.shape[-1]
    ZW = R + 2 * LANE
    cb = col0 // ZW
    S = T // batch
    tm = min(tm, S)
    nsb = S // tm
    N = w.shape[1]
    return pl.pallas_call(
        functools.partial(_mla_kv_kernel, kv_lora=R, n_heads=n_heads),
        out_shape=(jax.ShapeDtypeStruct((T, n_heads * 2 * LANE), BF16),
                   jax.ShapeDtypeStruct((batch * n_heads * V_HEAD, S), BF16)),
        grid=(T // tm,),
        in_specs=[pl.BlockSpec((tm, ZW), lambda i: (i, cb)),
                  pl.BlockSpec((1, R), lambda i: (0, 0)),
                  pl.BlockSpec((R, N), lambda i: (0, 0)),
                  pl.BlockSpec((tm, LANE), lambda i: (i, 0)),
                  pl.BlockSpec((tm, LANE), lambda i: (i, 0))],
        out_specs=(pl.BlockSpec((tm, n_heads * 2 * LANE), lambda i: (i, 0)),
                   pl.BlockSpec((n_heads * V_HEAD, tm), lambda i: (i // nsb, i % nsb))),
        compiler_params=_cparams("parallel"),
        name="mla_kv_proj",
    )(zkv, g.reshape(1, R), w, ctab, stab)


ATTN_TILES = 8


def _attn_kernel(q_ref, k_ref, vt_ref, o_ref, *, tq):
    k = k_ref[...]
    tiles = [slice(j * tq, (j + 1) * tq) for j in range(ATTN_TILES)]
    scores = lambda t: _dot_nt(k, q_ref[t, :])

    def finish(t, st):
        p = jnp.exp2(st - jnp.max(st, axis=0, keepdims=True))
        ot = _dot(vt_ref[...], p.astype(BF16))
        o_ref[t, :] = (ot / jnp.sum(p, axis=0, keepdims=True)).T.astype(o_ref.dtype)

    st = scores(tiles[0])
    for j, t in enumerate(tiles):
        st_next = scores(tiles[j + 1]) if j + 1 < len(tiles) else None
        finish(t, st)
        st = st_next


def mla_attention(q, k, vt, batch, n_heads, *, tq=512):
    T = q.shape[0]
    S = T // batch
    tb = tq * ATTN_TILES
    nq = S // tb
    QW = q.shape[1] // n_heads
    return pl.pallas_call(
        functools.partial(_attn_kernel, tq=tq),
        out_shape=jax.ShapeDtypeStruct((T, n_heads * V_HEAD), BF16),
        grid=(batch, n_heads, nq),
        in_specs=[pl.BlockSpec((tb, QW), lambda b, h, i: (b * nq + i, h)),
                  pl.BlockSpec((S, QW), lambda b, h, i: (b, h)),
                  pl.BlockSpec((V_HEAD, S), lambda b, h, i: (b * n_heads + h, 0))],
        out_specs=pl.BlockSpec((tb, V_HEAD), lambda b, h, i: (b * nq + i, h)),
        compiler_params=_cparams("parallel", "parallel", "arbitrary"),
        name="mla_attention",
    )(q, k, vt)


def _rope_tables(positions):
    inv_freq = 1.0 / (ROPE_THETA ** (jnp.arange(0, QK_ROPE, 2, dtype=F32) / QK_ROPE))
    ang = positions.astype(F32).reshape(-1, 1) * inv_freq
    cos, sin = jnp.cos(ang), jnp.sin(ang)
    T = ang.shape[0]
    z64 = jnp.zeros((T, LANE - QK_ROPE), F32)
    ck = jnp.concatenate([cos, cos, z64], axis=1)
    sk = jnp.concatenate([-sin, sin, z64], axis=1)
    return ck * Q_SCALE, sk * Q_SCALE, ck, sk


def _mla_q_weights(w_uq, n_heads):
    R = w_uq.shape[0]
    w = w_uq.reshape(R, n_heads, QK_NOPE + QK_ROPE)
    half = QK_ROPE // 2
    w = jnp.concatenate([w, w[:, :, QK_NOPE + half:], w[:, :, QK_NOPE:QK_NOPE + half]], axis=2)
    return w.reshape(R, n_heads * 2 * LANE).astype(BF16)


N_LEVELS = 6
HGRN_BATCH = 8


def _gla_constants():
    C = CHUNK
    t = np.arange(C)[:, None]
    u = np.arange(C)[None, :]
    mk = np.zeros((N_LEVELS + 1, C, C), np.float32)
    for l in range(N_LEVELS):
        m = C >> (l + 1)
        mk[l] = (t // (2 * m) == u // (2 * m)) & ((t % (2 * m)) >= m) & ((u % (2 * m)) < m)
    mk[N_LEVELS] = t == u
    tri = (u <= t).astype(np.float32)
    to2d = lambda a: np.ascontiguousarray(a).reshape(-1, C)
    return tri, np.ascontiguousarray(tri[::-1, ::-1]), to2d(mk), to2d(mk[:, ::-1, ::-1])


def _level_exponent(b, g, m, rev, row):
    C = CHUNK
    if m >= 4:
        r = m if rev else m - 1
        b_mid = jnp.concatenate([jnp.broadcast_to(b[s + r:s + r + 1, :], (2 * m, b.shape[1]))
                                 for s in range(0, C, 2 * m)], axis=0)
        return -jnp.abs(b - b_mid)
    g_next = pltpu.roll(g, C - 1, 0)
    g_prev = pltpu.roll(g, 1, 0)
    if m == 2:
        p4 = row % 4
        if rev:
            return jnp.where(p4 == 3, g_prev, jnp.where(p4 == 2, 0.0, jnp.where(p4 == 1, g, g + g_next)))
        return jnp.where(p4 == 0, g_next, jnp.where(p4 == 1, 0.0, jnp.where(p4 == 2, g, g + g_prev)))
    return jnp.where((row % 2 == 1) != rev, g, 0.0)


def _gla_local(chunks):
    C = CHUNK
    row = lax.broadcasted_iota(jnp.int32, (C, 1), 0)
    gs = [c[3] * LOG2_E for c in chunks]
    parts = [_split3(g) for g in gs]
    bs = [_dot(c[4], p[0]) + _dot(c[4], p[1]) + _dot(c[4], p[2]) for c, p in zip(chunks, parts)]
    qbs = [c[0].astype(BF16) for c in chunks]
    kbs = [c[1].astype(BF16) for c in chunks]
    scores = [c[5][N_LEVELS * C:(N_LEVELS + 1) * C] * _dot_nt(qb, kb) for c, qb, kb in zip(chunks, qbs, kbs)]
    for l in range(N_LEVELS):
        pls = [jnp.exp2(_level_exponent(b, g, C >> (l + 1), c[6], row)).astype(BF16)
               for c, b, g in zip(chunks, bs, gs)]
        prods = [_dot_nt(qb * p, kb * p) for qb, kb, p in zip(qbs, kbs, pls)]
        scores = [s + c[5][l * C:(l + 1) * C] * a for s, c, a in zip(scores, chunks, prods)]
    out = []
    for c, b, s in zip(chunks, bs, scores):
        q, k, v, rev = c[0], c[1], c[2], c[6]
        b_tot = b[0:1] if rev else b[C - 1:C]
        vb = v.astype(BF16)
        out.append((_dot(s.astype(BF16), vb), (q * jnp.exp2(b)).astype(BF16),
                    _dot_tn(vb, (k * jnp.exp2(b_tot - b)).astype(BF16)), jnp.exp2(b_tot)))
    return out


def _hgrn_kernel(q_ref, zf_ref, zb_ref, i_ref, g_ref, lb_ref, ng_ref, trif_ref, trib_ref, mkf_ref, mkb_ref,
                 o_ref, of_ref, ob_ref, *, n_chunks):
    C = CHUNK
    lb = lb_ref[...]
    one_m_lb = 1.0 - lb
    trif, trib, mkf, mkb = trif_ref[...], trib_ref[...], mkf_ref[...], mkb_ref[...]

    def gate(zz):
        w = one_m_lb * _sigmoid(zz)
        return one_m_lb - w, jnp.log(jnp.maximum(lb + w, F_TINY))

    def load(ref, c):
        return ref[pl.ds(pl.multiple_of(c * C, C), C), :]

    def chunk_inputs(c, z_ref, tri, mk, rev):
        q = load(q_ref, c)
        k, g = gate(load(z_ref, c))
        return (q * _sigmoid(q), k, load(i_ref, c), g, tri, mk, rev)

    def body(it, carry):
        states = list(carry)
        ids = [[it * HGRN_BATCH + j for j in range(HGRN_BATCH)]]
        ids.append([n_chunks - 1 - c for c in ids[0]])
        loc = _gla_local([chunk_inputs(c, zf_ref, trif, mkf, False) for c in ids[0]]
                         + [chunk_inputs(c, zb_ref, trib, mkb, True) for c in ids[1]])
        for d, out_ref in enumerate((of_ref, ob_ref)):
            for j, c in enumerate(ids[d]):
                o_intra, q_dec, st_inc, p_tot = loc[d * HGRN_BATCH + j]
                out_ref[pl.ds(pl.multiple_of(c * C, C), C), :] = o_intra + _dot_nt(q_dec, states[d].astype(BF16))
                states[d] = states[d] * p_tot + st_inc
        return tuple(states)

    dv, dk = i_ref.shape[1], q_ref.shape[1]
    z = jnp.zeros((dv, dk), F32)
    lax.fori_loop(0, n_chunks // HGRN_BATCH, body, (z, z))

    o = of_ref[...] + ob_ref[...]
    ms = jnp.mean(o * o, axis=-1, keepdims=True)
    gg = g_ref[...]
    o_ref[...] = (o * lax.rsqrt(ms + NORM_EPS) * ng_ref[...] * (gg * _sigmoid(gg))).astype(o_ref.dtype)


def hgrn2_mixer(z, lb, norm_g, batch, *, col0=0):
    T = z.shape[0]
    W = lb.shape[-1]
    S = T // batch
    H = W // HEAD_B
    c0 = col0 // HEAD_B
    trif, trib, mkf, mkb = _gla_constants()
    part = lambda n: pl.BlockSpec((S, HEAD_B), functools.partial(lambda b, h, n: (b, c0 + n * H + h), n=n))
    const = lambda a: pl.BlockSpec(a.shape, lambda b, h: (0, 0))
    return pl.pallas_call(
        functools.partial(_hgrn_kernel, n_chunks=S // CHUNK),
        out_shape=jax.ShapeDtypeStruct((T, W), BF16),
        grid=(batch, H),
        in_specs=[part(0), part(1), part(2), part(3), part(4),
                  pl.BlockSpec((1, HEAD_B), lambda b, h: (0, h)),
                  pl.BlockSpec((1, HEAD_B), lambda b, h: (0, 0)),
                  const(trif), const(trib), const(mkf), const(mkb)],
        out_specs=pl.BlockSpec((S, HEAD_B), lambda b, h: (b, h)),
        scratch_shapes=[pltpu.VMEM((S, HEAD_B), F32), pltpu.VMEM((S, HEAD_B), F32)],
        compiler_params=_cparams("parallel", "parallel"),
        name="hgrn2_mixer",
    )(z, z, z, z, z, lb.reshape(1, W), norm_g.reshape(1, HEAD_B),
      jnp.asarray(trif, BF16), jnp.asarray(trib, BF16), jnp.asarray(mkf), jnp.asarray(mkb))


def _head_sum(x, bd):
    x1, x2, _ = _split3(x)
    return _dot(x1, bd) + _dot(x2, bd)


def _rwkv_prep_kernel(x_ref, xp_ref, xn_ref, mu_ref, w0_ref, w2_ref, a0_ref, a2_ref, g2_ref, kk_ref,
                      ka_ref, rk_ref, bd_ref,
                      r_ref, v_ref, kap_ref, kdf_ref, kdb_ref, alf_ref, alb_ref, lwf_ref, lwb_ref,
                      g_ref, bon_ref, *, tm, seq, width):
    W = width
    i = pl.program_id(0)
    x = x_ref[...]
    at_start = (i * tm) % seq == 0
    at_end = ((i + 1) * tm) % seq == 0
    prev_row = jnp.where(at_start, 0.0, xp_ref[7:8, :])
    next_row = jnp.where(at_end, 0.0, xn_ref[0:1, :])
    row = lax.broadcasted_iota(jnp.int32, (8, 1), 0)
    x_prev = pltpu.roll(x, 1, 0)
    x_prev = jnp.concatenate([jnp.where(row == 0, prev_row, x_prev[:8]), x_prev[8:]], axis=0)
    x_next = pltpu.roll(x, tm - 1, 0)
    x_next = jnp.concatenate([x_next[:tm - 8], jnp.where(row == 7, next_row, x_next[tm - 8:])], axis=0)
    u = x + mu_ref[0:1, :] * (x_prev - x) + mu_ref[1:2, :] * (x_next - x)

    r, k, v = u[:, :W], u[:, W:2 * W], u[:, 2 * W:3 * W]
    o = 3 * W
    wd = jnp.tanh(u[:, o:o + 2 * LORA_DECAY]).astype(BF16)
    o += 2 * LORA_DECAY
    ad = u[:, o:o + 2 * LORA_AAA].astype(BF16)
    o += 2 * LORA_AAA
    gd = _sigmoid(u[:, o:]).astype(BF16)

    bd = bd_ref[...]
    kkr = k * kk_ref[...]
    sq = kkr * kkr
    rks = []
    a_dir = []
    for n in range(2):
        w_raw = w0_ref[n:n + 1, :] + _dot(wd[:, n * LORA_DECAY:(n + 1) * LORA_DECAY], w2_ref[n])
        lw = -DECAY_SCALE * _sigmoid(w_raw)
        a = _sigmoid(a0_ref[n:n + 1, :] + _dot(ad[:, n * LORA_AAA:(n + 1) * LORA_AAA], a2_ref[n]))
        a_dir.append(a)
        (lwf_ref, lwb_ref)[n][...] = lw
    kd = [k * (1.0 + (a - 1.0) * ka_ref[...]) for a in a_dir]
    kdf_ref[...] = kd[0].astype(kdf_ref.dtype)
    kdb_ref[...] = kd[1].astype(kdb_ref.dtype)
    rkk = r * (kd[0] + kd[1]) * rk_ref[...]
    for c in range(W // LANE):
        sl = slice(c * LANE, (c + 1) * LANE)
        nrm = jnp.maximum(jnp.sqrt(_head_sum(sq[:, sl], bd)), 1e-12)
        kap = kkr[:, sl] / nrm
        kap_ref[:, sl] = kap.astype(kap_ref.dtype)
        alf_ref[:, sl] = (kap * a_dir[0][:, sl]).astype(alf_ref.dtype)
        alb_ref[:, sl] = (kap * a_dir[1][:, sl]).astype(alb_ref.dtype)
        bon_ref[:, sl] = _head_sum(rkk[:, sl], bd) * v[:, sl]
    r_ref[...] = r.astype(r_ref.dtype)
    v_ref[...] = v.astype(v_ref.dtype)
    g_ref[...] = _dot(gd, g2_ref[...]).astype(g_ref.dtype)


def rwkv_prep(z, mu, w0, w2, a0, a2, g2, k_k, k_a, r_k, batch, *, tm=256):
    T, ZW = z.shape
    W = w0.shape[-1]
    S = T // batch
    nb8 = tm // 8
    used = 3 * W + 2 * LORA_DECAY + 2 * LORA_AAA + LORA_GATE
    mu_p = jnp.pad(mu, ((0, 0), (0, ZW - used)))
    g2_p = jnp.pad(g2, ((0, ZW - used), (0, 0))).astype(BF16)
    bd = np.kron(np.eye(LANE // HEAD_A, dtype=np.float32), np.ones((HEAD_A, HEAD_A), np.float32))
    full = lambda a: pl.BlockSpec(a.shape, lambda i: (0,) * a.ndim)
    vec = lambda a: a.reshape(1, W)
    args = [z, z, z, mu_p, w0, w2.astype(BF16), a0, a2.astype(BF16), g2_p, vec(k_k), vec(k_a),
            vec(r_k), jnp.asarray(bd, BF16)]
    in_specs = [pl.BlockSpec((tm, ZW), lambda i: (i, 0)),
                pl.BlockSpec((8, ZW), lambda i: (jnp.maximum(i * nb8 - 1, 0), 0)),
                pl.BlockSpec((8, ZW), lambda i: (jnp.minimum((i + 1) * nb8, T // 8 - 1), 0))]
    in_specs += [full(a) for a in args[3:]]
    b16 = jax.ShapeDtypeStruct((T, W), BF16)
    f32 = jax.ShapeDtypeStruct((T, W), F32)
    return pl.pallas_call(
        functools.partial(_rwkv_prep_kernel, tm=tm, seq=S, width=W),
        out_shape=(b16,) * 7 + (f32, f32, b16, f32),
        grid=(T // tm,),
        in_specs=in_specs,
        out_specs=(pl.BlockSpec((tm, W), lambda i: (i, 0)),) * 11,
        compiler_params=_cparams("parallel"),
        name="rwkv_prep",
    )(*args)


RWKV_GROUP = 256


def _rwkv_constants():
    G, C = RWKV_GROUP, CHUNK
    t = np.arange(G)[:, None]
    u = np.arange(G)[None, :]
    dd = (((t // C) == (u // C)) & (u <= t)).astype(np.float32)
    return dd, np.ascontiguousarray(dd[::-1, ::-1])


def _bdot(a, b):
    return _dot(a.astype(BF16), b.astype(BF16))


def _unit_tri_inverses(ms, eye, diag_blocks):
    each = lambda f, *ls: [f(*a) for a in zip(*ls)]
    mds = [jnp.where(diag_blocks, m, 0.0) for m in ms]
    mos = each(lambda m, md: m - md, ms, mds)
    xs = [eye - md for md in mds]
    ps = each(_bdot, mds, mds)
    for _ in range(2):
        xs = each(lambda x, p: x + _bdot(x, p), xs, ps)
        ps = each(_bdot, ps, ps)
    tds = each(lambda x, p: x + _bdot(x, p), xs, ps)
    gs = each(_bdot, tds, mos)
    g2s = each(_bdot, gs, gs)
    ys = each(lambda g, g2: (eye - g) + _bdot(eye - g, g2), gs, g2s)
    return each(_bdot, ys, tds)


def _rwkv_groups(pairs, eye, diag_blocks, head_masks):
    G = pairs[0][0].shape[0]
    pre = []
    for r, v, kap, kd, al, lw, dd, rev, causal_incl, causal_strict in [
            (r, v, kap) + tuple(d) for r, v, kap, dirs in pairs for d in dirs]:
        l1, l2, _ = _split3(lw)
        b_in = _dot(dd, l1) + _dot(dd, l2)
        r_tot = 0 if rev else CHUNK - 1
        b_tot = jnp.concatenate([jnp.broadcast_to(b_in[s + r_tot:s + r_tot + 1], (CHUNK, b_in.shape[1]))
                                 for s in range(0, G, CHUNK)], axis=0)
        b_ex, b_out = b_in - lw, b_tot - b_in
        p_neg = jnp.exp(-b_in)
        p_out = jnp.exp(b_out)
        pre.append(dict(kap_h=kap * jnp.exp(b_ex), r_h=r * jnp.exp(b_in), al_n=(al * p_neg).astype(BF16),
                        kd_n=(kd * p_neg).astype(BF16), kout=kd * p_out, aout=al * p_out, p_tot=jnp.exp(b_tot),
                        incl=causal_incl, strict=causal_strict, v=v))
    chains = [(p, hm) for p in pre for hm in head_masks]
    kap_hb = [jnp.where(hm, p["kap_h"], 0.0).astype(BF16) for p, hm in chains]
    r_hb = [jnp.where(hm, p["r_h"], 0.0).astype(BF16) for p, hm in chains]
    v_h = [jnp.where(hm, p["v"], 0.0).astype(BF16) for p, hm in chains]
    ms = [jnp.where(p["strict"], _dot_nt(k_, p["al_n"]), 0.0) for (p, _), k_ in zip(chains, kap_hb)]
    ns = [jnp.where(p["strict"], _dot_nt(k_, p["kd_n"]), 0.0) for (p, _), k_ in zip(chains, kap_hb)]
    ras = [jnp.where(p["incl"], _dot_nt(r_, p["al_n"]), 0.0) for (p, _), r_ in zip(chains, r_hb)]
    rks = [jnp.where(p["incl"], _dot_nt(r_, p["kd_n"]), 0.0) for (p, _), r_ in zip(chains, r_hb)]
    nvs = [_bdot(n, vh).astype(BF16) for n, vh in zip(ns, v_h)]
    y0s = [_bdot(rk, vh) for rk, vh in zip(rks, v_h)]
    racs = [sum(ra[:, s:s + CHUNK] for s in range(0, G, CHUNK)) for ra in ras]
    tinvs = [t.astype(BF16) for t in _unit_tri_inverses(ms, eye, diag_blocks)]
    kw_uv = [_dot(t, jnp.concatenate([k_, nv], axis=1)) for t, k_, nv in zip(tinvs, kap_hb, nvs)]
    out = []
    for d, p in enumerate(pre):
        a, b = kw_uv[2 * d], kw_uv[2 * d + 1]
        out.append((a[:, :LANE] + b[:, :LANE], p["r_h"], jnp.concatenate(racs[2 * d:2 * d + 2], axis=1),
                    p["kout"], p["aout"], a[:, LANE:] + b[:, LANE:], y0s[2 * d] + y0s[2 * d + 1], p["p_tot"]))
    return out


N_LOCAL = 8
LOCAL_PAIRS = 2


def _rwkv_local_kernel(r_ref, v_ref, kap_ref, kdf_ref, kdb_ref, alf_ref, alb_ref, lwf_ref, lwb_ref,
                       ddf_ref, ddb_ref, *out_refs):
    G, C = RWKV_GROUP, CHUNK
    ti = lax.broadcasted_iota(jnp.int32, (G, G), 0)
    si = lax.broadcasted_iota(jnp.int32, (G, G), 1)
    same_chunk = (ti // C) == (si // C)
    eye = (ti == si).astype(F32)
    diag_blocks = (ti // 16) == (si // 16)
    lane = lax.broadcasted_iota(jnp.int32, (1, LANE), 1)
    head_masks = [lane < HEAD_A, lane >= HEAD_A]
    masks = []
    for d in range(2):
        masks.append((same_chunk & ((si <= ti) if d == 0 else (si >= ti)),
                      same_chunk & ((si < ti) if d == 0 else (si > ti))))
    pairs = []
    for pr in range(LOCAL_PAIRS):
        ln = slice(pr * LANE, (pr + 1) * LANE)
        dirs = [(kd_ref[:, ln], al_ref[:, ln], lw_ref[:, ln], dd_ref[...], d == 1) + masks[d]
                for d, (kd_ref, al_ref, lw_ref, dd_ref) in enumerate(((kdf_ref, alf_ref, lwf_ref, ddf_ref),
                                                                      (kdb_ref, alb_ref, lwb_ref, ddb_ref)))]
        pairs.append((r_ref[:, ln], v_ref[:, ln], kap_ref[:, ln], dirs))
    results = _rwkv_groups(pairs, eye, diag_blocks, head_masks)
    for pr in range(LOCAL_PAIRS):
        ln = slice(pr * LANE, (pr + 1) * LANE)
        for d in range(2):
            res = results[2 * pr + d]
            outs = out_refs[d * N_LOCAL:(d + 1) * N_LOCAL]
            for o_ref, val in zip(outs[:-1], res[:-1]):
                o_ref[:, ln] = val.astype(o_ref.dtype)
            p_tot = res[-1]
            outs[-1][:, ln] = jnp.concatenate([p_tot[c * C:c * C + 8] for c in range(G // C)], axis=0)


def rwkv_local(r, v, kap, kd_f, kd_b, al_f, al_b, lw_f, lw_b):
    T, W = r.shape
    G = RWKV_GROUP
    ddf, ddb = _rwkv_constants()
    bw = LANE * LOCAL_PAIRS
    blk = pl.BlockSpec((G, bw), lambda i, h: (i, h))
    pblk = pl.BlockSpec((G // 8, bw), lambda i, h: (i, h))
    const = lambda a: pl.BlockSpec(a.shape, lambda i, h: (0, 0))
    b16 = jax.ShapeDtypeStruct((T, W), BF16)
    per_dir = (b16,) * 7 + (jax.ShapeDtypeStruct((T // 8, W), F32),)
    outs = pl.pallas_call(
        _rwkv_local_kernel,
        out_shape=per_dir * 2,
        grid=(T // G, W // bw),
        in_specs=[blk] * 9 + [const(ddf), const(ddb)],
        out_specs=((blk,) * 7 + (pblk,)) * 2,
        compiler_params=_cparams("parallel", "parallel"),
        name="rwkv_local",
    )(r, v, kap, kd_f, kd_b, al_f, al_b, lw_f, lw_b,
      jnp.asarray(ddf, BF16), jnp.asarray(ddb, BF16))
    return outs[:N_LOCAL], outs[N_LOCAL:]


SCAN_PAIRS = 8


def _rwkv_scan_chunks(chains, head_masks, same_head):
    C = CHUNK
    m1s = [_dot_nt(jnp.concatenate([c[0], c[1]], axis=0), c[9].astype(BF16)) for c in chains]
    us = [m1[:C] + c[5] for m1, c in zip(m1s, chains)]
    incs = [_dot_tn(jnp.concatenate([c[7].astype(BF16), (-u).astype(BF16)], axis=0),
                    jnp.concatenate([c[3], c[4]], axis=0)) for u, c in zip(us, chains)]
    sts = [jnp.where(same_head, c[9] * c[8] + inc, 0.0) for c, inc in zip(chains, incs)]
    u_cats = [jnp.concatenate([jnp.where(hm, u, 0.0).astype(BF16) for hm in head_masks], axis=0) for u in us]
    ys = [m1[C:] + c[6] - _dot(c[2], u_cat) for m1, c, u_cat in zip(m1s, chains, u_cats)]
    return list(zip(ys, sts))


def _rwkv_scan_kernel(*refs, n_chunks, n_pairs):
    C = CHUNK
    f_refs, vf_ref = refs[:N_LOCAL], refs[N_LOCAL]
    b_refs, vb_ref = refs[N_LOCAL + 1:2 * N_LOCAL + 1], refs[2 * N_LOCAL + 1]
    yf_ref, yb_ref, stf_ref, stb_ref = refs[2 * N_LOCAL + 2:]

    @pl.when(pl.program_id(2) == 0)
    def _():
        stf_ref[...] = jnp.zeros_like(stf_ref)
        stb_ref[...] = jnp.zeros_like(stb_ref)

    lane = lax.broadcasted_iota(jnp.int32, (1, LANE), 1)
    head_masks = [lane < HEAD_A, lane >= HEAD_A]
    vi = lax.broadcasted_iota(jnp.int32, (LANE, LANE), 0)
    ki = lax.broadcasted_iota(jnp.int32, (LANE, LANE), 1)
    same_head = (vi // HEAD_A) == (ki // HEAD_A)

    def rows(ref, c, n):
        return ref[pl.ds(pl.multiple_of(c * n, n), n), :]

    def body(it, carry):
        states = [list(carry[0]), list(carry[1])]
        chains = []
        for d, (d_refs, v_ref) in enumerate(((f_refs, vf_ref), (b_refs, vb_ref))):
            c = it if d == 0 else n_chunks - 1 - it
            vals = [rows(ref, c, C) for ref in d_refs[:-1]] + [rows(v_ref, c, C), rows(d_refs[-1], c, 8)[0:1]]
            for p in range(n_pairs):
                chains.append(tuple(a[:, p * LANE:(p + 1) * LANE] for a in vals) + (states[d][p],))
        res = _rwkv_scan_chunks(chains, head_masks, same_head)
        for d, y_ref in enumerate((yf_ref, yb_ref)):
            c = it if d == 0 else n_chunks - 1 - it
            y_ref[pl.ds(pl.multiple_of(c * C, C), C), :] = jnp.concatenate(
                [res[d * n_pairs + p][0] for p in range(n_pairs)], axis=1).astype(y_ref.dtype)
            states[d] = [res[d * n_pairs + p][1] for p in range(n_pairs)]
        return tuple(states[0]), tuple(states[1])

    init = tuple(tuple(ref[p] for p in range(n_pairs)) for ref in (stf_ref, stb_ref))
    st_f, st_b = lax.fori_loop(0, n_chunks, body, init)
    for p in range(n_pairs):
        stf_ref[p] = st_f[p]
        stb_ref[p] = st_b[p]


def rwkv_scan(loc_f, loc_b, v, batch, *, ts=256):
    T, W = v.shape
    S = T // batch
    ts = min(ts, S)
    ns = S // ts
    bw = LANE * SCAN_PAIRS
    fwd = lambda rows: pl.BlockSpec((rows, bw), lambda b, h, s: (b * ns + s, h))
    bwd = lambda rows: pl.BlockSpec((rows, bw), lambda b, h, s: (b * ns + ns - 1 - s, h))
    specs = lambda mk: [mk(ts)] * (N_LOCAL - 1) + [mk(ts // 8), mk(ts)]
    out = jax.ShapeDtypeStruct((T, W), BF16)
    state = pltpu.VMEM((SCAN_PAIRS, LANE, LANE), F32)
    return pl.pallas_call(
        functools.partial(_rwkv_scan_kernel, n_chunks=ts // CHUNK, n_pairs=SCAN_PAIRS),
        out_shape=(out, out),
        grid=(batch, W // bw, ns),
        in_specs=specs(fwd) + specs(bwd),
        out_specs=(fwd(ts), bwd(ts)),
        scratch_shapes=[state, state],
        compiler_params=_cparams("parallel", "parallel", "arbitrary"),
        name="rwkv_scan",
    )(*loc_f, v, *loc_b, v)


def _rwkv_out_kernel(yf_ref, yb_ref, bon_ref, g_ref, gw_ref, gb_ref, bd_ref, o_ref, *, width):
    bd = bd_ref[...]
    inv_n = 1.0 / HEAD_A
    for c in range(width // LANE):
        sl = slice(c * LANE, (c + 1) * LANE)
        y = yf_ref[:, sl].astype(F32) + yb_ref[:, sl].astype(F32)
        mean = _head_sum(y, bd) * inv_n
        d = y - mean
        var = _head_sum(d * d, bd) * inv_n
        yn = d * lax.rsqrt(var + GN_EPS) * gw_ref[:, sl] + gb_ref[:, sl] + bon_ref[:, sl]
        o_ref[:, sl] = (yn * g_ref[:, sl]).astype(o_ref.dtype)


def rwkv_out(y_f, y_b, bonus, g, gn_w, gn_b, *, tm=512):
    T, W = y_f.shape
    bd = np.kron(np.eye(LANE // HEAD_A, dtype=np.float32), np.ones((HEAD_A, HEAD_A), np.float32))
    row = pl.BlockSpec((tm, W), lambda i: (i, 0))
    vec = pl.BlockSpec((1, W), lambda i: (0, 0))
    return pl.pallas_call(
        functools.partial(_rwkv_out_kernel, width=W),
        out_shape=jax.ShapeDtypeStruct((T, W), BF16),
        grid=(T // tm,),
        in_specs=[row, row, row, row, vec, vec, pl.BlockSpec((LANE, LANE), lambda i: (0, 0))],
        out_specs=row,
        compiler_params=_cparams("parallel"),
        name="rwkv_out",
    )(y_f, y_b, bonus, g, gn_w.reshape(1, W), gn_b.reshape(1, W), jnp.asarray(bd, BF16))


def rwkv7_mixer(z, mu, w0, w2, a0, a2, g2, k_k, k_a, r_k, gn_w, gn_b, batch):
    r, v, kap, kd_f, kd_b, al_f, al_b, lw_f, lw_b, g, bonus = rwkv_prep(
        z, mu, w0, w2, a0, a2, g2, k_k, k_a, r_k.reshape(-1), batch)
    loc_f, loc_b = rwkv_local(r, v, kap, kd_f, kd_b, al_f, al_b, lw_f, lw_b)
    y_f, y_b = rwkv_scan(loc_f, loc_b, v, batch)
    return rwkv_out(y_f, y_b, bonus, g, gn_w, gn_b)


def _relu2(acc):
    r = jnp.maximum(acc, 0.0)
    return r * r


def _add(acc, res):
    return acc + res


def kernel(x, p, positions, ln1_g, w_in, rwkv_mu, rwkv_w0, rwkv_w2, rwkv_a0, rwkv_a2, rwkv_g2, rwkv_kk, rwkv_ka, rwkv_rk, rwkv_gn_w, rwkv_gn_b, hgrn_lb, hgrn_norm_g, mla_q_norm_g, mla_kv_norm_g, mla_w_uq, mla_w_ukv, w_branch, w_o, ln2_g, w_mlp1, w_mlp2, w_pe, w_pg, final_g):
    Bn, S, D = x.shape
    L = w_in.shape[0]
    T = Bn * S
    W = rwkv_w0.shape[-1]
    q_lora, kv_lora = mla_q_norm_g.shape[-1], mla_kv_norm_g.shape[-1]
    n_heads_c = mla_w_ukv.shape[-1] // (QK_NOPE + V_HEAD)
    rwkv_w = 3 * W + 2 * LORA_DECAY + 2 * LORA_AAA + LORA_GATE
    hgrn_w = 5 * W
    o_hgrn = rwkv_w
    o_cq = o_hgrn + hgrn_w
    o_ckv = o_cq + q_lora
    o_kr = o_ckv + kv_lora
    o_gate = o_kr + QK_ROPE
    half = QK_ROPE // 2

    lb_w = jax.nn.softmax(hgrn_lb.astype(F32), axis=0)
    lower_bounds = jnp.cumsum(lb_w, axis=0) - lb_w[0]
    cq_tab, sq_tab, ck_tab, sk_tab = _rope_tables(positions)

    pad3 = lambda w, mult: jnp.pad(w, ((0, 0), (0, 0), (0, (-w.shape[2]) % mult)))
    w_kr = w_in[:, :, o_kr:o_gate]
    w_all = jnp.concatenate(
        [pad3(w_in[:, :, :rwkv_w], 1024), w_in[:, :, o_hgrn:o_cq], w_in[:, :, o_gate:], w_in[:, :, o_cq:o_kr],
         pad3(w_kr, LANE), pad3(jnp.concatenate([w_kr[:, :, half:], w_kr[:, :, :half]], axis=2), LANE)],
        axis=2).astype(BF16)
    n_rwkv = rwkv_w + (-rwkv_w) % 512
    n_hgrn, n_gate = hgrn_w, w_in.shape[2] - o_gate
    n_ckv = q_lora + kv_lora + 2 * LANE
    c_hgrn = rwkv_w + (-rwkv_w) % 1024
    c_gate, c_ckv = c_hgrn + n_hgrn, c_hgrn + n_hgrn + n_gate
    w_branch_b, w_o_b = w_branch.astype(BF16), w_o.astype(BF16)
    w_mlp1_b, w_mlp2_b = w_mlp1.astype(BF16), w_mlp2.astype(BF16)
    w_pg_b, w_pe_b = w_pg.astype(BF16), w_pe.astype(BF16)
    p2 = p.reshape(L, T, -1)

    h = x.reshape(T, D)
    hn = rmsnorm(h, ln1_g[0], BF16)
    for l in range(L):
        z_rwkv = matmul(hn, w_all, l, out_dtype=F32, tm=1024, tn=n_rwkv // 2, n=n_rwkv, name="in_rwkv")
        z_hgrn = matmul(hn, w_all, l, out_dtype=F32, tm=1024, tn=1024, col0=c_hgrn, n=n_hgrn, name="in_hgrn")
        z_gate = matmul(hn, w_all, l, out_dtype=BF16, tm=1024, tn=1024, col0=c_gate, n=n_gate, name="in_gate")
        z_ckv = matmul(hn, w_all, l, out_dtype=F32, tm=1024, tn=n_ckv // 2, col0=c_ckv, n=n_ckv, name="in_ckv")

        y_a = rwkv7_mixer(z_rwkv, rwkv_mu[l], rwkv_w0[l], rwkv_w2[l], rwkv_a0[l], rwkv_a2[l], rwkv_g2[l],
                          rwkv_kk[l], rwkv_ka[l], rwkv_rk[l], rwkv_gn_w[l], rwkv_gn_b[l], Bn)
        y_b = hgrn2_mixer(z_hgrn, lower_bounds[l], hgrn_norm_g[l], Bn)
        q = mla_q_proj(z_ckv, mla_q_norm_g[l], _mla_q_weights(mla_w_uq[l], n_heads_c), cq_tab, sq_tab,
                       n_heads_c, Q_SCALE)
        k, vt = mla_kv_proj(z_ckv, q_lora, mla_kv_norm_g[l], mla_w_ukv[l].astype(BF16), ck_tab, sk_tab,
                            n_heads_c, Bn)
        y_c = mla_attention(q, k, vt, Bn, n_heads_c)

        mixed = branch_mix(y_a, y_b, y_c, w_branch_b, l, z_gate)
        h, hn2 = out_proj_norm(mixed, w_o_b, l, h, ln2_g[l])
        hid = matmul(hn2, w_mlp1_b, l, out_dtype=BF16, tm=1024, tn=1024, epilogue=_relu2, name="mlp1")
        h = matmul(hid, w_mlp2_b, l, out_dtype=F32, tm=1024, tn=1024, tk=2048, epilogue=_add,
                   extras=(h,), alias_extra=0, name="mlp2")
        last = l == L - 1
        h, hn = ple_update_norm(h, w_pg_b, p2, w_pe_b, l, final_g if last else ln1_g[l + 1],
                                F32 if last else BF16)
    return hn.reshape(Bn, S, D)
```

```python
import functools

import numpy as np
import jax
import jax.numpy as jnp
from jax import lax
from jax.experimental import pallas as pl
from jax.experimental.pallas import tpu as pltpu

F32 = jnp.float32
BF16 = jnp.bfloat16

LANE = 128
VMEM_LIMIT = 48 * 2**20

HEAD_A = 64
LORA_DECAY = 64
LORA_AAA = 64
LORA_GATE = 160
DECAY_SCALE = 0.606531
GN_EPS = 64e-5
HEAD_B = 128
F_TINY = 1e-30
QK_NOPE = 128
QK_ROPE = 64
V_HEAD = 128
ROPE_THETA = 10000.0
NORM_EPS = 1e-6
CHUNK = 64
LOG2_E = 1.4426950408889634
Q_SCALE = (QK_NOPE + QK_ROPE) ** -0.5 * LOG2_E


def _cparams(*sem):
    return pltpu.CompilerParams(dimension_semantics=sem, vmem_limit_bytes=VMEM_LIMIT)


def _sigmoid(x):
    return 1.0 / (1.0 + jnp.exp(-x))


def _dot(a, b):
    return jnp.dot(a, b, preferred_element_type=F32)


def _dot_nt(a, b):
    return lax.dot_general(a, b, (((1,), (1,)), ((), ())), preferred_element_type=F32)


def _dot_tn(a, b):
    return lax.dot_general(a, b, (((0,), (0,)), ((), ())), preferred_element_type=F32)


def _split3(x):
    x1 = x.astype(BF16)
    r1 = x - x1.astype(F32)
    x2 = r1.astype(BF16)
    x3 = (r1 - x2.astype(F32)).astype(BF16)
    return x1, x2, x3


def _rmsnorm_kernel(x_ref, g_ref, o_ref):
    x = x_ref[...]
    ms = jnp.mean(x * x, axis=-1, keepdims=True)
    o_ref[...] = (x * lax.rsqrt(ms + NORM_EPS) * g_ref[...]).astype(o_ref.dtype)


def rmsnorm(x, g, out_dtype, tm=512):
    T, D = x.shape
    return pl.pallas_call(
        _rmsnorm_kernel,
        out_shape=jax.ShapeDtypeStruct((T, D), out_dtype),
        grid=(T // tm,),
        in_specs=[pl.BlockSpec((tm, D), lambda i: (i, 0)),
                  pl.BlockSpec((1, D), lambda i: (0, 0))],
        out_specs=pl.BlockSpec((tm, D), lambda i: (i, 0)),
        compiler_params=_cparams("parallel"),
        name="rmsnorm",
    )(x, g.reshape(1, D))


def _mm_kernel(a_ref, w_ref, *rest, nk, epilogue, n_extra):
    extras = rest[:n_extra]
    o_ref = rest[n_extra]

    def finish(acc):
        o_ref[...] = epilogue(acc, *[e[...] for e in extras]).astype(o_ref.dtype)

    if nk == 1:
        finish(_dot(a_ref[...], w_ref[...]))
    else:
        acc_ref = rest[n_extra + 1]
        k = pl.program_id(2)

        @pl.when(k == 0)
        def _():
            acc_ref[...] = jnp.zeros_like(acc_ref)

        acc_ref[...] += _dot(a_ref[...], w_ref[...])

        @pl.when(k == nk - 1)
        def _():
            finish(acc_ref[...])


def matmul(a, w, layer, *, out_dtype, tm, tn, tk=None, col0=0, n=None, epilogue=None, extras=(),
           alias_extra=None, name="matmul"):
    M, K = a.shape
    N = w.shape[2] - col0 if n is None else n
    tk = K if tk is None else tk
    nk = K // tk
    assert col0 % tn == 0 and N % tn == 0 and M % tm == 0 and K % tk == 0, (col0, N, tn, M, tm, K, tk)
    j0 = col0 // tn
    epilogue = epilogue or (lambda acc: acc)
    kern = functools.partial(_mm_kernel, nk=nk, epilogue=epilogue, n_extra=len(extras))
    in_specs = [pl.BlockSpec((tm, tk), lambda i, j, k: (i, k)),
                pl.BlockSpec((None, tk, tn), lambda i, j, k: (layer, k, j0 + j))]
    in_specs += [pl.BlockSpec((tm, tn), lambda i, j, k: (i, j)) for _ in extras]
    aliases = {} if alias_extra is None else {2 + alias_extra: 0}
    return pl.pallas_call(
        kern,
        out_shape=jax.ShapeDtypeStruct((M, N), out_dtype),
        grid=(M // tm, N // tn, nk),
        in_specs=in_specs,
        out_specs=pl.BlockSpec((tm, tn), lambda i, j, k: (i, j)),
        scratch_shapes=[pltpu.VMEM((tm, tn), F32)] if nk > 1 else [],
        input_output_aliases=aliases,
        compiler_params=_cparams("parallel", "parallel", "arbitrary"),
        name=name,
    )(a, w, *extras)


def _branch_kernel(ya_ref, yb_ref, yc_ref, p_ref, ga_ref, gb_ref, gc_ref, o_ref):
    acc = _sigmoid(ga_ref[...].astype(F32)) * _dot(ya_ref[...], p_ref[0])
    acc += _sigmoid(gb_ref[...].astype(F32)) * _dot(yb_ref[...], p_ref[1])
    acc += _sigmoid(gc_ref[...].astype(F32)) * _dot(yc_ref[...], p_ref[2])
    o_ref[...] = acc.astype(o_ref.dtype)


def branch_mix(ya, yb, yc, p, layer, zg, *, tm=1024, tn=1024):
    T, W = ya.shape
    D = p.shape[3]
    nj = D // tn
    y_spec = pl.BlockSpec((tm, W), lambda i, j: (i, 0))
    g_specs = [pl.BlockSpec((tm, tn), functools.partial(lambda i, j, n: (i, n * nj + j), n=n))
               for n in range(3)]
    return pl.pallas_call(
        _branch_kernel,
        out_shape=jax.ShapeDtypeStruct((T, D), BF16),
        grid=(T // tm, nj),
        in_specs=[y_spec, y_spec, y_spec,
                  pl.BlockSpec((None, 3, W, tn), lambda i, j: (layer, 0, 0, j))] + g_specs,
        out_specs=pl.BlockSpec((tm, tn), lambda i, j: (i, j)),
        compiler_params=_cparams("parallel", "parallel"),
        name="branch_mix",
    )(ya, yb, yc, p, zg, zg, zg)


def _rms_rows(h, g):
    return h * lax.rsqrt(jnp.mean(h * h, axis=-1, keepdims=True) + NORM_EPS) * g


def _wo_ln_kernel(a_ref, w_ref, h_ref, g_ref, h_out_ref, hn_ref):
    h = h_ref[...] + _dot(a_ref[...], w_ref[...])
    h_out_ref[...] = h
    hn_ref[...] = _rms_rows(h, g_ref[...]).astype(hn_ref.dtype)


def out_proj_norm(a, w, layer, h, g, *, tm=512):
    T, D = h.shape
    K = a.shape[1]
    row = lambda width: pl.BlockSpec((tm, width), lambda i: (i, 0))
    return pl.pallas_call(
        _wo_ln_kernel,
        out_shape=(jax.ShapeDtypeStruct((T, D), F32), jax.ShapeDtypeStruct((T, D), BF16)),
        grid=(T // tm,),
        in_specs=[row(K), pl.BlockSpec((None, K, D), lambda i: (layer, 0, 0)), row(D),
                  pl.BlockSpec((1, D), lambda i: (0, 0))],
        out_specs=(row(D), row(D)),
        input_output_aliases={2: 0},
        compiler_params=_cparams("parallel"),
        name="out_proj_norm",
    )(a, w, h, g.reshape(1, D))


def _ple_ln_kernel(h_ref, wpg_ref, p_ref, wpe_ref, g_ref, h_out_ref, hn_ref):
    h = h_ref[...]
    gate = _sigmoid(_dot(h.astype(BF16), wpg_ref[...]))
    h = h + gate * _dot(p_ref[...].astype(BF16), wpe_ref[...])
    h_out_ref[...] = h
    hn_ref[...] = _rms_rows(h, g_ref[...]).astype(hn_ref.dtype)


def ple_update_norm(h, wpg, p, wpe, layer, g, norm_dtype, *, tm=512):
    T, D = h.shape
    E = p.shape[2]
    row = lambda width: pl.BlockSpec((tm, width), lambda i: (i, 0))
    stacked = lambda r, c: pl.BlockSpec((None, r, c), lambda i: (layer, 0, 0))
    return pl.pallas_call(
        _ple_ln_kernel,
        out_shape=(jax.ShapeDtypeStruct((T, D), F32), jax.ShapeDtypeStruct((T, D), norm_dtype)),
        grid=(T // tm,),
        in_specs=[row(D), stacked(D, D), pl.BlockSpec((None, tm, E), lambda i: (layer, i, 0)), stacked(E, D),
                  pl.BlockSpec((1, D), lambda i: (0, 0))],
        out_specs=(row(D), row(D)),
        input_output_aliases={0: 0},
        compiler_params=_cparams("parallel"),
        name="ple_update_norm",
    )(h, wpg, p, wpe, g.reshape(1, D))


def _mla_q_kernel(cq_ref, g_ref, w_ref, c_ref, s_ref, o_ref, *, scale, n_heads):
    xn = _rms_rows(cq_ref[...], g_ref[...]).astype(BF16)
    x = _dot(xn, w_ref[...])
    c, s = c_ref[...], s_ref[...]
    for h in range(n_heads):
        lo = 2 * LANE * h
        hi = x[:, lo + QK_NOPE:lo + 2 * LANE]
        o_ref[:, lo:lo + QK_NOPE] = (x[:, lo:lo + QK_NOPE] * scale).astype(BF16)
        o_ref[:, lo + QK_NOPE:lo + 2 * LANE] = (hi * c + pltpu.roll(hi, LANE // 2, 1) * s).astype(BF16)


def mla_q_proj(z, g, w, ctab, stab, n_heads, scale, *, tm=512):
    T, R = z.shape[0], g.shape[-1]
    N = w.shape[1]
    return pl.pallas_call(
        functools.partial(_mla_q_kernel, scale=scale, n_heads=n_heads),
        out_shape=jax.ShapeDtypeStruct((T, N), BF16),
        grid=(T // tm,),
        in_specs=[pl.BlockSpec((tm, R), lambda i: (i, 0)),
                  pl.BlockSpec((1, R), lambda i: (0, 0)),
                  pl.BlockSpec((R, N), lambda i: (0, 0)),
                  pl.BlockSpec((tm, LANE), lambda i: (i, 0)),
                  pl.BlockSpec((tm, LANE), lambda i: (i, 0))],
        out_specs=pl.BlockSpec((tm, N), lambda i: (i, 0)),
        compiler_params=_cparams("parallel"),
        name="mla_q_proj",
    )(z, g.reshape(1, R), w, ctab, stab)


def _mla_kv_kernel(z_ref, g_ref, w_ref, c_ref, s_ref, k_ref, v_ref, *, kv_lora, n_heads):
    xn = _rms_rows(z_ref[:, :kv_lora], g_ref[...]).astype(BF16)
    kv = _dot(xn, w_ref[...])
    kr = (z_ref[:, kv_lora:kv_lora + LANE] * c_ref[...] + z_ref[:, kv_lora + LANE:] * s_ref[...]).astype(BF16)
    for h in range(n_heads):
        lo = 2 * LANE * h
        k_ref[:, lo:lo + QK_NOPE] = kv[:, lo:lo + QK_NOPE].astype(BF16)
        k_ref[:, lo + QK_NOPE:lo + 2 * LANE] = kr
        v_ref[h * V_HEAD:(h + 1) * V_HEAD, :] = kv[:, lo + QK_NOPE:lo + 2 * LANE].T.astype(BF16)


def mla_kv_proj(zkv, col0, g, w, ctab, stab, n_heads, batch, *, tm=512):
    T = zkv.shape[0]
    R = g.shape[-1]
    ZW = R + 2 * LANE
    cb = col0 // ZW
    S = T // batch
    tm = min(tm, S)
    nsb = S // tm
    N = w.shape[1]
    return pl.pallas_call(
        functools.partial(_mla_kv_kernel, kv_lora=R, n_heads=n_heads),
        out_shape=(jax.ShapeDtypeStruct((T, n_heads * 2 * LANE), BF16),
                   jax.ShapeDtypeStruct((batch * n_heads * V_HEAD, S), BF16)),
        grid=(T // tm,),
        in_specs=[pl.BlockSpec((tm, ZW), lambda i: (i, cb)),
                  pl.BlockSpec((1, R), lambda i: (0, 0)),
                  pl.BlockSpec((R, N), lambda i: (0, 0)),
                  pl.BlockSpec((tm, LANE), lambda i: (i, 0)),
                  pl.BlockSpec((tm, LANE), lambda i: (i, 0))],
        out_specs=(pl.BlockSpec((tm, n_heads * 2 * LANE), lambda i: (i, 0)),
                   pl.BlockSpec((n_heads * V_HEAD, tm), lambda i: (i // nsb, i % nsb))),
        compiler_params=_cparams("parallel"),
        name="mla_kv_proj",
    )(zkv, g.reshape(1, R), w, ctab, stab)


ATTN_TILES = 8


def _attn_kernel(q_ref, k_ref, vt_ref, o_ref, *, tq):
    k = k_ref[...]
    tiles = [slice(j * tq, (j + 1) * tq) for j in range(ATTN_TILES)]
    scores = lambda t: _dot_nt(k, q_ref[t, :])

    def finish(t, st):
        p = jnp.exp2(st - jnp.max(st, axis=0, keepdims=True))
        ot = _dot(vt_ref[...], p.astype(BF16))
        o_ref[t, :] = (ot / jnp.sum(p, axis=0, keepdims=True)).T.astype(o_ref.dtype)

    st = scores(tiles[0])
    for j, t in enumerate(tiles):
        st_next = scores(tiles[j + 1]) if j + 1 < len(tiles) else None
        finish(t, st)
        st = st_next


def mla_attention(q, k, vt, batch, n_heads, *, tq=512):
    T = q.shape[0]
    S = T // batch
    tb = tq * ATTN_TILES
    nq = S // tb
    QW = q.shape[1] // n_heads
    return pl.pallas_call(
        functools.partial(_attn_kernel, tq=tq),
        out_shape=jax.ShapeDtypeStruct((T, n_heads * V_HEAD), BF16),
        grid=(batch, n_heads, nq),
        in_specs=[pl.BlockSpec((tb, QW), lambda b, h, i: (b * nq + i, h)),
                  pl.BlockSpec((S, QW), lambda b, h, i: (b, h)),
                  pl.BlockSpec((V_HEAD, S), lambda b, h, i: (b * n_heads + h, 0))],
        out_specs=pl.BlockSpec((tb, V_HEAD), lambda b, h, i: (b * nq + i, h)),
        compiler_params=_cparams("parallel", "parallel", "arbitrary"),
        name="mla_attention",
    )(q, k, vt)


def _rope_tables(positions):
    inv_freq = 1.0 / (ROPE_THETA ** (jnp.arange(0, QK_ROPE, 2, dtype=F32) / QK_ROPE))
    ang = positions.astype(F32).reshape(-1, 1) * inv_freq
    cos, sin = jnp.cos(ang), jnp.sin(ang)
    T = ang.shape[0]
    z64 = jnp.zeros((T, LANE - QK_ROPE), F32)
    ck = jnp.concatenate([cos, cos, z64], axis=1)
    sk = jnp.concatenate([-sin, sin, z64], axis=1)
    return ck * Q_SCALE, sk * Q_SCALE, ck, sk


def _mla_q_weights(w_uq, n_heads):
    R = w_uq.shape[0]
    w = w_uq.reshape(R, n_heads, QK_NOPE + QK_ROPE)
    half = QK_ROPE // 2
    w = jnp.concatenate([w, w[:, :, QK_NOPE + half:], w[:, :, QK_NOPE:QK_NOPE + half]], axis=2)
    return w.reshape(R, n_heads * 2 * LANE).astype(BF16)


N_LEVELS = 6
HGRN_BATCH = 8


def _gla_constants():
    C = CHUNK
    t = np.arange(C)[:, None]
    u = np.arange(C)[None, :]
    mk = np.zeros((N_LEVELS + 1, C, C), np.float32)
    for l in range(N_LEVELS):
        m = C >> (l + 1)
        mk[l] = (t // (2 * m) == u // (2 * m)) & ((t % (2 * m)) >= m) & ((u % (2 * m)) < m)
    mk[N_LEVELS] = t == u
    tri = (u <= t).astype(np.float32)
    to2d = lambda a: np.ascontiguousarray(a).reshape(-1, C)
    return tri, np.ascontiguousarray(tri[::-1, ::-1]), to2d(mk), to2d(mk[:, ::-1, ::-1])


def _level_exponent(b, g, m, rev, row):
    C = CHUNK
    if m >= 4:
        r = m if rev else m - 1
        b_mid = jnp.concatenate([jnp.broadcast_to(b[s + r:s + r + 1, :], (2 * m, b.shape[1]))
                                 for s in range(0, C, 2 * m)], axis=0)
        return -jnp.abs(b - b_mid)
    g_next = pltpu.roll(g, C - 1, 0)
    g_prev = pltpu.roll(g, 1, 0)
    if m == 2:
        p4 = row % 4
        if rev:
            return jnp.where(p4 == 3, g_prev, jnp.where(p4 == 2, 0.0, jnp.where(p4 == 1, g, g + g_next)))
        return jnp.where(p4 == 0, g_next, jnp.where(p4 == 1, 0.0, jnp.where(p4 == 2, g, g + g_prev)))
    return jnp.where((row % 2 == 1) != rev, g, 0.0)


def _gla_local(chunks):
    C = CHUNK
    row = lax.broadcasted_iota(jnp.int32, (C, 1), 0)
    gs = [c[3] * LOG2_E for c in chunks]
    parts = [_split3(g) for g in gs]
    bs = [_dot(c[4], p[0]) + _dot(c[4], p[1]) + _dot(c[4], p[2]) for c, p in zip(chunks, parts)]
    qbs = [c[0].astype(BF16) for c in chunks]
    kbs = [c[1].astype(BF16) for c in chunks]
    scores = [c[5][N_LEVELS * C:(N_LEVELS + 1) * C] * _dot_nt(qb, kb) for c, qb, kb in zip(chunks, qbs, kbs)]
    for l in range(N_LEVELS):
        pls = [jnp.exp2(_level_exponent(b, g, C >> (l + 1), c[6], row)).astype(BF16)
               for c, b, g in zip(chunks, bs, gs)]
        prods = [_dot_nt(qb * p, kb * p) for qb, kb, p in zip(qbs, kbs, pls)]
        scores = [s + c[5][l * C:(l + 1) * C] * a for s, c, a in zip(scores, chunks, prods)]
    out = []
    for c, b, s in zip(chunks, bs, scores):
        q, k, v, rev = c[0], c[1], c[2], c[6]
        b_tot = b[0:1] if rev else b[C - 1:C]
        vb = v.astype(BF16)
        out.append((_dot(s.astype(BF16), vb), (q * jnp.exp2(b)).astype(BF16),
                    _dot_tn(vb, (k * jnp.exp2(b_tot - b)).astype(BF16)), jnp.exp2(b_tot)))
    return out


def _hgrn_kernel(q_ref, zf_ref, zb_ref, i_ref, g_ref, lb_ref, ng_ref, trif_ref, trib_ref, mkf_ref, mkb_ref,
                 o_ref, of_ref, ob_ref, *, n_chunks):
    C = CHUNK
    lb = lb_ref[...]
    one_m_lb = 1.0 - lb
    trif, trib, mkf, mkb = trif_ref[...], trib_ref[...], mkf_ref[...], mkb_ref[...]

    def gate(zz):
        w = one_m_lb * _sigmoid(zz)
        return one_m_lb - w, jnp.log(jnp.maximum(lb + w, F_TINY))

    def load(ref, c):
        return ref[pl.ds(pl.multiple_of(c * C, C), C), :]

    def chunk_inputs(c, z_ref, tri, mk, rev):
        q = load(q_ref, c)
        k, g = gate(load(z_ref, c))
        return (q * _sigmoid(q), k, load(i_ref, c), g, tri, mk, rev)

    def body(it, carry):
        states = list(carry)
        ids = [[it * HGRN_BATCH + j for j in range(HGRN_BATCH)]]
        ids.append([n_chunks - 1 - c for c in ids[0]])
        loc = _gla_local([chunk_inputs(c, zf_ref, trif, mkf, False) for c in ids[0]]
                         + [chunk_inputs(c, zb_ref, trib, mkb, True) for c in ids[1]])
        for d, out_ref in enumerate((of_ref, ob_ref)):
            for j, c in enumerate(ids[d]):
                o_intra, q_dec, st_inc, p_tot = loc[d * HGRN_BATCH + j]
                out_ref[pl.ds(pl.multiple_of(c * C, C), C), :] = o_intra + _dot_nt(q_dec, states[d].astype(BF16))
                states[d] = states[d] * p_tot + st_inc
        return tuple(states)

    dv, dk = i_ref.shape[1], q_ref.shape[1]
    z = jnp.zeros((dv, dk), F32)
    lax.fori_loop(0, n_chunks // HGRN_BATCH, body, (z, z))

    o = of_ref[...] + ob_ref[...]
    ms = jnp.mean(o * o, axis=-1, keepdims=True)
    gg = g_ref[...]
    o_ref[...] = (o * lax.rsqrt(ms + NORM_EPS) * ng_ref[...] * (gg * _sigmoid(gg))).astype(o_ref.dtype)


def hgrn2_mixer(z, lb, norm_g, batch, *, col0=0):
    T = z.shape[0]
    W = lb.shape[-1]
    S = T // batch
    H = W // HEAD_B
    c0 = col0 // HEAD_B
    trif, trib, mkf, mkb = _gla_constants()
    part = lambda n: pl.BlockSpec((S, HEAD_B), functools.partial(lambda b, h, n: (b, c0 + n * H + h), n=n))
    const = lambda a: pl.BlockSpec(a.shape, lambda b, h: (0, 0))
    return pl.pallas_call(
        functools.partial(_hgrn_kernel, n_chunks=S // CHUNK),
        out_shape=jax.ShapeDtypeStruct((T, W), BF16),
        grid=(batch, H),
        in_specs=[part(0), part(1), part(2), part(3), part(4),
                  pl.BlockSpec((1, HEAD_B), lambda b, h: (0, h)),
                  pl.BlockSpec((1, HEAD_B), lambda b, h: (0, 0)),
                  const(trif), const(trib), const(mkf), const(mkb)],
        out_specs=pl.BlockSpec((S, HEAD_B), lambda b, h: (b, h)),
        scratch_shapes=[pltpu.VMEM((S, HEAD_B), F32), pltpu.VMEM((S, HEAD_B), F32)],
        compiler_params=_cparams("parallel", "parallel"),
        name="hgrn2_mixer",
    )(z, z, z, z, z, lb.reshape(1, W), norm_g.reshape(1, HEAD_B),
      jnp.asarray(trif, BF16), jnp.asarray(trib, BF16), jnp.asarray(mkf), jnp.asarray(mkb))


def _head_sum(x, bd):
    x1, x2, _ = _split3(x)
    return _dot(x1, bd) + _dot(x2, bd)


def _rwkv_prep_kernel(x_ref, xp_ref, xn_ref, mu_ref, w0_ref, w2_ref, a0_ref, a2_ref, g2_ref, kk_ref,
                      ka_ref, rk_ref, bd_ref,
                      r_ref, v_ref, kap_ref, kdf_ref, kdb_ref, alf_ref, alb_ref, lwf_ref, lwb_ref,
                      g_ref, bon_ref, *, tm, seq, width):
    W = width
    i = pl.program_id(0)
    x = x_ref[...]
    at_start = (i * tm) % seq == 0
    at_end = ((i + 1) * tm) % seq == 0
    prev_row = jnp.where(at_start, 0.0, xp_ref[7:8, :])
    next_row = jnp.where(at_end, 0.0, xn_ref[0:1, :])
    row = lax.broadcasted_iota(jnp.int32, (8, 1), 0)
    x_prev = pltpu.roll(x, 1, 0)
    x_prev = jnp.concatenate([jnp.where(row == 0, prev_row, x_prev[:8]), x_prev[8:]], axis=0)
    x_next = pltpu.roll(x, tm - 1, 0)
    x_next = jnp.concatenate([x_next[:tm - 8], jnp.where(row == 7, next_row, x_next[tm - 8:])], axis=0)
    u = x + mu_ref[0:1, :] * (x_prev - x) + mu_ref[1:2, :] * (x_next - x)

    r, k, v = u[:, :W], u[:, W:2 * W], u[:, 2 * W:3 * W]
    o = 3 * W
    wd = jnp.tanh(u[:, o:o + 2 * LORA_DECAY]).astype(BF16)
    o += 2 * LORA_DECAY
    ad = u[:, o:o + 2 * LORA_AAA].astype(BF16)
    o += 2 * LORA_AAA
    gd = _sigmoid(u[:, o:]).astype(BF16)

    bd = bd_ref[...]
    kkr = k * kk_ref[...]
    sq = kkr * kkr
    a_dir = []
    for n in range(2):
        w_raw = w0_ref[n:n + 1, :] + _dot(wd[:, n * LORA_DECAY:(n + 1) * LORA_DECAY], w2_ref[n])
        lw = -DECAY_SCALE * _sigmoid(w_raw)
        a = _sigmoid(a0_ref[n:n + 1, :] + _dot(ad[:, n * LORA_AAA:(n + 1) * LORA_AAA], a2_ref[n]))
        a_dir.append(a)
        (lwf_ref, lwb_ref)[n][...] = lw
    kd = [k * (1.0 + (a - 1.0) * ka_ref[...]) for a in a_dir]
    kdf_ref[...] = kd[0].astype(kdf_ref.dtype)
    kdb_ref[...] = kd[1].astype(kdb_ref.dtype)
    rkk = r * (kd[0] + kd[1]) * rk_ref[...]
    for c in range(W // LANE):
        sl = slice(c * LANE, (c + 1) * LANE)
        nrm = jnp.maximum(jnp.sqrt(_head_sum(sq[:, sl], bd)), 1e-12)
        kap = kkr[:, sl] / nrm
        kap_ref[:, sl] = kap.astype(kap_ref.dtype)
        alf_ref[:, sl] = (kap * a_dir[0][:, sl]).astype(alf_ref.dtype)
        alb_ref[:, sl] = (kap * a_dir[1][:, sl]).astype(alb_ref.dtype)
        bon_ref[:, sl] = _head_sum(rkk[:, sl], bd) * v[:, sl]
    r_ref[...] = r.astype(r_ref.dtype)
    v_ref[...] = v.astype(v_ref.dtype)
    g_ref[...] = _dot(gd, g2_ref[...]).astype(g_ref.dtype)


def rwkv_prep(z, mu, w0, w2, a0, a2, g2, k_k, k_a, r_k, batch, *, tm=256):
    T, ZW = z.shape
    W = w0.shape[-1]
    S = T // batch
    nb8 = tm // 8
    used = 3 * W + 2 * LORA_DECAY + 2 * LORA_AAA + LORA_GATE
    mu_p = jnp.pad(mu, ((0, 0), (0, ZW - used)))
    g2_p = jnp.pad(g2, ((0, ZW - used), (0, 0))).astype(BF16)
    bd = np.kron(np.eye(LANE // HEAD_A, dtype=np.float32), np.ones((HEAD_A, HEAD_A), np.float32))
    full = lambda a: pl.BlockSpec(a.shape, lambda i: (0,) * a.ndim)
    vec = lambda a: a.reshape(1, W)
    args = [z, z, z, mu_p, w0, w2.astype(BF16), a0, a2.astype(BF16), g2_p, vec(k_k), vec(k_a),
            vec(r_k), jnp.asarray(bd, BF16)]
    in_specs = [pl.BlockSpec((tm, ZW), lambda i: (i, 0)),
                pl.BlockSpec((8, ZW), lambda i: (jnp.maximum(i * nb8 - 1, 0), 0)),
                pl.BlockSpec((8, ZW), lambda i: (jnp.minimum((i + 1) * nb8, T // 8 - 1), 0))]
    in_specs += [full(a) for a in args[3:]]
    b16 = jax.ShapeDtypeStruct((T, W), BF16)
    f32 = jax.ShapeDtypeStruct((T, W), F32)
    return pl.pallas_call(
        functools.partial(_rwkv_prep_kernel, tm=tm, seq=S, width=W),
        out_shape=(b16,) * 7 + (f32, f32, b16, f32),
        grid=(T // tm,),
        in_specs=in_specs,
        out_specs=(pl.BlockSpec((tm, W), lambda i: (i, 0)),) * 11,
        compiler_params=_cparams("parallel"),
        name="rwkv_prep",
    )(*args)


RWKV_GROUP = 256


def _rwkv_constants():
    G, C = RWKV_GROUP, CHUNK
    t = np.arange(G)[:, None]
    u = np.arange(G)[None, :]
    dd = (((t // C) == (u // C)) & (u <= t)).astype(np.float32)
    return dd, np.ascontiguousarray(dd[::-1, ::-1])


def _bdot(a, b):
    return _dot(a.astype(BF16), b.astype(BF16))


def _unit_tri_inverses(ms, eye, diag_blocks):
    each = lambda f, *ls: [f(*a) for a in zip(*ls)]
    mds = [jnp.where(diag_blocks, m, 0.0) for m in ms]
    mos = each(lambda m, md: m - md, ms, mds)
    xs = [eye - md for md in mds]
    ps = each(_bdot, mds, mds)
    for _ in range(2):
        xs = each(lambda x, p: x + _bdot(x, p), xs, ps)
        ps = each(_bdot, ps, ps)
    tds = each(lambda x, p: x + _bdot(x, p), xs, ps)
    gs = each(_bdot, tds, mos)
    g2s = each(_bdot, gs, gs)
    ys = each(lambda g, g2: (eye - g) + _bdot(eye - g, g2), gs, g2s)
    return each(_bdot, ys, tds)


def _rwkv_groups(pairs, eye, diag_blocks, head_masks):
    G = pairs[0][0].shape[0]
    pre = []
    for r, v, kap, kd, al, lw, dd, rev, causal_incl, causal_strict in [
            (r, v, kap) + tuple(d) for r, v, kap, dirs in pairs for d in dirs]:
        l1, l2, _ = _split3(lw)
        b_in = _dot(dd, l1) + _dot(dd, l2)
        r_tot = 0 if rev else CHUNK - 1
        b_tot = jnp.concatenate([jnp.broadcast_to(b_in[s + r_tot:s + r_tot + 1], (CHUNK, b_in.shape[1]))
                                 for s in range(0, G, CHUNK)], axis=0)
        b_ex, b_out = b_in - lw, b_tot - b_in
        p_neg = jnp.exp(-b_in)
        p_out = jnp.exp(b_out)
        pre.append(dict(kap_h=kap * jnp.exp(b_ex), r_h=r * jnp.exp(b_in), al_n=(al * p_neg).astype(BF16),
                        kd_n=(kd * p_neg).astype(BF16), kout=kd * p_out, aout=al * p_out, p_tot=jnp.exp(b_tot),
                        incl=causal_incl, strict=causal_strict, v=v))
    chains = [(p, hm) for p in pre for hm in head_masks]
    kap_hb = [jnp.where(hm, p["kap_h"], 0.0).astype(BF16) for p, hm in chains]
    r_hb = [jnp.where(hm, p["r_h"], 0.0).astype(BF16) for p, hm in chains]
    v_h = [jnp.where(hm, p["v"], 0.0).astype(BF16) for p, hm in chains]
    ms = [jnp.where(p["strict"], _dot_nt(k_, p["al_n"]), 0.0) for (p, _), k_ in zip(chains, kap_hb)]
    ns = [jnp.where(p["strict"], _dot_nt(k_, p["kd_n"]), 0.0) for (p, _), k_ in zip(chains, kap_hb)]
    ras = [jnp.where(p["incl"], _dot_nt(r_, p["al_n"]), 0.0) for (p, _), r_ in zip(chains, r_hb)]
    rks = [jnp.where(p["incl"], _dot_nt(r_, p["kd_n"]), 0.0) for (p, _), r_ in zip(chains, r_hb)]
    nvs = [_bdot(n, vh).astype(BF16) for n, vh in zip(ns, v_h)]
    y0s = [_bdot(rk, vh) for rk, vh in zip(rks, v_h)]
    racs = [sum(ra[:, s:s + CHUNK] for s in range(0, G, CHUNK)) for ra in ras]
    tinvs = [t.astype(BF16) for t in _unit_tri_inverses(ms, eye, diag_blocks)]
    kw_uv = [_dot(t, jnp.concatenate([k_, nv], axis=1)) for t, k_, nv in zip(tinvs, kap_hb, nvs)]
    out = []
    for d, p in enumerate(pre):
        a, b = kw_uv[2 * d], kw_uv[2 * d + 1]
        out.append((a[:, :LANE] + b[:, :LANE], p["r_h"], jnp.concatenate(racs[2 * d:2 * d + 2], axis=1),
                    p["kout"], p["aout"], a[:, LANE:] + b[:, LANE:], y0s[2 * d] + y0s[2 * d + 1], p["p_tot"]))
    return out


N_LOCAL = 8
LOCAL_PAIRS = 2


def _rwkv_local_kernel(r_ref, v_ref, kap_ref, kdf_ref, kdb_ref, alf_ref, alb_ref, lwf_ref, lwb_ref,
                       ddf_ref, ddb_ref, *out_refs):
    G, C = RWKV_GROUP, CHUNK
    ti = lax.broadcasted_iota(jnp.int32, (G, G), 0)
    si = lax.broadcasted_iota(jnp.int32, (G, G), 1)
    same_chunk = (ti // C) == (si // C)
    eye = (ti == si).astype(F32)
    diag_blocks = (ti // 16) == (si // 16)
    lane = lax.broadcasted_iota(jnp.int32, (1, LANE), 1)
    head_masks = [lane < HEAD_A, lane >= HEAD_A]
    masks = []
    for d in range(2):
        masks.append((same_chunk & ((si <= ti) if d == 0 else (si >= ti)),
                      same_chunk & ((si < ti) if d == 0 else (si > ti))))
    pairs = []
    for pr in range(LOCAL_PAIRS):
        ln = slice(pr * LANE, (pr + 1) * LANE)
        dirs = [(kd_ref[:, ln], al_ref[:, ln], lw_ref[:, ln], dd_ref[...], d == 1) + masks[d]
                for d, (kd_ref, al_ref, lw_ref, dd_ref) in enumerate(((kdf_ref, alf_ref, lwf_ref, ddf_ref),
                                                                      (kdb_ref, alb_ref, lwb_ref, ddb_ref)))]
        pairs.append((r_ref[:, ln], v_ref[:, ln], kap_ref[:, ln], dirs))
    results = _rwkv_groups(pairs, eye, diag_blocks, head_masks)
    for pr in range(LOCAL_PAIRS):
        ln = slice(pr * LANE, (pr + 1) * LANE)
        for d in range(2):
            res = results[2 * pr + d]
            outs = out_refs[d * N_LOCAL:(d + 1) * N_LOCAL]
            for o_ref, val in zip(outs[:-1], res[:-1]):
                o_ref[:, ln] = val.astype(o_ref.dtype)
            p_tot = res[-1]
            outs[-1][:, ln] = jnp.concatenate([p_tot[c * C:c * C + 8] for c in range(G // C)], axis=0)


def rwkv_local(r, v, kap, kd_f, kd_b, al_f, al_b, lw_f, lw_b):
    T, W = r.shape
    G = RWKV_GROUP
    ddf, ddb = _rwkv_constants()
    bw = LANE * LOCAL_PAIRS
    blk = pl.BlockSpec((G, bw), lambda i, h: (i, h))
    pblk = pl.BlockSpec((G // 8, bw), lambda i, h: (i, h))
    const = lambda a: pl.BlockSpec(a.shape, lambda i, h: (0, 0))
    b16 = jax.ShapeDtypeStruct((T, W), BF16)
    per_dir = (b16,) * 7 + (jax.ShapeDtypeStruct((T // 8, W), F32),)
    outs = pl.pallas_call(
        _rwkv_local_kernel,
        out_shape=per_dir * 2,
        grid=(T // G, W // bw),
        in_specs=[blk] * 9 + [const(ddf), const(ddb)],
        out_specs=((blk,) * 7 + (pblk,)) * 2,
        compiler_params=_cparams("parallel", "parallel"),
        name="rwkv_local",
    )(r, v, kap, kd_f, kd_b, al_f, al_b, lw_f, lw_b,
      jnp.asarray(ddf, BF16), jnp.asarray(ddb, BF16))
    return outs[:N_LOCAL], outs[N_LOCAL:]


SCAN_PAIRS = 8


def _rwkv_scan_chunks(chains, head_masks, same_head):
    C = CHUNK
    m1s = [_dot_nt(jnp.concatenate([c[0], c[1]], axis=0), c[9].astype(BF16)) for c in chains]
    us = [m1[:C] + c[5] for m1, c in zip(m1s, chains)]
    incs = [_dot_tn(jnp.concatenate([c[7].astype(BF16), (-u).astype(BF16)], axis=0),
                    jnp.concatenate([c[3], c[4]], axis=0)) for u, c in zip(us, chains)]
    sts = [jnp.where(same_head, c[9] * c[8] + inc, 0.0) for c, inc in zip(chains, incs)]
    u_cats = [jnp.concatenate([jnp.where(hm, u, 0.0).astype(BF16) for hm in head_masks], axis=0) for u in us]
    ys = [m1[C:] + c[6] - _dot(c[2], u_cat) for m1, c, u_cat in zip(m1s, chains, u_cats)]
    return list(zip(ys, sts))


def _rwkv_scan_kernel(*refs, n_chunks, n_pairs):
    C = CHUNK
    f_refs, vf_ref = refs[:N_LOCAL], refs[N_LOCAL]
    b_refs, vb_ref = refs[N_LOCAL + 1:2 * N_LOCAL + 1], refs[2 * N_LOCAL + 1]
    yf_ref, yb_ref, stf_ref, stb_ref = refs[2 * N_LOCAL + 2:]

    @pl.when(pl.program_id(2) == 0)
    def _():
        stf_ref[...] = jnp.zeros_like(stf_ref)
        stb_ref[...] = jnp.zeros_like(stb_ref)

    lane = lax.broadcasted_iota(jnp.int32, (1, LANE), 1)
    head_masks = [lane < HEAD_A, lane >= HEAD_A]
    vi = lax.broadcasted_iota(jnp.int32, (LANE, LANE), 0)
    ki = lax.broadcasted_iota(jnp.int32, (LANE, LANE), 1)
    same_head = (vi // HEAD_A) == (ki // HEAD_A)

    def rows(ref, c, n):
        return ref[pl.ds(pl.multiple_of(c * n, n), n), :]

    def body(it, carry):
        states = [list(carry[0]), list(carry[1])]
        chains = []
        for d, (d_refs, v_ref) in enumerate(((f_refs, vf_ref), (b_refs, vb_ref))):
            c = it if d == 0 else n_chunks - 1 - it
            vals = [rows(ref, c, C) for ref in d_refs[:-1]] + [rows(v_ref, c, C), rows(d_refs[-1], c, 8)[0:1]]
            for p in range(n_pairs):
                chains.append(tuple(a[:, p * LANE:(p + 1) * LANE] for a in vals) + (states[d][p],))
        res = _rwkv_scan_chunks(chains, head_masks, same_head)
        for d, y_ref in enumerate((yf_ref, yb_ref)):
            c = it if d == 0 else n_chunks - 1 - it
            y_ref[pl.ds(pl.multiple_of(c * C, C), C), :] = jnp.concatenate(
                [res[d * n_pairs + p][0] for p in range(n_pairs)], axis=1).astype(y_ref.dtype)
            states[d] = [res[d * n_pairs + p][1] for p in range(n_pairs)]
        return tuple(states[0]), tuple(states[1])

    init = tuple(tuple(ref[p] for p in range(n_pairs)) for ref in (stf_ref, stb_ref))
    st_f, st_b = lax.fori_loop(0, n_chunks, body, init)
    for p in range(n_pairs):
        stf_ref[p] = st_f[p]
        stb_ref[p] = st_b[p]


def rwkv_scan(loc_f, loc_b, v, batch, *, ts=256):
    T, W = v.shape
    S = T // batch
    ts = min(ts, S)
    ns = S // ts
    bw = LANE * SCAN_PAIRS
    fwd = lambda rows: pl.BlockSpec((rows, bw), lambda b, h, s: (b * ns + s, h))
    bwd = lambda rows: pl.BlockSpec((rows, bw), lambda b, h, s: (b * ns + ns - 1 - s, h))
    specs = lambda mk: [mk(ts)] * (N_LOCAL - 1) + [mk(ts // 8), mk(ts)]
    out = jax.ShapeDtypeStruct((T, W), BF16)
    state = pltpu.VMEM((SCAN_PAIRS, LANE, LANE), F32)
    return pl.pallas_call(
        functools.partial(_rwkv_scan_kernel, n_chunks=ts // CHUNK, n_pairs=SCAN_PAIRS),
        out_shape=(out, out),
        grid=(batch, W // bw, ns),
        in_specs=specs(fwd) + specs(bwd),
        out_specs=(fwd(ts), bwd(ts)),
        scratch_shapes=[state, state],
        compiler_params=_cparams("parallel", "parallel", "arbitrary"),
        name="rwkv_scan",
    )(*loc_f, v, *loc_b, v)


def _rwkv_out_kernel(yf_ref, yb_ref, bon_ref, g_ref, gw_ref, gb_ref, bd_ref, o_ref, *, width):
    bd = bd_ref[...]
    inv_n = 1.0 / HEAD_A
    for c in range(width // LANE):
        sl = slice(c * LANE, (c + 1) * LANE)
        y = yf_ref[:, sl].astype(F32) + yb_ref[:, sl].astype(F32)
        mean = _head_sum(y, bd) * inv_n
        d = y - mean
        var = _head_sum(d * d, bd) * inv_n
        yn = d * lax.rsqrt(var + GN_EPS) * gw_ref[:, sl] + gb_ref[:, sl] + bon_ref[:, sl]
        o_ref[:, sl] = (yn * g_ref[:, sl]).astype(o_ref.dtype)


def rwkv_out(y_f, y_b, bonus, g, gn_w, gn_b, *, tm=512):
    T, W = y_f.shape
    bd = np.kron(np.eye(LANE // HEAD_A, dtype=np.float32), np.ones((HEAD_A, HEAD_A), np.float32))
    row = pl.BlockSpec((tm, W), lambda i: (i, 0))
    vec = pl.BlockSpec((1, W), lambda i: (0, 0))
    return pl.pallas_call(
        functools.partial(_rwkv_out_kernel, width=W),
        out_shape=jax.ShapeDtypeStruct((T, W), BF16),
        grid=(T // tm,),
        in_specs=[row, row, row, row, vec, vec, pl.BlockSpec((LANE, LANE), lambda i: (0, 0))],
        out_specs=row,
        compiler_params=_cparams("parallel"),
        name="rwkv_out",
    )(y_f, y_b, bonus, g, gn_w.reshape(1, W), gn_b.reshape(1, W), jnp.asarray(bd, BF16))


def rwkv7_mixer(z, mu, w0, w2, a0, a2, g2, k_k, k_a, r_k, gn_w, gn_b, batch):
    r, v, kap, kd_f, kd_b, al_f, al_b, lw_f, lw_b, g, bonus = rwkv_prep(
        z, mu, w0, w2, a0, a2, g2, k_k, k_a, r_k.reshape(-1), batch)
    loc_f, loc_b = rwkv_local(r, v, kap, kd_f, kd_b, al_f, al_b, lw_f, lw_b)
    y_f, y_b = rwkv_scan(loc_f, loc_b, v, batch)
    return rwkv_out(y_f, y_b, bonus, g, gn_w, gn_b)


def _relu2(acc):
    r = jnp.maximum(acc, 0.0)
    return r * r


def _add(acc, res):
    return acc + res


def kernel(x, p, positions, ln1_g, w_in, rwkv_mu, rwkv_w0, rwkv_w2, rwkv_a0, rwkv_a2, rwkv_g2, rwkv_kk, rwkv_ka, rwkv_rk, rwkv_gn_w, rwkv_gn_b, hgrn_lb, hgrn_norm_g, mla_q_norm_g, mla_kv_norm_g, mla_w_uq, mla_w_ukv, w_branch, w_o, ln2_g, w_mlp1, w_mlp2, w_pe, w_pg, final_g):
    Bn, S, D = x.shape
    L = w_in.shape[0]
    T = Bn * S
    W = rwkv_w0.shape[-1]
    q_lora, kv_lora = mla_q_norm_g.shape[-1], mla_kv_norm_g.shape[-1]
    n_heads_c = mla_w_ukv.shape[-1] // (QK_NOPE + V_HEAD)
    rwkv_w = 3 * W + 2 * LORA_DECAY + 2 * LORA_AAA + LORA_GATE
    hgrn_w = 5 * W
    o_hgrn = rwkv_w
    o_cq = o_hgrn + hgrn_w
    o_ckv = o_cq + q_lora
    o_kr = o_ckv + kv_lora
    o_gate = o_kr + QK_ROPE
    half = QK_ROPE // 2

    lb_w = jax.nn.softmax(hgrn_lb.astype(F32), axis=0)
    lower_bounds = jnp.cumsum(lb_w, axis=0) - lb_w[0]
    cq_tab, sq_tab, ck_tab, sk_tab = _rope_tables(positions)

    pad3 = lambda w, mult: jnp.pad(w, ((0, 0), (0, 0), (0, (-w.shape[2]) % mult)))
    w_kr = w_in[:, :, o_kr:o_gate]
    w_all = jnp.concatenate(
        [pad3(w_in[:, :, :rwkv_w], 1024), w_in[:, :, o_hgrn:o_cq], w_in[:, :, o_gate:], w_in[:, :, o_cq:o_kr],
         pad3(w_kr, LANE), pad3(jnp.concatenate([w_kr[:, :, half:], w_kr[:, :, :half]], axis=2), LANE)],
        axis=2).astype(BF16)
    n_rwkv = rwkv_w + (-rwkv_w) % 512
    n_hgrn, n_gate = hgrn_w, w_in.shape[2] - o_gate
    n_ckv = q_lora + kv_lora + 2 * LANE
    c_hgrn = rwkv_w + (-rwkv_w) % 1024
    c_gate, c_ckv = c_hgrn + n_hgrn, c_hgrn + n_hgrn + n_gate
    w_branch_b, w_o_b = w_branch.astype(BF16), w_o.astype(BF16)
    w_mlp1_b, w_mlp2_b = w_mlp1.astype(BF16), w_mlp2.astype(BF16)
    w_pg_b, w_pe_b = w_pg.astype(BF16), w_pe.astype(BF16)
    p2 = p.reshape(L, T, -1)

    h = x.reshape(T, D)
    hn = rmsnorm(h, ln1_g[0], BF16)
    for l in range(L):
        z_rwkv = matmul(hn, w_all, l, out_dtype=F32, tm=1024, tn=n_rwkv // 2, n=n_rwkv, name="in_rwkv")
        z_hgrn = matmul(hn, w_all, l, out_dtype=F32, tm=1024, tn=1024, col0=c_hgrn, n=n_hgrn, name="in_hgrn")
        z_gate = matmul(hn, w_all, l, out_dtype=BF16, tm=1024, tn=1024, col0=c_gate, n=n_gate, name="in_gate")
        z_ckv = matmul(hn, w_all, l, out_dtype=F32, tm=1024, tn=n_ckv // 2, col0=c_ckv, n=n_ckv, name="in_ckv")

        y_a = rwkv7_mixer(z_rwkv, rwkv_mu[l], rwkv_w0[l], rwkv_w2[l], rwkv_a0[l], rwkv_a2[l], rwkv_g2[l],
                          rwkv_kk[l], rwkv_ka[l], rwkv_rk[l], rwkv_gn_w[l], rwkv_gn_b[l], Bn)
        y_b = hgrn2_mixer(z_hgrn, lower_bounds[l], hgrn_norm_g[l], Bn)
        q = mla_q_proj(z_ckv, mla_q_norm_g[l], _mla_q_weights(mla_w_uq[l], n_heads_c), cq_tab, sq_tab,
                       n_heads_c, Q_SCALE)
        k, vt = mla_kv_proj(z_ckv, q_lora, mla_kv_norm_g[l], mla_w_ukv[l].astype(BF16), ck_tab, sk_tab,
                            n_heads_c, Bn)
        y_c = mla_attention(q, k, vt, Bn, n_heads_c)

        mixed = branch_mix(y_a, y_b, y_c, w_branch_b, l, z_gate)
        h, hn2 = out_proj_norm(mixed, w_o_b, l, h, ln2_g[l])
        hid = matmul(hn2, w_mlp1_b, l, out_dtype=BF16, tm=1024, tn=1024, epilogue=_relu2, name="mlp1")
        h = matmul(hid, w_mlp2_b, l, out_dtype=F32, tm=1024, tn=1024, tk=2048, epilogue=_add,
                   extras=(h,), alias_extra=0, name="mlp2")
        last = l == L - 1
        h, hn = ple_update_norm(h, w_pg_b, p2, w_pe_b, l, final_g if last else ln1_g[l + 1],
                                F32 if last else BF16)
    return hn.reshape(Bn, S, D)
```

```python
import functools

import numpy as np
import jax
import jax.numpy as jnp
from jax import lax
from jax.experimental import pallas as pl
from jax.experimental.pallas import tpu as pltpu

F32 = jnp.float32
BF16 = jnp.bfloat16

LANE = 128
VMEM_LIMIT = 48 * 2**20

HEAD_A = 64
LORA_DECAY = 64
LORA_AAA = 64
LORA_GATE = 160
DECAY_SCALE = 0.606531
GN_EPS = 64e-5
HEAD_B = 128
F_TINY = 1e-30
QK_NOPE = 128
QK_ROPE = 64
V_HEAD = 128
ROPE_THETA = 10000.0
NORM_EPS = 1e-6
CHUNK = 64
LOG2_E = 1.4426950408889634
Q_SCALE = (QK_NOPE + QK_ROPE) ** -0.5 * LOG2_E


def _cparams(*sem):
    return pltpu.CompilerParams(dimension_semantics=sem, vmem_limit_bytes=VMEM_LIMIT)


def _sigmoid(x):
    return 1.0 / (1.0 + jnp.exp(-x))


def _dot(a, b):
    return jnp.dot(a, b, preferred_element_type=F32)


def _dot_nt(a, b):
    return lax.dot_general(a, b, (((1,), (1,)), ((), ())), preferred_element_type=F32)


def _dot_tn(a, b):
    return lax.dot_general(a, b, (((0,), (0,)), ((), ())), preferred_element_type=F32)


def _split3(x):
    x1 = x.astype(BF16)
    r1 = x - x1.astype(F32)
    x2 = r1.astype(BF16)
    x3 = (r1 - x2.astype(F32)).astype(BF16)
    return x1, x2, x3


def _rmsnorm_kernel(x_ref, g_ref, o_ref):
    x = x_ref[...]
    ms = jnp.mean(x * x, axis=-1, keepdims=True)
    o_ref[...] = (x * lax.rsqrt(ms + NORM_EPS) * g_ref[...]).astype(o_ref.dtype)


def rmsnorm(x, g, out_dtype, tm=512):
    T, D = x.shape
    return pl.pallas_call(
        _rmsnorm_kernel,
        out_shape=jax.ShapeDtypeStruct((T, D), out_dtype),
        grid=(T // tm,),
        in_specs=[pl.BlockSpec((tm, D), lambda i: (i, 0)),
                  pl.BlockSpec((1, D), lambda i: (0, 0))],
        out_specs=pl.BlockSpec((tm, D), lambda i: (i, 0)),
        compiler_params=_cparams("parallel"),
        name="rmsnorm",
    )(x, g.reshape(1, D))


def _mm_kernel(a_ref, w_ref, *rest, nk, epilogue, n_extra, additive):
    extras = rest[:n_extra]
    o_ref = rest[n_extra]

    def finish(acc):
        o_ref[...] = epilogue(acc, *[e[...] for e in extras]).astype(o_ref.dtype)

    if nk == 1:
        finish(_dot(a_ref[...], w_ref[...]))
    elif additive:
        k = pl.program_id(2)

        @pl.when(k == 0)
        def _():
            finish(_dot(a_ref[...], w_ref[...]))

        @pl.when(k > 0)
        def _():
            o_ref[...] += _dot(a_ref[...], w_ref[...])
    else:
        acc_ref = rest[n_extra + 1]
        k = pl.program_id(2)

        @pl.when(k == 0)
        def _():
            acc_ref[...] = jnp.zeros_like(acc_ref)

        acc_ref[...] += _dot(a_ref[...], w_ref[...])

        @pl.when(k == nk - 1)
        def _():
            finish(acc_ref[...])


def matmul(a, w, layer, *, out_dtype, tm, tn, tk=None, col0=0, n=None, epilogue=None, extras=(),
           alias_extra=None, additive=False, name="matmul"):
    M, K = a.shape
    N = w.shape[2] - col0 if n is None else n
    tk = K if tk is None else tk
    nk = K // tk
    assert col0 % tn == 0 and N % tn == 0 and M % tm == 0 and K % tk == 0, (col0, N, tn, M, tm, K, tk)
    j0 = col0 // tn
    epilogue = epilogue or (lambda acc: acc)
    assert not additive or out_dtype == F32
    kern = functools.partial(_mm_kernel, nk=nk, epilogue=epilogue, n_extra=len(extras), additive=additive)
    in_specs = [pl.BlockSpec((tm, tk), lambda i, j, k: (i, k)),
                pl.BlockSpec((None, tk, tn), lambda i, j, k: (layer, k, j0 + j))]
    in_specs += [pl.BlockSpec((tm, tn), lambda i, j, k: (i, j)) for _ in extras]
    aliases = {} if alias_extra is None else {2 + alias_extra: 0}
    return pl.pallas_call(
        kern,
        out_shape=jax.ShapeDtypeStruct((M, N), out_dtype),
        grid=(M // tm, N // tn, nk),
        in_specs=in_specs,
        out_specs=pl.BlockSpec((tm, tn), lambda i, j, k: (i, j)),
        scratch_shapes=[pltpu.VMEM((tm, tn), F32)] if nk > 1 and not additive else [],
        input_output_aliases=aliases,
        compiler_params=_cparams("parallel", "parallel", "arbitrary"),
        name=name,
    )(a, w, *extras)


def _branch_kernel(ya_ref, yb_ref, yc_ref, p_ref, ga_ref, gb_ref, gc_ref, o_ref):
    acc = _sigmoid(ga_ref[...].astype(F32)) * _dot(ya_ref[...], p_ref[0])
    acc += _sigmoid(gb_ref[...].astype(F32)) * _dot(yb_ref[...], p_ref[1])
    acc += _sigmoid(gc_ref[...].astype(F32)) * _dot(yc_ref[...], p_ref[2])
    o_ref[...] = acc.astype(o_ref.dtype)


def branch_mix(ya, yb, yc, p, layer, zg, *, tm=1024, tn=1024):
    T, W = ya.shape
    D = p.shape[3]
    nj = D // tn
    y_spec = pl.BlockSpec((tm, W), lambda i, j: (i, 0))
    g_specs = [pl.BlockSpec((tm, tn), functools.partial(lambda i, j, n: (i, n * nj + j), n=n))
               for n in range(3)]
    return pl.pallas_call(
        _branch_kernel,
        out_shape=jax.ShapeDtypeStruct((T, D), BF16),
        grid=(T // tm, nj),
        in_specs=[y_spec, y_spec, y_spec,
                  pl.BlockSpec((None, 3, W, tn), lambda i, j: (layer, 0, 0, j))] + g_specs,
        out_specs=pl.BlockSpec((tm, tn), lambda i, j: (i, j)),
        compiler_params=_cparams("parallel", "parallel"),
        name="branch_mix",
    )(ya, yb, yc, p, zg, zg, zg)


def _rms_rows(h, g):
    return h * lax.rsqrt(jnp.mean(h * h, axis=-1, keepdims=True) + NORM_EPS) * g


def _wo_ln_kernel(a_ref, w_ref, h_ref, g_ref, h_out_ref, hn_ref):
    h = h_ref[...] + _dot(a_ref[...], w_ref[...])
    h_out_ref[...] = h
    hn_ref[...] = _rms_rows(h, g_ref[...]).astype(hn_ref.dtype)


def out_proj_norm(a, w, layer, h, g, *, tm=512):
    T, D = h.shape
    K = a.shape[1]
    row = lambda width: pl.BlockSpec((tm, width), lambda i: (i, 0))
    return pl.pallas_call(
        _wo_ln_kernel,
        out_shape=(jax.ShapeDtypeStruct((T, D), F32), jax.ShapeDtypeStruct((T, D), BF16)),
        grid=(T // tm,),
        in_specs=[row(K), pl.BlockSpec((None, K, D), lambda i: (layer, 0, 0)), row(D),
                  pl.BlockSpec((1, D), lambda i: (0, 0))],
        out_specs=(row(D), row(D)),
        input_output_aliases={2: 0},
        compiler_params=_cparams("parallel"),
        name="out_proj_norm",
    )(a, w, h, g.reshape(1, D))


def _ple_ln_kernel(h_ref, wpg_ref, p_ref, wpe_ref, g_ref, h_out_ref, hn_ref):
    h = h_ref[...]
    gate = _sigmoid(_dot(h.astype(BF16), wpg_ref[...]))
    h = h + gate * _dot(p_ref[...].astype(BF16), wpe_ref[...])
    h_out_ref[...] = h
    hn_ref[...] = _rms_rows(h, g_ref[...]).astype(hn_ref.dtype)


def ple_update_norm(h, wpg, p, wpe, layer, g, norm_dtype, *, tm=512):
    T, D = h.shape
    E = p.shape[2]
    row = lambda width: pl.BlockSpec((tm, width), lambda i: (i, 0))
    stacked = lambda r, c: pl.BlockSpec((None, r, c), lambda i: (layer, 0, 0))
    return pl.pallas_call(
        _ple_ln_kernel,
        out_shape=(jax.ShapeDtypeStruct((T, D), F32), jax.ShapeDtypeStruct((T, D), norm_dtype)),
        grid=(T // tm,),
        in_specs=[row(D), stacked(D, D), pl.BlockSpec((None, tm, E), lambda i: (layer, i, 0)), stacked(E, D),
                  pl.BlockSpec((1, D), lambda i: (0, 0))],
        out_specs=(row(D), row(D)),
        input_output_aliases={0: 0},
        compiler_params=_cparams("parallel"),
        name="ple_update_norm",
    )(h, wpg, p, wpe, g.reshape(1, D))


def _mla_q_kernel(cq_ref, g_ref, w_ref, c_ref, s_ref, o_ref, *, scale, n_heads):
    xn = _rms_rows(cq_ref[...], g_ref[...]).astype(BF16)
    x = _dot(xn, w_ref[...])
    c, s = c_ref[...], s_ref[...]
    for h in range(n_heads):
        lo = 2 * LANE * h
        hi = x[:, lo + QK_NOPE:lo + 2 * LANE]
        o_ref[:, lo:lo + QK_NOPE] = (x[:, lo:lo + QK_NOPE] * scale).astype(BF16)
        o_ref[:, lo + QK_NOPE:lo + 2 * LANE] = (hi * c + pltpu.roll(hi, LANE // 2, 1) * s).astype(BF16)


def mla_q_proj(z, g, w, ctab, stab, n_heads, scale, *, tm=512):
    T, R = z.shape[0], g.shape[-1]
    N = w.shape[1]
    return pl.pallas_call(
        functools.partial(_mla_q_kernel, scale=scale, n_heads=n_heads),
        out_shape=jax.ShapeDtypeStruct((T, N), BF16),
        grid=(T // tm,),
        in_specs=[pl.BlockSpec((tm, R), lambda i: (i, 0)),
                  pl.BlockSpec((1, R), lambda i: (0, 0)),
                  pl.BlockSpec((R, N), lambda i: (0, 0)),
                  pl.BlockSpec((tm, LANE), lambda i: (i, 0)),
                  pl.BlockSpec((tm, LANE), lambda i: (i, 0))],
        out_specs=pl.BlockSpec((tm, N), lambda i: (i, 0)),
        compiler_params=_cparams("parallel"),
        name="mla_q_proj",
    )(z, g.reshape(1, R), w, ctab, stab)


def _mla_kv_kernel(z_ref, g_ref, w_ref, c_ref, s_ref, k_ref, v_ref, *, kv_lora, n_heads):
    xn = _rms_rows(z_ref[:, :kv_lora], g_ref[...]).astype(BF16)
    kv = _dot(xn, w_ref[...])
    kr = (z_ref[:, kv_lora:kv_lora + LANE] * c_ref[...] + z_ref[:, kv_lora + LANE:] * s_ref[...]).astype(BF16)
    for h in range(n_heads):
        lo = 2 * LANE * h
        k_ref[:, lo:lo + QK_NOPE] = kv[:, lo:lo + QK_NOPE].astype(BF16)
        k_ref[:, lo + QK_NOPE:lo + 2 * LANE] = kr
        v_ref[h * V_HEAD:(h + 1) * V_HEAD, :] = kv[:, lo + QK_NOPE:lo + 2 * LANE].T.astype(BF16)


def mla_kv_proj(zkv, col0, g, w, ctab, stab, n_heads, batch, *, tm=512):
    T = zkv.shape[0]
    R = g.shape[-1]
    ZW = R + 2 * LANE
    cb = col0 // ZW
    S = T // batch
    tm = min(tm, S)
    nsb = S // tm
    N = w.shape[1]
    return pl.pallas_call(
        functools.partial(_mla_kv_kernel, kv_lora=R, n_heads=n_heads),
        out_shape=(jax.ShapeDtypeStruct((T, n_heads * 2 * LANE), BF16),
                   jax.ShapeDtypeStruct((batch * n_heads * V_HEAD, S), BF16)),
        grid=(T // tm,),
        in_specs=[pl.BlockSpec((tm, ZW), lambda i: (i, cb)),
                  pl.BlockSpec((1, R), lambda i: (0, 0)),
                  pl.BlockSpec((R, N), lambda i: (0, 0)),
                  pl.BlockSpec((tm, LANE), lambda i: (i, 0)),
                  pl.BlockSpec((tm, LANE), lambda i: (i, 0))],
        out_specs=(pl.BlockSpec((tm, n_heads * 2 * LANE), lambda i: (i, 0)),
                   pl.BlockSpec((n_heads * V_HEAD, tm), lambda i: (i // nsb, i % nsb))),
        compiler_params=_cparams("parallel"),
        name="mla_kv_proj",
    )(zkv, g.reshape(1, R), w, ctab, stab)


ATTN_TILES = 8


def _attn_kernel(q_ref, k_ref, vt_ref, o_ref, *, tq):
    k = k_ref[...]
    tiles = [slice(j * tq, (j + 1) * tq) for j in range(ATTN_TILES)]
    scores = lambda t: _dot_nt(k, q_ref[t, :])

    def finish(t, st):
        p = jnp.exp2(st - jnp.max(st, axis=0, keepdims=True))
        ot = _dot(vt_ref[...], p.astype(BF16))
        o_ref[t, :] = (ot / jnp.sum(p, axis=0, keepdims=True)).T.astype(o_ref.dtype)

    st = scores(tiles[0])
    for j, t in enumerate(tiles):
        st_next = scores(tiles[j + 1]) if j + 1 < len(tiles) else None
        finish(t, st)
        st = st_next


def mla_attention(q, k, vt, batch, n_heads, *, tq=512):
    T = q.shape[0]
    S = T // batch
    tb = tq * ATTN_TILES
    nq = S // tb
    QW = q.shape[1] // n_heads
    return pl.pallas_call(
        functools.partial(_attn_kernel, tq=tq),
        out_shape=jax.ShapeDtypeStruct((T, n_heads * V_HEAD), BF16),
        grid=(batch, n_heads, nq),
        in_specs=[pl.BlockSpec((tb, QW), lambda b, h, i: (b * nq + i, h)),
                  pl.BlockSpec((S, QW), lambda b, h, i: (b, h)),
                  pl.BlockSpec((V_HEAD, S), lambda b, h, i: (b * n_heads + h, 0))],
        out_specs=pl.BlockSpec((tb, V_HEAD), lambda b, h, i: (b * nq + i, h)),
        compiler_params=_cparams("parallel", "parallel", "arbitrary"),
        name="mla_attention",
    )(q, k, vt)


def _rope_tables(positions):
    inv_freq = 1.0 / (ROPE_THETA ** (jnp.arange(0, QK_ROPE, 2, dtype=F32) / QK_ROPE))
    ang = positions.astype(F32).reshape(-1, 1) * inv_freq
    cos, sin = jnp.cos(ang), jnp.sin(ang)
    T = ang.shape[0]
    z64 = jnp.zeros((T, LANE - QK_ROPE), F32)
    ck = jnp.concatenate([cos, cos, z64], axis=1)
    sk = jnp.concatenate([-sin, sin, z64], axis=1)
    return ck * Q_SCALE, sk * Q_SCALE, ck, sk


def _mla_q_weights(w_uq, n_heads):
    R = w_uq.shape[0]
    w = w_uq.reshape(R, n_heads, QK_NOPE + QK_ROPE)
    half = QK_ROPE // 2
    w = jnp.concatenate([w, w[:, :, QK_NOPE + half:], w[:, :, QK_NOPE:QK_NOPE + half]], axis=2)
    return w.reshape(R, n_heads * 2 * LANE).astype(BF16)


N_LEVELS = 6
HGRN_BATCH = 8


def _gla_constants():
    C = CHUNK
    t = np.arange(C)[:, None]
    u = np.arange(C)[None, :]
    mk = np.zeros((N_LEVELS + 1, C, C), np.float32)
    for l in range(N_LEVELS):
        m = C >> (l + 1)
        mk[l] = (t // (2 * m) == u // (2 * m)) & ((t % (2 * m)) >= m) & ((u % (2 * m)) < m)
    mk[N_LEVELS] = t == u
    tri = (u <= t).astype(np.float32)
    to2d = lambda a: np.ascontiguousarray(a).reshape(-1, C)
    return tri, np.ascontiguousarray(tri[::-1, ::-1]), to2d(mk), to2d(mk[:, ::-1, ::-1])


def _level_exponent(b, g, m, rev, row):
    C = CHUNK
    if m >= 4:
        r = m if rev else m - 1
        b_mid = jnp.concatenate([jnp.broadcast_to(b[s + r:s + r + 1, :], (2 * m, b.shape[1]))
                                 for s in range(0, C, 2 * m)], axis=0)
        return -jnp.abs(b - b_mid)
    g_next = pltpu.roll(g, C - 1, 0)
    g_prev = pltpu.roll(g, 1, 0)
    if m == 2:
        p4 = row % 4
        if rev:
            return jnp.where(p4 == 3, g_prev, jnp.where(p4 == 2, 0.0, jnp.where(p4 == 1, g, g + g_next)))
        return jnp.where(p4 == 0, g_next, jnp.where(p4 == 1, 0.0, jnp.where(p4 == 2, g, g + g_prev)))
    return jnp.where((row % 2 == 1) != rev, g, 0.0)


def _gla_local(chunks):
    C = CHUNK
    row = lax.broadcasted_iota(jnp.int32, (C, 1), 0)
    gs = [c[3] * LOG2_E for c in chunks]
    parts = [_split3(g) for g in gs]
    bs = [_dot(c[4], p[0]) + _dot(c[4], p[1]) + _dot(c[4], p[2]) for c, p in zip(chunks, parts)]
    qbs = [c[0].astype(BF16) for c in chunks]
    kbs = [c[1].astype(BF16) for c in chunks]
    scores = [c[5][N_LEVELS * C:(N_LEVELS + 1) * C] * _dot_nt(qb, kb) for c, qb, kb in zip(chunks, qbs, kbs)]
    for l in range(N_LEVELS):
        pls = [jnp.exp2(_level_exponent(b, g, C >> (l + 1), c[6], row)).astype(BF16)
               for c, b, g in zip(chunks, bs, gs)]
        prods = [_dot_nt(qb * p, kb * p) for qb, kb, p in zip(qbs, kbs, pls)]
        scores = [s + c[5][l * C:(l + 1) * C] * a for s, c, a in zip(scores, chunks, prods)]
    out = []
    for c, b, s in zip(chunks, bs, scores):
        q, k, v, rev = c[0], c[1], c[2], c[6]
        b_tot = b[0:1] if rev else b[C - 1:C]
        vb = v.astype(BF16)
        out.append((_dot(s.astype(BF16), vb), (q * jnp.exp2(b)).astype(BF16),
                    _dot_tn(vb, (k * jnp.exp2(b_tot - b)).astype(BF16)), jnp.exp2(b_tot)))
    return out


def _hgrn_kernel(q_ref, zf_ref, zb_ref, i_ref, g_ref, lb_ref, ng_ref, trif_ref, trib_ref, mkf_ref, mkb_ref,
                 o_ref, of_ref, ob_ref, *, n_chunks):
    C = CHUNK
    lb = lb_ref[...]
    one_m_lb = 1.0 - lb
    trif, trib, mkf, mkb = trif_ref[...], trib_ref[...], mkf_ref[...], mkb_ref[...]

    def gate(zz):
        w = one_m_lb * _sigmoid(zz)
        return one_m_lb - w, jnp.log(jnp.maximum(lb + w, F_TINY))

    def load(ref, c):
        return ref[pl.ds(pl.multiple_of(c * C, C), C), :]

    def chunk_inputs(c, z_ref, tri, mk, rev):
        q = load(q_ref, c)
        k, g = gate(load(z_ref, c))
        return (q * _sigmoid(q), k, load(i_ref, c), g, tri, mk, rev)

    def body(it, carry):
        states = list(carry)
        ids = [[it * HGRN_BATCH + j for j in range(HGRN_BATCH)]]
        ids.append([n_chunks - 1 - c for c in ids[0]])
        loc = _gla_local([chunk_inputs(c, zf_ref, trif, mkf, False) for c in ids[0]]
                         + [chunk_inputs(c, zb_ref, trib, mkb, True) for c in ids[1]])
        for d, out_ref in enumerate((of_ref, ob_ref)):
            for j, c in enumerate(ids[d]):
                o_intra, q_dec, st_inc, p_tot = loc[d * HGRN_BATCH + j]
                out_ref[pl.ds(pl.multiple_of(c * C, C), C), :] = o_intra + _dot_nt(q_dec, states[d].astype(BF16))
                states[d] = states[d] * p_tot + st_inc
        return tuple(states)

    dv, dk = i_ref.shape[1], q_ref.shape[1]
    z = jnp.zeros((dv, dk), F32)
    lax.fori_loop(0, n_chunks // HGRN_BATCH, body, (z, z))

    o = of_ref[...] + ob_ref[...]
    ms = jnp.mean(o * o, axis=-1, keepdims=True)
    gg = g_ref[...]
    o_ref[...] = (o * lax.rsqrt(ms + NORM_EPS) * ng_ref[...] * (gg * _sigmoid(gg))).astype(o_ref.dtype)


def hgrn2_mixer(z, lb, norm_g, batch, *, col0=0):
    T = z.shape[0]
    W = lb.shape[-1]
    S = T // batch
    H = W // HEAD_B
    c0 = col0 // HEAD_B
    trif, trib, mkf, mkb = _gla_constants()
    part = lambda n: pl.BlockSpec((S, HEAD_B), functools.partial(lambda b, h, n: (b, c0 + n * H + h), n=n))
    const = lambda a: pl.BlockSpec(a.shape, lambda b, h: (0, 0))
    return pl.pallas_call(
        functools.partial(_hgrn_kernel, n_chunks=S // CHUNK),
        out_shape=jax.ShapeDtypeStruct((T, W), BF16),
        grid=(batch, H),
        in_specs=[part(0), part(1), part(2), part(3), part(4),
                  pl.BlockSpec((1, HEAD_B), lambda b, h: (0, h)),
                  pl.BlockSpec((1, HEAD_B), lambda b, h: (0, 0)),
                  const(trif), const(trib), const(mkf), const(mkb)],
        out_specs=pl.BlockSpec((S, HEAD_B), lambda b, h: (b, h)),
        scratch_shapes=[pltpu.VMEM((S, HEAD_B), F32), pltpu.VMEM((S, HEAD_B), F32)],
        compiler_params=_cparams("parallel", "parallel"),
        name="hgrn2_mixer",
    )(z, z, z, z, z, lb.reshape(1, W), norm_g.reshape(1, HEAD_B),
      jnp.asarray(trif, BF16), jnp.asarray(trib, BF16), jnp.asarray(mkf), jnp.asarray(mkb))


def _head_sum(x, bd):
    x1, x2, _ = _split3(x)
    return _dot(x1, bd) + _dot(x2, bd)


def _rwkv_prep_kernel(x_ref, xp_ref, xn_ref, mu_ref, w0_ref, w2_ref, a0_ref, a2_ref, g2_ref, kk_ref,
                      ka_ref, rk_ref, bd_ref,
                      r_ref, v_ref, kap_ref, kdf_ref, kdb_ref, alf_ref, alb_ref, lwf_ref, lwb_ref,
                      g_ref, bon_ref, *, tm, seq, width):
    W = width
    i = pl.program_id(0)
    x = x_ref[...]
    at_start = (i * tm) % seq == 0
    at_end = ((i + 1) * tm) % seq == 0
    prev_row = jnp.where(at_start, 0.0, xp_ref[7:8, :])
    next_row = jnp.where(at_end, 0.0, xn_ref[0:1, :])
    row = lax.broadcasted_iota(jnp.int32, (8, 1), 0)
    x_prev = pltpu.roll(x, 1, 0)
    x_prev = jnp.concatenate([jnp.where(row == 0, prev_row, x_prev[:8]), x_prev[8:]], axis=0)
    x_next = pltpu.roll(x, tm - 1, 0)
    x_next = jnp.concatenate([x_next[:tm - 8], jnp.where(row == 7, next_row, x_next[tm - 8:])], axis=0)
    u = x + mu_ref[0:1, :] * (x_prev - x) + mu_ref[1:2, :] * (x_next - x)

    r, k, v = u[:, :W], u[:, W:2 * W], u[:, 2 * W:3 * W]
    o = 3 * W
    wd = jnp.tanh(u[:, o:o + 2 * LORA_DECAY]).astype(BF16)
    o += 2 * LORA_DECAY
    ad = u[:, o:o + 2 * LORA_AAA].astype(BF16)
    o += 2 * LORA_AAA
    gd = _sigmoid(u[:, o:]).astype(BF16)

    bd = bd_ref[...]
    kkr = k * kk_ref[...]
    sq = kkr * kkr
    a_dir = []
    for n in range(2):
        w_raw = w0_ref[n:n + 1, :] + _dot(wd[:, n * LORA_DECAY:(n + 1) * LORA_DECAY], w2_ref[n])
        lw = -DECAY_SCALE * _sigmoid(w_raw)
        a = _sigmoid(a0_ref[n:n + 1, :] + _dot(ad[:, n * LORA_AAA:(n + 1) * LORA_AAA], a2_ref[n]))
        a_dir.append(a)
        (lwf_ref, lwb_ref)[n][...] = lw
    kd = [k * (1.0 + (a - 1.0) * ka_ref[...]) for a in a_dir]
    kdf_ref[...] = kd[0].astype(kdf_ref.dtype)
    kdb_ref[...] = kd[1].astype(kdb_ref.dtype)
    rkk = r * (kd[0] + kd[1]) * rk_ref[...]
    for c in range(W // LANE):
        sl = slice(c * LANE, (c + 1) * LANE)
        nrm = jnp.maximum(jnp.sqrt(_head_sum(sq[:, sl], bd)), 1e-12)
        kap = kkr[:, sl] / nrm
        kap_ref[:, sl] = kap.astype(kap_ref.dtype)
        alf_ref[:, sl] = (kap * a_dir[0][:, sl]).astype(alf_ref.dtype)
        alb_ref[:, sl] = (kap * a_dir[1][:, sl]).astype(alb_ref.dtype)
        bon_ref[:, sl] = _head_sum(rkk[:, sl], bd) * v[:, sl]
    r_ref[...] = r.astype(r_ref.dtype)
    v_ref[...] = v.astype(v_ref.dtype)
    g_ref[...] = _dot(gd, g2_ref[...]).astype(g_ref.dtype)


def rwkv_prep(z, mu, w0, w2, a0, a2, g2, k_k, k_a, r_k, batch, *, tm=256):
    T, ZW = z.shape
    W = w0.shape[-1]
    S = T // batch
    nb8 = tm // 8
    used = 3 * W + 2 * LORA_DECAY + 2 * LORA_AAA + LORA_GATE
    mu_p = jnp.pad(mu, ((0, 0), (0, ZW - used)))
    g2_p = jnp.pad(g2, ((0, ZW - used), (0, 0))).astype(BF16)
    bd = np.kron(np.eye(LANE // HEAD_A, dtype=np.float32), np.ones((HEAD_A, HEAD_A), np.float32))
    full = lambda a: pl.BlockSpec(a.shape, lambda i: (0,) * a.ndim)
    vec = lambda a: a.reshape(1, W)
    args = [z, z, z, mu_p, w0, w2.astype(BF16), a0, a2.astype(BF16), g2_p, vec(k_k), vec(k_a),
            vec(r_k), jnp.asarray(bd, BF16)]
    in_specs = [pl.BlockSpec((tm, ZW), lambda i: (i, 0)),
                pl.BlockSpec((8, ZW), lambda i: (jnp.maximum(i * nb8 - 1, 0), 0)),
                pl.BlockSpec((8, ZW), lambda i: (jnp.minimum((i + 1) * nb8, T // 8 - 1), 0))]
    in_specs += [full(a) for a in args[3:]]
    b16 = jax.ShapeDtypeStruct((T, W), BF16)
    f32 = jax.ShapeDtypeStruct((T, W), F32)
    return pl.pallas_call(
        functools.partial(_rwkv_prep_kernel, tm=tm, seq=S, width=W),
        out_shape=(b16,) * 7 + (f32, f32, b16, f32),
        grid=(T // tm,),
        in_specs=in_specs,
        out_specs=(pl.BlockSpec((tm, W), lambda i: (i, 0)),) * 11,
        compiler_params=_cparams("parallel"),
        name="rwkv_prep",
    )(*args)


RWKV_GROUP = 256


def _rwkv_constants():
    G, C = RWKV_GROUP, CHUNK
    t = np.arange(G)[:, None]
    u = np.arange(G)[None, :]
    dd = (((t // C) == (u // C)) & (u <= t)).astype(np.float32)
    return dd, np.ascontiguousarray(dd[::-1, ::-1])


def _bdot(a, b):
    return _dot(a.astype(BF16), b.astype(BF16))


def _unit_tri_inverses(ms, eye, diag_blocks):
    each = lambda f, *ls: [f(*a) for a in zip(*ls)]
    mds = [jnp.where(diag_blocks, m, 0.0) for m in ms]
    mos = each(lambda m, md: m - md, ms, mds)
    xs = [eye - md for md in mds]
    ps = each(_bdot, mds, mds)
    for _ in range(2):
        xs = each(lambda x, p: x + _bdot(x, p), xs, ps)
        ps = each(_bdot, ps, ps)
    tds = each(lambda x, p: x + _bdot(x, p), xs, ps)
    gs = each(_bdot, tds, mos)
    g2s = each(_bdot, gs, gs)
    ys = each(lambda g, g2: (eye - g) + _bdot(eye - g, g2), gs, g2s)
    return each(_bdot, ys, tds)


def _rwkv_groups(pairs, eye, diag_blocks, head_masks):
    G = pairs[0][0].shape[0]
    pre = []
    for r, v, kap, kd, al, lw, dd, rev, causal_incl, causal_strict in [
            (r, v, kap) + tuple(d) for r, v, kap, dirs in pairs for d in dirs]:
        l1, l2, _ = _split3(lw)
        b_in = _dot(dd, l1) + _dot(dd, l2)
        r_tot = 0 if rev else CHUNK - 1
        b_tot = jnp.concatenate([jnp.broadcast_to(b_in[s + r_tot:s + r_tot + 1], (CHUNK, b_in.shape[1]))
                                 for s in range(0, G, CHUNK)], axis=0)
        b_ex, b_out = b_in - lw, b_tot - b_in
        p_neg = jnp.exp(-b_in)
        p_out = jnp.exp(b_out)
        pre.append(dict(kap_h=kap * jnp.exp(b_ex), r_h=r * jnp.exp(b_in), al_n=(al * p_neg).astype(BF16),
                        kd_n=(kd * p_neg).astype(BF16), kout=kd * p_out, aout=al * p_out, p_tot=jnp.exp(b_tot),
                        incl=causal_incl, strict=causal_strict, v=v))
    chains = [(p, hm) for p in pre for hm in head_masks]
    kap_hb = [jnp.where(hm, p["kap_h"], 0.0).astype(BF16) for p, hm in chains]
    r_hb = [jnp.where(hm, p["r_h"], 0.0).astype(BF16) for p, hm in chains]
    v_h = [jnp.where(hm, p["v"], 0.0).astype(BF16) for p, hm in chains]
    ms = [jnp.where(p["strict"], _dot_nt(k_, p["al_n"]), 0.0) for (p, _), k_ in zip(chains, kap_hb)]
    ns = [jnp.where(p["strict"], _dot_nt(k_, p["kd_n"]), 0.0) for (p, _), k_ in zip(chains, kap_hb)]
    ras = [jnp.where(p["incl"], _dot_nt(r_, p["al_n"]), 0.0) for (p, _), r_ in zip(chains, r_hb)]
    rks = [jnp.where(p["incl"], _dot_nt(r_, p["kd_n"]), 0.0) for (p, _), r_ in zip(chains, r_hb)]
    nvs = [_bdot(n, vh).astype(BF16) for n, vh in zip(ns, v_h)]
    y0s = [_bdot(rk, vh) for rk, vh in zip(rks, v_h)]
    racs = [sum(ra[:, s:s + CHUNK] for s in range(0, G, CHUNK)) for ra in ras]
    tinvs = [t.astype(BF16) for t in _unit_tri_inverses(ms, eye, diag_blocks)]
    kw_uv = [_dot(t, jnp.concatenate([k_, nv], axis=1)) for t, k_, nv in zip(tinvs, kap_hb, nvs)]
    out = []
    for d, p in enumerate(pre):
        a, b = kw_uv[2 * d], kw_uv[2 * d + 1]
        out.append((a[:, :LANE] + b[:, :LANE], p["r_h"], jnp.concatenate(racs[2 * d:2 * d + 2], axis=1),
                    p["kout"], p["aout"], a[:, LANE:] + b[:, LANE:], y0s[2 * d] + y0s[2 * d + 1], p["p_tot"]))
    return out


N_LOCAL = 8
LOCAL_PAIRS = 2


def _rwkv_local_kernel(r_ref, v_ref, kap_ref, kdf_ref, kdb_ref, alf_ref, alb_ref, lwf_ref, lwb_ref,
                       ddf_ref, ddb_ref, *out_refs):
    G, C = RWKV_GROUP, CHUNK
    ti = lax.broadcasted_iota(jnp.int32, (G, G), 0)
    si = lax.broadcasted_iota(jnp.int32, (G, G), 1)
    same_chunk = (ti // C) == (si // C)
    eye = (ti == si).astype(F32)
    diag_blocks = (ti // 16) == (si // 16)
    lane = lax.broadcasted_iota(jnp.int32, (1, LANE), 1)
    head_masks = [lane < HEAD_A, lane >= HEAD_A]
    masks = []
    for d in range(2):
        masks.append((same_chunk & ((si <= ti) if d == 0 else (si >= ti)),
                      same_chunk & ((si < ti) if d == 0 else (si > ti))))
    pairs = []
    for pr in range(LOCAL_PAIRS):
        ln = slice(pr * LANE, (pr + 1) * LANE)
        dirs = [(kd_ref[:, ln], al_ref[:, ln], lw_ref[:, ln], dd_ref[...], d == 1) + masks[d]
                for d, (kd_ref, al_ref, lw_ref, dd_ref) in enumerate(((kdf_ref, alf_ref, lwf_ref, ddf_ref),
                                                                      (kdb_ref, alb_ref, lwb_ref, ddb_ref)))]
        pairs.append((r_ref[:, ln], v_ref[:, ln], kap_ref[:, ln], dirs))
    results = _rwkv_groups(pairs, eye, diag_blocks, head_masks)
    for pr in range(LOCAL_PAIRS):
        ln = slice(pr * LANE, (pr + 1) * LANE)
        for d in range(2):
            res = results[2 * pr + d]
            outs = out_refs[d * N_LOCAL:(d + 1) * N_LOCAL]
            for o_ref, val in zip(outs[:-1], res[:-1]):
                o_ref[:, ln] = val.astype(o_ref.dtype)
            p_tot = res[-1]
            outs[-1][:, ln] = jnp.concatenate([p_tot[c * C:c * C + 8] for c in range(G // C)], axis=0)


def rwkv_local(r, v, kap, kd_f, kd_b, al_f, al_b, lw_f, lw_b):
    T, W = r.shape
    G = RWKV_GROUP
    ddf, ddb = _rwkv_constants()
    bw = LANE * LOCAL_PAIRS
    blk = pl.BlockSpec((G, bw), lambda i, h: (i, h))
    pblk = pl.BlockSpec((G // 8, bw), lambda i, h: (i, h))
    const = lambda a: pl.BlockSpec(a.shape, lambda i, h: (0, 0))
    b16 = jax.ShapeDtypeStruct((T, W), BF16)
    per_dir = (b16,) * 7 + (jax.ShapeDtypeStruct((T // 8, W), F32),)
    outs = pl.pallas_call(
        _rwkv_local_kernel,
        out_shape=per_dir * 2,
        grid=(T // G, W // bw),
        in_specs=[blk] * 9 + [const(ddf), const(ddb)],
        out_specs=((blk,) * 7 + (pblk,)) * 2,
        compiler_params=_cparams("parallel", "parallel"),
        name="rwkv_local",
    )(r, v, kap, kd_f, kd_b, al_f, al_b, lw_f, lw_b,
      jnp.asarray(ddf, BF16), jnp.asarray(ddb, BF16))
    return outs[:N_LOCAL], outs[N_LOCAL:]


SCAN_PAIRS = 8


def _rwkv_scan_chunks(chains, head_masks, same_head):
    C = CHUNK
    m1s = [_dot_nt(jnp.concatenate([c[0], c[1]], axis=0), c[9].astype(BF16)) for c in chains]
    us = [m1[:C] + c[5] for m1, c in zip(m1s, chains)]
    incs = [_dot_tn(jnp.concatenate([c[7].astype(BF16), (-u).astype(BF16)], axis=0),
                    jnp.concatenate([c[3], c[4]], axis=0)) for u, c in zip(us, chains)]
    sts = [jnp.where(same_head, c[9] * c[8] + inc, 0.0) for c, inc in zip(chains, incs)]
    u_cats = [jnp.concatenate([jnp.where(hm, u, 0.0).astype(BF16) for hm in head_masks], axis=0) for u in us]
    ys = [m1[C:] + c[6] - _dot(c[2], u_cat) for m1, c, u_cat in zip(m1s, chains, u_cats)]
    return list(zip(ys, sts))


def _rwkv_scan_kernel(*refs, n_chunks, n_pairs):
    C = CHUNK
    f_refs, vf_ref = refs[:N_LOCAL], refs[N_LOCAL]
    b_refs, vb_ref = refs[N_LOCAL + 1:2 * N_LOCAL + 1], refs[2 * N_LOCAL + 1]
    yf_ref, yb_ref, stf_ref, stb_ref = refs[2 * N_LOCAL + 2:]

    @pl.when(pl.program_id(2) == 0)
    def _():
        stf_ref[...] = jnp.zeros_like(stf_ref)
        stb_ref[...] = jnp.zeros_like(stb_ref)

    lane = lax.broadcasted_iota(jnp.int32, (1, LANE), 1)
    head_masks = [lane < HEAD_A, lane >= HEAD_A]
    vi = lax.broadcasted_iota(jnp.int32, (LANE, LANE), 0)
    ki = lax.broadcasted_iota(jnp.int32, (LANE, LANE), 1)
    same_head = (vi // HEAD_A) == (ki // HEAD_A)

    def rows(ref, c, n):
        return ref[pl.ds(pl.multiple_of(c * n, n), n), :]

    def body(it, carry):
        states = [list(carry[0]), list(carry[1])]
        chains = []
        for d, (d_refs, v_ref) in enumerate(((f_refs, vf_ref), (b_refs, vb_ref))):
            c = it if d == 0 else n_chunks - 1 - it
            vals = [rows(ref, c, C) for ref in d_refs[:-1]] + [rows(v_ref, c, C), rows(d_refs[-1], c, 8)[0:1]]
            for p in range(n_pairs):
                chains.append(tuple(a[:, p * LANE:(p + 1) * LANE] for a in vals) + (states[d][p],))
        res = _rwkv_scan_chunks(chains, head_masks, same_head)
        for d, y_ref in enumerate((yf_ref, yb_ref)):
            c = it if d == 0 else n_chunks - 1 - it
            y_ref[pl.ds(pl.multiple_of(c * C, C), C), :] = jnp.concatenate(
                [res[d * n_pairs + p][0] for p in range(n_pairs)], axis=1).astype(y_ref.dtype)
            states[d] = [res[d * n_pairs + p][1] for p in range(n_pairs)]
        return tuple(states[0]), tuple(states[1])

    init = tuple(tuple(ref[p] for p in range(n_pairs)) for ref in (stf_ref, stb_ref))
    st_f, st_b = lax.fori_loop(0, n_chunks, body, init)
    for p in range(n_pairs):
        stf_ref[p] = st_f[p]
        stb_ref[p] = st_b[p]


def rwkv_scan(loc_f, loc_b, v, batch, *, ts=256):
    T, W = v.shape
    S = T // batch
    ts = min(ts, S)
    ns = S // ts
    bw = LANE * SCAN_PAIRS
    fwd = lambda rows: pl.BlockSpec((rows, bw), lambda b, h, s: (b * ns + s, h))
    bwd = lambda rows: pl.BlockSpec((rows, bw), lambda b, h, s: (b * ns + ns - 1 - s, h))
    specs = lambda mk: [mk(ts)] * (N_LOCAL - 1) + [mk(ts // 8), mk(ts)]
    out = jax.ShapeDtypeStruct((T, W), BF16)
    state = pltpu.VMEM((SCAN_PAIRS, LANE, LANE), F32)
    return pl.pallas_call(
        functools.partial(_rwkv_scan_kernel, n_chunks=ts // CHUNK, n_pairs=SCAN_PAIRS),
        out_shape=(out, out),
        grid=(batch, W // bw, ns),
        in_specs=specs(fwd) + specs(bwd),
        out_specs=(fwd(ts), bwd(ts)),
        scratch_shapes=[state, state],
        compiler_params=_cparams("parallel", "parallel", "arbitrary"),
        name="rwkv_scan",
    )(*loc_f, v, *loc_b, v)


def _rwkv_out_kernel(yf_ref, yb_ref, bon_ref, g_ref, gw_ref, gb_ref, bd_ref, o_ref, *, width):
    bd = bd_ref[...]
    inv_n = 1.0 / HEAD_A
    for c in range(width // LANE):
        sl = slice(c * LANE, (c + 1) * LANE)
        y = yf_ref[:, sl].astype(F32) + yb_ref[:, sl].astype(F32)
        mean = _head_sum(y, bd) * inv_n
        d = y - mean
        var = _head_sum(d * d, bd) * inv_n
        yn = d * lax.rsqrt(var + GN_EPS) * gw_ref[:, sl] + gb_ref[:, sl] + bon_ref[:, sl]
        o_ref[:, sl] = (yn * g_ref[:, sl]).astype(o_ref.dtype)


def rwkv_out(y_f, y_b, bonus, g, gn_w, gn_b, *, tm=512):
    T, W = y_f.shape
    bd = np.kron(np.eye(LANE // HEAD_A, dtype=np.float32), np.ones((HEAD_A, HEAD_A), np.float32))
    row = pl.BlockSpec((tm, W), lambda i: (i, 0))
    vec = pl.BlockSpec((1, W), lambda i: (0, 0))
    return pl.pallas_call(
        functools.partial(_rwkv_out_kernel, width=W),
        out_shape=jax.ShapeDtypeStruct((T, W), BF16),
        grid=(T // tm,),
        in_specs=[row, row, row, row, vec, vec, pl.BlockSpec((LANE, LANE), lambda i: (0, 0))],
        out_specs=row,
        compiler_params=_cparams("parallel"),
        name="rwkv_out",
    )(y_f, y_b, bonus, g, gn_w.reshape(1, W), gn_b.reshape(1, W), jnp.asarray(bd, BF16))


def rwkv7_mixer(z, mu, w0, w2, a0, a2, g2, k_k, k_a, r_k, gn_w, gn_b, batch):
    r, v, kap, kd_f, kd_b, al_f, al_b, lw_f, lw_b, g, bonus = rwkv_prep(
        z, mu, w0, w2, a0, a2, g2, k_k, k_a, r_k.reshape(-1), batch)
    loc_f, loc_b = rwkv_local(r, v, kap, kd_f, kd_b, al_f, al_b, lw_f, lw_b)
    y_f, y_b = rwkv_scan(loc_f, loc_b, v, batch)
    return rwkv_out(y_f, y_b, bonus, g, gn_w, gn_b)


def _relu2(acc):
    r = jnp.maximum(acc, 0.0)
    return r * r


def _add(acc, res):
    return acc + res


def kernel(x, p, positions, ln1_g, w_in, rwkv_mu, rwkv_w0, rwkv_w2, rwkv_a0, rwkv_a2, rwkv_g2, rwkv_kk, rwkv_ka, rwkv_rk, rwkv_gn_w, rwkv_gn_b, hgrn_lb, hgrn_norm_g, mla_q_norm_g, mla_kv_norm_g, mla_w_uq, mla_w_ukv, w_branch, w_o, ln2_g, w_mlp1, w_mlp2, w_pe, w_pg, final_g):
    Bn, S, D = x.shape
    L = w_in.shape[0]
    T = Bn * S
    W = rwkv_w0.shape[-1]
    q_lora, kv_lora = mla_q_norm_g.shape[-1], mla_kv_norm_g.shape[-1]
    n_heads_c = mla_w_ukv.shape[-1] // (QK_NOPE + V_HEAD)
    rwkv_w = 3 * W + 2 * LORA_DECAY + 2 * LORA_AAA + LORA_GATE
    hgrn_w = 5 * W
    o_hgrn = rwkv_w
    o_cq = o_hgrn + hgrn_w
    o_ckv = o_cq + q_lora
    o_kr = o_ckv + kv_lora
    o_gate = o_kr + QK_ROPE
    half = QK_ROPE // 2

    lb_w = jax.nn.softmax(hgrn_lb.astype(F32), axis=0)
    lower_bounds = jnp.cumsum(lb_w, axis=0) - lb_w[0]
    cq_tab, sq_tab, ck_tab, sk_tab = _rope_tables(positions)

    pad3 = lambda w, mult: jnp.pad(w, ((0, 0), (0, 0), (0, (-w.shape[2]) % mult)))
    w_kr = w_in[:, :, o_kr:o_gate]
    w_all = jnp.concatenate(
        [pad3(w_in[:, :, :rwkv_w], 1024), w_in[:, :, o_hgrn:o_cq], w_in[:, :, o_gate:], w_in[:, :, o_cq:o_kr],
         pad3(w_kr, LANE), pad3(jnp.concatenate([w_kr[:, :, half:], w_kr[:, :, :half]], axis=2), LANE)],
        axis=2).astype(BF16)
    n_rwkv = rwkv_w + (-rwkv_w) % 512
    n_hgrn, n_gate = hgrn_w, w_in.shape[2] - o_gate
    n_ckv = q_lora + kv_lora + 2 * LANE
    c_hgrn = rwkv_w + (-rwkv_w) % 1024
    c_gate, c_ckv = c_hgrn + n_hgrn, c_hgrn + n_hgrn + n_gate
    w_branch_b, w_o_b = w_branch.astype(BF16), w_o.astype(BF16)
    w_mlp1_b, w_mlp2_b = w_mlp1.astype(BF16), w_mlp2.astype(BF16)
    w_pg_b, w_pe_b = w_pg.astype(BF16), w_pe.astype(BF16)
    p2 = p.reshape(L, T, -1)

    h = x.reshape(T, D)
    hn = rmsnorm(h, ln1_g[0], BF16)
    for l in range(L):
        z_rwkv = matmul(hn, w_all, l, out_dtype=F32, tm=1024, tn=n_rwkv // 2, n=n_rwkv, name="in_rwkv")
        z_hgrn = matmul(hn, w_all, l, out_dtype=F32, tm=1024, tn=1024, col0=c_hgrn, n=n_hgrn, name="in_hgrn")
        z_gate = matmul(hn, w_all, l, out_dtype=BF16, tm=1024, tn=1024, col0=c_gate, n=n_gate, name="in_gate")
        z_ckv = matmul(hn, w_all, l, out_dtype=F32, tm=1024, tn=n_ckv // 2, col0=c_ckv, n=n_ckv, name="in_ckv")

        y_a = rwkv7_mixer(z_rwkv, rwkv_mu[l], rwkv_w0[l], rwkv_w2[l], rwkv_a0[l], rwkv_a2[l], rwkv_g2[l],
                          rwkv_kk[l], rwkv_ka[l], rwkv_rk[l], rwkv_gn_w[l], rwkv_gn_b[l], Bn)
        y_b = hgrn2_mixer(z_hgrn, lower_bounds[l], hgrn_norm_g[l], Bn)
        q = mla_q_proj(z_ckv, mla_q_norm_g[l], _mla_q_weights(mla_w_uq[l], n_heads_c), cq_tab, sq_tab,
                       n_heads_c, Q_SCALE)
        k, vt = mla_kv_proj(z_ckv, q_lora, mla_kv_norm_g[l], mla_w_ukv[l].astype(BF16), ck_tab, sk_tab,
                            n_heads_c, Bn)
        y_c = mla_attention(q, k, vt, Bn, n_heads_c)

        mixed = branch_mix(y_a, y_b, y_c, w_branch_b, l, z_gate)
        h, hn2 = out_proj_norm(mixed, w_o_b, l, h, ln2_g[l])
        hid = matmul(hn2, w_mlp1_b, l, out_dtype=BF16, tm=1024, tn=1024, epilogue=_relu2, name="mlp1")
        h = matmul(hid, w_mlp2_b, l, out_dtype=F32, tm=1024, tn=1024, tk=2048, epilogue=_add,
                   extras=(h,), alias_extra=0, additive=True, name="mlp2")
        last = l == L - 1
        h, hn = ple_update_norm(h, w_pg_b, p2, w_pe_b, l, final_g if last else ln1_g[l + 1],
                                F32 if last else BF16)
    return hn.reshape(Bn, S, D)
```
